```python
import jax
import jax.numpy as jnp
from jax import lax
import numpy as np

D_MODEL = 1024
BATCH = 4
SEQ = 8192
DEPTH = 4

CHUNK = 64
PLE_DIM = 256
D_FF = 2816
RW_HEADS = 8
RW_HEAD_DIM = 64
RW_WIDTH = RW_HEADS * RW_HEAD_DIM
RW_CHUNK = CHUNK
RW_DECAY_LORA = 64
RW_AAA_LORA = 64
RW_GATE_LORA = 160
RW_GN_EPS = 64e-5
SB_HEADS = 4
SB_HEAD_DIM = 128
SB_WIDTH = SB_HEADS * SB_HEAD_DIM
SB_BLOCK = 128
HG_HEADS = 4
HG_EXPAND = 128
HG_HEAD_V = 128
HG_FWIDTH = HG_HEADS * HG_EXPAND
HG_VWIDTH = HG_HEADS * HG_HEAD_V
HG_CHUNK = 16
HG_EPS = 1e-5
N_BRANCH = 3
LN_EPS = 1e-5
ALPHA = (2 * DEPTH) ** 0.25
BETA = (8 * DEPTH) ** -0.25

RW_SPLITS = [RW_WIDTH, RW_WIDTH, RW_WIDTH, RW_DECAY_LORA, RW_AAA_LORA, RW_GATE_LORA]
RW_COLS = sum(RW_SPLITS)
SB_COLS = 3 * SB_WIDTH
HG_SPLITS = [HG_FWIDTH, HG_FWIDTH, HG_VWIDTH, HG_VWIDTH]
HG_COLS = sum(HG_SPLITS)
GATE_COLS = N_BRANCH * D_MODEL
IN_SPLITS = [RW_COLS, SB_COLS, HG_COLS, GATE_COLS]
IN_COLS = sum(IN_SPLITS)

kernel_name = "hybrid_rwkv7_stickbreak_hgrn2_macaron_deepnorm"


def split_cols(u, sizes):
    idx = np.cumsum(sizes)[:-1].tolist()
    return jnp.split(u, idx, axis=-1)


def heads(t, n_heads):
    b, s, c = t.shape
    return t.reshape(b, s, n_heads, c // n_heads)


def layer_norm(x, g, b):
    xf = x.astype(jnp.float32)
    mu = jnp.mean(xf, axis=-1, keepdims=True)
    var = jnp.mean(jnp.square(xf - mu), axis=-1, keepdims=True)
    return ((xf - mu) * lax.rsqrt(var + LN_EPS) * g + b).astype(x.dtype)


def head_norm(y, g, b, eps):
    mu = jnp.mean(y, axis=-1, keepdims=True)
    var = jnp.mean(jnp.square(y - mu), axis=-1, keepdims=True)
    y = ((y - mu) * lax.rsqrt(var + eps)).reshape(y.shape[:2] + (-1,))
    return y * g + b


def head_rms_norm(y, g, eps):
    y = y * lax.rsqrt(jnp.mean(jnp.square(y), axis=-1, keepdims=True) + eps)
    return y.reshape(y.shape[:2] + (-1,)) * g


def swiglu(h, wg, wu, wd):
    return (jax.nn.silu(h @ wg) * (h @ wu)) @ wd


def token_shift(u):
    return jnp.pad(u[:, :-1], ((0, 0), (1, 0), (0, 0)))


def rwkv7_chunked(r, log_w, k, v, a, b):
    bsz, seq, nh, n = r.shape
    nc = seq // RW_CHUNK
    f32 = jnp.float32

    def chunks(t):
        return t.astype(f32).reshape(bsz, nc, RW_CHUNK, nh, n).transpose(0, 1, 3, 2, 4)

    r, log_w, k, v, a, b = (chunks(t) for t in (r, log_w, k, v, a, b))
    g = jnp.cumsum(log_w, axis=3)
    g_last = g[..., -1:, :]
    inv = jnp.exp(-g)
    a_dec = a * jnp.exp(g - log_w)
    r_dec = r * jnp.exp(g)
    b_inv = b * inv
    k_inv = k * inv
    strict = jnp.tril(jnp.ones((RW_CHUNK, RW_CHUNK), bool), -1)
    incl = jnp.tril(jnp.ones((RW_CHUNK, RW_CHUNK), bool))

    def pair(x_t, y_s, mask):
        return jnp.where(mask, jnp.einsum('bnhtk,bnhsk->bnhts', x_t, y_s), 0.0)

    L_ab = pair(a_dec, b_inv, strict)
    L_ak = pair(a_dec, k_inv, strict)
    M_rb = pair(r_dec, b_inv, incl)
    M_rk = pair(r_dec, k_inv, incl)
    lhs = jnp.eye(RW_CHUNK, dtype=f32) - L_ab
    W1 = lax.linalg.triangular_solve(lhs, a_dec, left_side=True, lower=True, unit_diagonal=True)
    U0 = lax.linalg.triangular_solve(lhs, jnp.einsum('bnhts,bnhsv->bnhtv', L_ak, v),
                                     left_side=True, lower=True, unit_diagonal=True)
    Q_eff = r_dec + jnp.einsum('bnhts,bnhsk->bnhtk', M_rb, W1)
    Y_intra = jnp.einsum('bnhts,bnhsv->bnhtv', M_rb, U0) + jnp.einsum('bnhts,bnhsv->bnhtv', M_rk, v)
    dec = jnp.exp(g_last - g)
    b_dec = b * dec
    k_dec = k * dec
    P = jnp.einsum('bnhck,bnhcj->bnhkj', W1, b_dec) + jnp.exp(g_last)[..., 0, :, None] * jnp.eye(n, dtype=f32)
    D = jnp.einsum('bnhcv,bnhcj->bnhvj', U0, b_dec) + jnp.einsum('bnhcv,bnhcj->bnhvj', v, k_dec)

    def step(S, inp):
        P_c, D_c = inp
        return jnp.einsum('bhvk,bhkj->bhvj', S, P_c) + D_c, S

    S0 = jnp.zeros((bsz, nh, n, n), f32)
    _, S_in = lax.scan(step, S0, (jnp.moveaxis(P, 1, 0), jnp.moveaxis(D, 1, 0)))
    S_in = jnp.moveaxis(S_in, 0, 1)
    y = Y_intra + jnp.einsum('bnhck,bnhvk->bnhcv', Q_eff, S_in)
    return y.transpose(0, 1, 3, 2, 4).reshape(bsz, seq, nh, n)


def rwkv7_mix(u, mu, w0, w_up, a0, a_up, g_up, k_k, k_a, r_k, gn_g, gn_b):
    f32 = jnp.float32
    u = u + (token_shift(u) - u) * mu
    r, k, v, xw, xa, xg = split_cols(u, RW_SPLITS)
    w_log = -jax.nn.softplus(-(w0 + jnp.tanh(xw) @ w_up)) - 0.5
    log_w = -jnp.exp(w_log.astype(f32))
    a = jax.nn.sigmoid(a0 + xa @ a_up)
    g = jax.nn.sigmoid(xg) @ g_up
    kk = heads(k * k_k, RW_HEADS).astype(f32)
    kk = kk / jnp.maximum(jnp.linalg.norm(kk, axis=-1, keepdims=True), 1e-12)
    k = k * (1 + (a - 1) * k_a)
    r_h, k_h, v_h, a_h = (heads(t, RW_HEADS) for t in (r, k, v, a))
    y = rwkv7_chunked(r_h, heads(log_w, RW_HEADS), k_h, v_h, -kk, kk * a_h.astype(f32))
    y = head_norm(y, gn_g, gn_b, RW_GN_EPS)
    bonus = (jnp.sum(r_h * k_h * r_k, axis=-1, keepdims=True) * v_h).reshape(y.shape)
    return ((y + bonus) * g).astype(u.dtype)


def stick_breaking_attention(q, k, v):
    bsz, seq, nh, hd = q.shape
    f32 = jnp.float32
    qf = (q.astype(f32) * (hd ** -0.5)).transpose(0, 2, 1, 3)
    kf = k.astype(f32).transpose(0, 2, 1, 3)
    vf = v.astype(f32).transpose(0, 2, 1, 3)
    incl_upper = jnp.triu(jnp.ones((SB_BLOCK, SB_BLOCK), f32))
    diag_mask = jnp.tril(jnp.ones((SB_BLOCK, SB_BLOCK), bool), -1)
    outs = []
    for i in range(seq // SB_BLOCK):
        nk = i + 1
        lk = nk * SB_BLOCK
        qb = qf[:, :, i * SB_BLOCK:lk]
        z = jnp.einsum('bhqd,bhkd->bhqk', qb, kf[:, :, :lk]).reshape(bsz, nh, SB_BLOCK, nk, SB_BLOCK)
        mask = jnp.concatenate([jnp.ones((SB_BLOCK, i, SB_BLOCK), bool), diag_mask[:, None, :]], axis=1)
        m = jnp.where(mask, -jax.nn.softplus(z), 0.0)
        within = jnp.einsum('bhqnj,ij->bhqni', m, incl_upper)
        later = jnp.triu(jnp.ones((nk, nk), f32), 1)
        carry = jnp.einsum('bhqm,nm->bhqn', jnp.sum(m, axis=-1), later)
        att = jnp.where(mask, jnp.exp(z + within + carry[..., None]), 0.0)
        vb = vf[:, :, :lk].reshape(bsz, nh, nk, SB_BLOCK, hd)
        outs.append(jnp.einsum('bhqnk,bhnkd->bqhd', att, vb))
    return jnp.concatenate(outs, axis=1).reshape(bsz, seq, nh * hd).astype(q.dtype)


def hgrn2_mix(q_raw, f_raw, i_val, out_gate, lb, norm_g):
    bsz, seq, _ = q_raw.shape
    nc = seq // HG_CHUNK
    f32 = jnp.float32
    q = jax.nn.sigmoid(q_raw.astype(f32))
    log_f = jnp.logaddexp(jnp.log(lb), jnp.log1p(-lb) + jax.nn.log_sigmoid(f_raw.astype(f32)))
    k = -jnp.expm1(log_f)

    def chunks(t, hd):
        return t.reshape(bsz, nc, HG_CHUNK, HG_HEADS, hd).transpose(0, 1, 3, 2, 4)

    q, k, g = chunks(q, HG_EXPAND), chunks(k, HG_EXPAND), chunks(log_f, HG_EXPAND)
    v = chunks(i_val.astype(f32), HG_HEAD_V)
    G = jnp.cumsum(g, axis=3)
    causal = jnp.tril(jnp.ones((HG_CHUNK, HG_CHUNK), bool))[:, :, None]
    diff = jnp.where(causal, G[..., :, None, :] - G[..., None, :, :], -jnp.inf)
    att = jnp.sum(q[..., :, None, :] * jnp.exp(diff) * k[..., None, :, :], axis=-1)
    intra = jnp.einsum('bnhts,bnhsv->bnhtv', att, v)
    G_last = G[..., -1:, :]
    D = jnp.einsum('bnhsk,bnhsv->bnhkv', k * jnp.exp(G_last - G), v)
    gam = jnp.exp(G_last[..., 0, :])

    def step(S, inp):
        gam_c, D_c = inp
        return gam_c[..., None] * S + D_c, S

    S0 = jnp.zeros((bsz, HG_HEADS, HG_EXPAND, HG_HEAD_V), f32)
    _, S_in = lax.scan(step, S0, (jnp.moveaxis(gam, 1, 0), jnp.moveaxis(D, 1, 0)))
    S_in = jnp.moveaxis(S_in, 0, 1)
    inter = jnp.einsum('bnhtk,bnhkv->bnhtv', q * jnp.exp(G), S_in)
    o = (inter + intra).transpose(0, 1, 3, 2, 4).reshape(bsz, seq, HG_HEADS, HG_HEAD_V)
    o = head_rms_norm(o, norm_g, HG_EPS)
    return (o * jax.nn.silu(out_gate.astype(f32))).astype(q_raw.dtype)


def hybrid_mixer(h, w_in, rw_mu, rw_w0, rw_w_up, rw_a0, rw_a_up, rw_g_up, rw_k_k, rw_k_a,
                 rw_r_k, rw_gn_g, rw_gn_b, lb, hg_norm_g, w_br_rw, w_br_sb, w_br_hg, w_out):
    u_rw, u_sb, u_hg, u_gate = split_cols(h @ w_in, IN_SPLITS)
    y_rw = rwkv7_mix(u_rw, rw_mu, rw_w0, rw_w_up, rw_a0, rw_a_up, rw_g_up, rw_k_k, rw_k_a,
                     rw_r_k, rw_gn_g, rw_gn_b)
    q, k, v = (heads(t, SB_HEADS) for t in split_cols(u_sb, [SB_WIDTH] * 3))
    y_sb = stick_breaking_attention(q, k, v)
    hq, hf, hi, hg = split_cols(u_hg, HG_SPLITS)
    y_hg = hgrn2_mix(hq, hf, hi, hg, lb, hg_norm_g)
    g_rw, g_sb, g_hg = split_cols(jax.nn.sigmoid(u_gate), [D_MODEL] * N_BRANCH)
    merged = g_rw * (y_rw @ w_br_rw) + g_sb * (y_sb @ w_br_sb) + g_hg * (y_hg @ w_br_hg)
    return merged @ w_out


def setup_inputs(seed: int = 0) -> dict:
    key = jax.random.key(seed)
    ks = jax.random.split(key, 32)
    L = DEPTH
    D = D_MODEL

    def nrm(k, shape, scale):
        return jax.random.normal(k, shape, jnp.float32) * scale

    return {
        "x": nrm(ks[0], (BATCH, SEQ, D), 1.0),
        "p": nrm(ks[1], (L, BATCH, SEQ, PLE_DIM), 1.0),
        "ln_g": 1.0 + nrm(ks[2], (L, 4, D), 0.02),
        "ln_b": nrm(ks[3], (L, 4, D), 0.01),
        "ffn1_wg": nrm(ks[4], (L, D, D_FF), D ** -0.5),
        "ffn1_wu": nrm(ks[5], (L, D, D_FF), D ** -0.5),
        "ffn1_wd": nrm(ks[6], (L, D_FF, D), BETA * D_FF ** -0.5),
        "w_in": nrm(ks[7], (L, D, IN_COLS), D ** -0.5),
        "rw_mu": jax.random.uniform(ks[8], (L, RW_COLS), jnp.float32, 0.05, 0.95),
        "rw_w0": jax.random.uniform(ks[9], (L, RW_WIDTH), jnp.float32, -6.0, -1.0),
        "rw_w_up": nrm(ks[10], (L, RW_DECAY_LORA, RW_WIDTH), 0.1 * RW_DECAY_LORA ** -0.5),
        "rw_a0": nrm(ks[11], (L, RW_WIDTH), 0.1),
        "rw_a_up": nrm(ks[12], (L, RW_AAA_LORA, RW_WIDTH), 0.1 * RW_AAA_LORA ** -0.5),
        "rw_g_up": nrm(ks[13], (L, RW_GATE_LORA, RW_WIDTH), RW_GATE_LORA ** -0.5),
        "rw_k_k": 0.85 + nrm(ks[14], (L, RW_WIDTH), 0.05),
        "rw_k_a": 1.0 + nrm(ks[15], (L, RW_WIDTH), 0.05),
        "rw_r_k": nrm(ks[16], (L, RW_HEADS, RW_HEAD_DIM), 0.1),
        "rw_gn_g": 1.0 + nrm(ks[17], (L, RW_WIDTH), 0.02),
        "rw_gn_b": nrm(ks[18], (L, RW_WIDTH), 0.01),
        "hg_lb_raw": nrm(ks[19], (L, HG_FWIDTH), 0.1),
        "hg_norm_g": 1.0 + nrm(ks[20], (L, HG_VWIDTH), 0.02),
        "w_br_rw": nrm(ks[21], (L, RW_WIDTH, D), RW_WIDTH ** -0.5),
        "w_br_sb": nrm(ks[22], (L, SB_WIDTH, D), SB_WIDTH ** -0.5),
        "w_br_hg": nrm(ks[23], (L, HG_VWIDTH, D), HG_VWIDTH ** -0.5),
        "w_out": nrm(ks[24], (L, D, D), BETA * D ** -0.5),
        "ffn2_wg": nrm(ks[25], (L, D, D_FF), D ** -0.5),
        "ffn2_wu": nrm(ks[26], (L, D, D_FF), D ** -0.5),
        "ffn2_wd": nrm(ks[27], (L, D_FF, D), BETA * D_FF ** -0.5),
        "ple_gate": nrm(ks[28], (L, D, D), D ** -0.5),
        "ple_proj": nrm(ks[29], (L, PLE_DIM, D), BETA * PLE_DIM ** -0.5),
    }


def reference(x, p, ln_g, ln_b, ffn1_wg, ffn1_wu, ffn1_wd, w_in, rw_mu, rw_w0, rw_w_up, rw_a0,
              rw_a_up, rw_g_up, rw_k_k, rw_k_a, rw_r_k, rw_gn_g, rw_gn_b, hg_lb_raw, hg_norm_g,
              w_br_rw, w_br_sb, w_br_hg, w_out, ffn2_wg, ffn2_wu, ffn2_wd, ple_gate, ple_proj):
    lb_sm = jax.nn.softmax(hg_lb_raw.astype(jnp.float32), axis=0)
    lb_all = jnp.cumsum(lb_sm, axis=0)
    lb_all = lb_all - lb_all[0:1]
    h = x
    for i in range(DEPTH):
        h = layer_norm(ALPHA * h + 0.5 * swiglu(h, ffn1_wg[i], ffn1_wu[i], ffn1_wd[i]), ln_g[i, 0], ln_b[i, 0])
        mix = hybrid_mixer(h, w_in[i], rw_mu[i], rw_w0[i], rw_w_up[i], rw_a0[i], rw_a_up[i], rw_g_up[i],
                           rw_k_k[i], rw_k_a[i], rw_r_k[i], rw_gn_g[i], rw_gn_b[i], lb_all[i],
                           hg_norm_g[i], w_br_rw[i], w_br_sb[i], w_br_hg[i], w_out[i])
        h = layer_norm(ALPHA * h + mix, ln_g[i, 1], ln_b[i, 1])
        h = layer_norm(ALPHA * h + 0.5 * swiglu(h, ffn2_wg[i], ffn2_wu[i], ffn2_wd[i]), ln_g[i, 2], ln_b[i, 2])
        ple = jax.nn.sigmoid(h @ ple_gate[i]) * (p[i] @ ple_proj[i])
        h = layer_norm(ALPHA * h + ple, ln_g[i, 3], ln_b[i, 3])
    return h
```

```python
import functools

import jax
import jax.numpy as jnp
from jax import lax
from jax.experimental import pallas as pl
from jax.experimental.pallas import tpu as pltpu

F32 = jnp.float32
BF16 = jnp.bfloat16

D_MODEL = 1024
DEPTH = 4
PLE_DIM = 256
D_FF = 2816
RW_HEADS = 8
RW_HEAD_DIM = 64
RW_WIDTH = RW_HEADS * RW_HEAD_DIM
RW_CHUNK = 64
RW_DECAY_LORA = 64
RW_AAA_LORA = 64
RW_GATE_LORA = 160
RW_GN_EPS = 64e-5
RW_QUAD = 256
SB_HEADS = 4
SB_HEAD_DIM = 128
SB_WIDTH = SB_HEADS * SB_HEAD_DIM
HG_HEADS = 4
HG_DIM = 128
HG_WIDTH = HG_HEADS * HG_DIM
HG_CHUNK = 16
HG_EPS = 1e-5
LN_EPS = 1e-5
ALPHA = (2 * DEPTH) ** 0.25

LANES = 128
RW_LORA_PAD = (128, 128, 256)
RW_U_COLS = 3 * RW_WIDTH + sum(RW_LORA_PAD)
VMEM_LIMIT = 56 * 1024 * 1024


def _resident(shape):
    return pl.BlockSpec(shape, lambda *_: (0,) * len(shape), pipeline_mode=pl.Buffered(1))


def _dot(a, b):
    return jnp.dot(a, b, preferred_element_type=F32)


def _dot_nt(a, b):
    return lax.dot_general(a, b, (((1,), (1,)), ((), ())), preferred_element_type=F32)


def _dot_tn(a, b):
    return lax.dot_general(a, b, (((0,), (0,)), ((), ())), preferred_element_type=F32)


def _split2(x):
    hi = x.astype(BF16)
    lo = (x - hi.astype(F32)).astype(BF16)
    return hi, lo


def _split3(x):
    hi = x.astype(BF16)
    r = x - hi.astype(F32)
    mid = r.astype(BF16)
    lo = (r - mid.astype(F32)).astype(BF16)
    return hi, mid, lo


def _sigmoid(x):
    return 1.0 / (1.0 + jnp.exp(-x))


def _softplus(x):
    return jnp.maximum(x, 0.0) + jnp.log1p(jnp.exp(-jnp.abs(x)))


def _layer_norm(x, g, b):
    mu = jnp.mean(x, axis=-1, keepdims=True)
    xc = x - mu
    var = jnp.mean(xc * xc, axis=-1, keepdims=True)
    return xc * lax.rsqrt(var + LN_EPS) * g + b


def _ffn_ln_kernel(h_ref, wg_ref, wu_ref, wd_ref, lng_ref, lnb_ref, o_ref):
    h = h_ref[...]
    hb = h.astype(BF16)
    g = _dot(hb, wg_ref[...])
    u = _dot(hb, wu_ref[...])
    a = (g * _sigmoid(g) * u).astype(BF16)
    y = _dot(a, wd_ref[...])
    o_ref[...] = _layer_norm(ALPHA * h + 0.5 * y, lng_ref[...], lnb_ref[...])


def _ffn_ln(h, wg, wu, wd, ln_g, ln_b, tm):
    n = h.shape[0]
    row = pl.BlockSpec((tm, D_MODEL), lambda i: (i, 0))
    return pl.pallas_call(
        _ffn_ln_kernel,
        grid=(n // tm,),
        in_specs=[row, _resident((D_MODEL, D_FF)), _resident((D_MODEL, D_FF)),
                  _resident((D_FF, D_MODEL)), _resident((1, D_MODEL)), _resident((1, D_MODEL))],
        out_specs=row,
        out_shape=jax.ShapeDtypeStruct((n, D_MODEL), F32),
        compiler_params=pltpu.CompilerParams(dimension_semantics=("arbitrary",),
                                             vmem_limit_bytes=VMEM_LIMIT),
        name="ffn_ln",
    )(h, wg, wu, wd, ln_g, ln_b)


def _in_proj_kernel(h_ref, wrw_ref, wsb_ref, whg_ref, wgt_ref,
                    urw_ref, q_ref, k_ref, v_ref, uhg_ref, gate_ref):
    hb = h_ref[...].astype(BF16)
    urw_ref[...] = _dot(hb, wrw_ref[...])
    sb = _dot(hb, wsb_ref[...])
    q_ref[...] = (sb[:, :SB_WIDTH] * (SB_HEAD_DIM ** -0.5)).astype(BF16)
    k_ref[...] = sb[:, SB_WIDTH:2 * SB_WIDTH].astype(BF16)
    v_ref[...] = sb[:, 2 * SB_WIDTH:].astype(BF16)
    uhg_ref[...] = _dot(hb, whg_ref[...])
    gate_ref[...] = _sigmoid(_dot(hb, wgt_ref[...]))


def _in_proj(h, w_rw, w_sb, w_hg, w_gate, tm):
    n = h.shape[0]

    def row(c):
        return pl.BlockSpec((tm, c), lambda i: (i, 0))

    return pl.pallas_call(
        _in_proj_kernel,
        grid=(n // tm,),
        in_specs=[row(D_MODEL), _resident(w_rw.shape), _resident(w_sb.shape),
                  _resident(w_hg.shape), _resident(w_gate.shape)],
        out_specs=[row(RW_U_COLS), row(SB_WIDTH), row(SB_WIDTH), row(SB_WIDTH),
                   row(4 * HG_WIDTH), row(3 * D_MODEL)],
        out_shape=[jax.ShapeDtypeStruct((n, RW_U_COLS), F32),
                   jax.ShapeDtypeStruct((n, SB_WIDTH), BF16),
                   jax.ShapeDtypeStruct((n, SB_WIDTH), BF16),
                   jax.ShapeDtypeStruct((n, SB_WIDTH), BF16),
                   jax.ShapeDtypeStruct((n, 4 * HG_WIDTH), F32),
                   jax.ShapeDtypeStruct((n, 3 * D_MODEL), F32)],
        compiler_params=pltpu.CompilerParams(dimension_semantics=("arbitrary",),
                                             vmem_limit_bytes=VMEM_LIMIT),
        name="in_proj",
    )(h, w_rw, w_sb, w_hg, w_gate)


def _merge_ln_kernel(h_ref, yrw_ref, ysb_ref, yhg_ref, gate_ref, wrw_ref, wsb_ref, whg_ref,
                     wout_ref, lng_ref, lnb_ref, o_ref):
    gate = gate_ref[...]
    merged = (gate[:, :D_MODEL] * _dot(yrw_ref[...], wrw_ref[...])
              + gate[:, D_MODEL:2 * D_MODEL] * _dot(ysb_ref[...], wsb_ref[...])
              + gate[:, 2 * D_MODEL:] * _dot(yhg_ref[...], whg_ref[...]))
    mix = _dot(merged.astype(BF16), wout_ref[...])
    o_ref[...] = _layer_norm(ALPHA * h_ref[...] + mix, lng_ref[...], lnb_ref[...])


def _merge_ln(h, y_rw, y_sb, y_hg, gate, w_br_rw, w_br_sb, w_br_hg, w_out, ln_g, ln_b, tm):
    n = h.shape[0]

    def row(c):
        return pl.BlockSpec((tm, c), lambda i: (i, 0))

    return pl.pallas_call(
        _merge_ln_kernel,
        grid=(n // tm,),
        in_specs=[row(D_MODEL), row(RW_WIDTH), row(SB_WIDTH), row(HG_WIDTH), row(3 * D_MODEL),
                  _resident(w_br_rw.shape), _resident(w_br_sb.shape), _resident(w_br_hg.shape),
                  _resident(w_out.shape), _resident((1, D_MODEL)), _resident((1, D_MODEL))],
        out_specs=row(D_MODEL),
        out_shape=jax.ShapeDtypeStruct((n, D_MODEL), F32),
        compiler_params=pltpu.CompilerParams(dimension_semantics=("arbitrary",),
                                             vmem_limit_bytes=VMEM_LIMIT),
        name="merge_ln",
    )(h, y_rw, y_sb, y_hg, gate, w_br_rw, w_br_sb, w_br_hg, w_out, ln_g, ln_b)


def _ple_ln_kernel(h_ref, p_ref, wg_ref, wp_ref, lng_ref, lnb_ref, o_ref):
    h = h_ref[...]
    gate = _sigmoid(_dot(h.astype(BF16), wg_ref[...]))
    ple = gate * _dot(p_ref[...].astype(BF16), wp_ref[...])
    o_ref[...] = _layer_norm(ALPHA * h + ple, lng_ref[...], lnb_ref[...])


def _ple_ln(h, p, w_gate, w_proj, ln_g, ln_b, tm):
    n = h.shape[0]

    def row(c):
        return pl.BlockSpec((tm, c), lambda i: (i, 0))

    return pl.pallas_call(
        _ple_ln_kernel,
        grid=(n // tm,),
        in_specs=[row(D_MODEL), row(PLE_DIM), _resident(w_gate.shape), _resident(w_proj.shape),
                  _resident((1, D_MODEL)), _resident((1, D_MODEL))],
        out_specs=row(D_MODEL),
        out_shape=jax.ShapeDtypeStruct((n, D_MODEL), F32),
        compiler_params=pltpu.CompilerParams(dimension_semantics=("arbitrary",),
                                             vmem_limit_bytes=VMEM_LIMIT),
        name="ple_ln",
    )(h, p, w_gate, w_proj, ln_g, ln_b)


def _sb_kernel(q_ref, k_ref, v_ref, o_ref, acc_ref, carry_ref, *, blk):
    i = pl.program_id(2)
    q = q_ref[...]
    row = lax.broadcasted_iota(jnp.int32, (blk, blk), 0)
    col = lax.broadcasted_iota(jnp.int32, (blk, blk), 1)
    suffix = jnp.where(row >= col, 1.0, 0.0).astype(BF16)
    strict = col < row

    def block(jb, masked):
        start = pl.multiple_of(jb * blk, blk)
        kb = k_ref[pl.ds(start, blk), :]
        vb = v_ref[pl.ds(start, blk), :]
        z = _dot_nt(q, kb)
        m = -_softplus(z)
        if masked:
            m = jnp.where(strict, m, 0.0)
        m_hi, m_lo = _split2(m)
        within = _dot(m_hi, suffix) + _dot(m_lo, suffix)
        att = jnp.exp(z + within + carry_ref[...])
        if masked:
            att = jnp.where(strict, att, 0.0)
        acc_ref[...] += _dot(att.astype(BF16), vb)
        carry_ref[...] += within[:, 0:1]

    acc_ref[...] = jnp.zeros_like(acc_ref)
    carry_ref[...] = jnp.zeros_like(carry_ref)
    block(i, True)

    def body(t, c):
        block(i - 1 - t, False)
        return c

    lax.fori_loop(0, i, body, 0)
    o_ref[...] = acc_ref[...].astype(o_ref.dtype)


def _sb_attention(q, k, v, bsz, seq, blk):
    n = bsz * seq
    nq = seq // blk
    qspec = pl.BlockSpec((blk, SB_HEAD_DIM), lambda b, h, i: (b * nq + i, h))
    kvspec = pl.BlockSpec((seq, SB_HEAD_DIM), lambda b, h, i: (b, h))
    return pl.pallas_call(
        functools.partial(_sb_kernel, blk=blk),
        grid=(bsz, SB_HEADS, nq),
        in_specs=[qspec, kvspec, kvspec],
        out_specs=qspec,
        out_shape=jax.ShapeDtypeStruct((n, SB_WIDTH), BF16),
        scratch_shapes=[pltpu.VMEM((blk, SB_HEAD_DIM), F32), pltpu.VMEM((blk, 1), F32)],
        compiler_params=pltpu.CompilerParams(
            dimension_semantics=("arbitrary", "arbitrary", "arbitrary"),
            vmem_limit_bytes=VMEM_LIMIT),
        name="stick_breaking",
    )(q, k, v)


def _hgrn2_kernel(u_ref, lbraw_ref, ng_ref, o_ref, st_ref, *, layer, tile):
    @pl.when(pl.program_id(1) == 0)
    def _():
        st_ref[...] = jnp.zeros_like(st_ref)

    raw = lbraw_ref[...]
    e = jnp.exp(raw - jnp.max(raw, axis=0, keepdims=True))
    sm = e / jnp.sum(e, axis=0, keepdims=True)
    lb = jnp.zeros((1, HG_WIDTH), F32)
    for l in range(1, layer + 1):
        lb = lb + sm[l:l + 1]

    u = u_ref[...]
    q = _sigmoid(u[:, :HG_WIDTH])
    f_raw = u[:, HG_WIDTH:2 * HG_WIDTH]
    val = u[:, 2 * HG_WIDTH:3 * HG_WIDTH]
    out_gate = u[:, 3 * HG_WIDTH:]

    log_sig = jnp.minimum(f_raw, 0.0) - jnp.log1p(jnp.exp(-jnp.abs(f_raw)))
    if layer == 0:
        log_f = log_sig
    else:
        a = jnp.log(lb)
        b = jnp.log1p(-lb) + log_sig
        log_f = jnp.maximum(a, b) + jnp.log1p(jnp.exp(-jnp.abs(a - b)))
    kg = (1.0 - lb) * _sigmoid(-f_raw)

    pos = lax.broadcasted_iota(jnp.int32, (tile, HG_WIDTH), 0) % HG_CHUNK
    g = log_f
    d = 1
    while d < HG_CHUNK:
        g = g + jnp.where(pos >= d, pltpu.roll(g, d, 0), 0.0)
        d *= 2

    intra = [jnp.zeros((tile, HG_DIM), F32) for _ in range(HG_HEADS)]
    for d in range(HG_CHUNK):
        if d == 0:
            w = q * kg
            vs = val
        else:
            valid = pos >= d
            dec = jnp.exp(jnp.where(valid, g - pltpu.roll(g, d, 0), 0.0))
            w = jnp.where(valid, q * dec * pltpu.roll(kg, d, 0), 0.0)
            vs = pltpu.roll(val, d, 0)
        for h in range(HG_HEADS):
            sl = slice(h * HG_DIM, (h + 1) * HG_DIM)
            intra[h] = intra[h] + jnp.sum(w[:, sl], axis=-1, keepdims=True) * vs[:, sl]

    qg = (q * jnp.exp(g)).astype(BF16)
    ng = ng_ref[...]
    for c in range(tile // HG_CHUNK):
        rows = slice(c * HG_CHUNK, (c + 1) * HG_CHUNK)
        g_c = g[rows]
        g_last = g_c[HG_CHUNK - 1:HG_CHUNK]
        kd = (kg[rows] * jnp.exp(g_last - g_c)).astype(BF16)
        gam = jnp.exp(g_last)
        v_c = val[rows].astype(BF16)
        for h in range(HG_HEADS):
            sl = slice(h * HG_DIM, (h + 1) * HG_DIM)
            st = st_ref[h]
            inter = _dot_nt(qg[rows, sl], st.astype(BF16))
            o = inter + intra[h][rows]
            o = o * lax.rsqrt(jnp.mean(o * o, axis=-1, keepdims=True) + HG_EPS) * ng[:, sl]
            gate = out_gate[rows, sl]
            o_ref[rows, sl] = (o * (gate * _sigmoid(gate))).astype(o_ref.dtype)
            st_ref[h] = st * gam[:, sl] + _dot_tn(v_c[:, sl], kd[:, sl])


def _hgrn2(u_hg, lb_raw, norm_g, layer, bsz, seq, tile):
    n = bsz * seq
    nt = seq // tile
    return pl.pallas_call(
        functools.partial(_hgrn2_kernel, layer=layer, tile=tile),
        grid=(bsz, nt),
        in_specs=[pl.BlockSpec((tile, 4 * HG_WIDTH), lambda b, t: (b * nt + t, 0)),
                  _resident(lb_raw.shape), _resident((1, HG_WIDTH))],
        out_specs=pl.BlockSpec((tile, HG_WIDTH), lambda b, t: (b * nt + t, 0)),
        out_shape=jax.ShapeDtypeStruct((n, HG_WIDTH), BF16),
        scratch_shapes=[pltpu.VMEM((HG_HEADS, HG_DIM, HG_DIM), F32)],
        compiler_params=pltpu.CompilerParams(dimension_semantics=("arbitrary", "arbitrary"),
                                             vmem_limit_bytes=VMEM_LIMIT),
        name="hgrn2",
    )(u_hg, lb_raw, norm_g)


def _quad_mask(rows):
    r = lax.broadcasted_iota(jnp.int32, (rows, RW_QUAD), 0)
    c = lax.broadcasted_iota(jnp.int32, (rows, RW_QUAD), 1)
    return r, c


def _block_diag(y, same_head):
    y4 = jnp.concatenate([y, y, y, y], axis=0)
    return jnp.where(same_head, y4, jnp.zeros_like(y4))


def _rwkv7_kernel(u_ref, mu_ref, w0_ref, wup_ref, a0_ref, aup_ref, gup_ref, kk_ref, ka_ref,
                  rk_ref, gng_ref, gnb_ref, o_ref, prev_ref, st_ref):
    C = RW_CHUNK
    W = RW_WIDTH

    @pl.when(pl.program_id(1) == 0)
    def _():
        prev_ref[...] = jnp.zeros_like(prev_ref)
        st_ref[...] = jnp.zeros_like(st_ref)

    u = u_ref[...]
    trow = lax.broadcasted_iota(jnp.int32, u.shape, 0)
    shifted = jnp.where(trow == 0, prev_ref[...], pltpu.roll(u, 1, 0))
    prev_ref[...] = u[C - 1:C]
    u = u + (shifted - u) * mu_ref[...]

    r = u[:, :W]
    k = u[:, W:2 * W]
    v = u[:, 2 * W:3 * W]
    o0 = 3 * W
    xw = u[:, o0:o0 + RW_LORA_PAD[0]]
    xa = u[:, o0 + RW_LORA_PAD[0]:o0 + RW_LORA_PAD[0] + RW_LORA_PAD[1]]
    xg = u[:, o0 + RW_LORA_PAD[0] + RW_LORA_PAD[1]:]

    w_log = -_softplus(-(w0_ref[...] + _dot(jnp.tanh(xw).astype(BF16), wup_ref[...]))) - 0.5
    log_w = -jnp.exp(w_log)
    a = _sigmoid(a0_ref[...] + _dot(xa.astype(BF16), aup_ref[...]))
    gate = _dot(_sigmoid(xg).astype(BF16), gup_ref[...])

    hr = lax.broadcasted_iota(jnp.int32, (W, W), 0) // RW_HEAD_DIM
    hc = lax.broadcasted_iota(jnp.int32, (W, W), 1) // RW_HEAD_DIM
    head_ones = jnp.where(hr == hc, 1.0, 0.0).astype(BF16)

    def head_sum(x):
        hi, mid, lo = _split3(x)
        return _dot(hi, head_ones) + _dot(mid, head_ones) + _dot(lo, head_ones)

    kk = k * kk_ref[...]
    kk = kk / jnp.maximum(jnp.sqrt(head_sum(kk * kk)), 1e-12)
    k = k * (1.0 + (a - 1.0) * ka_ref[...])
    bonus = head_sum(r * k * rk_ref[...]) * v
    a_vec = -kk
    b_vec = kk * a

    g = log_w
    wrow = lax.broadcasted_iota(jnp.int32, (C, W), 0)
    d = 1
    while d < C:
        g = g + jnp.where(wrow >= d, pltpu.roll(g, d, 0), 0.0)
        d *= 2
    g_last = g[C - 1:C]
    inv = jnp.exp(-g)
    a_dec = a_vec * jnp.exp(g - log_w)
    r_dec = r * jnp.exp(g)
    b_inv = b_vec * inv
    k_inv = k * inv
    dec = jnp.exp(g_last - g)
    b_dec = b_vec * dec
    k_dec = k * dec
    e_last = jnp.exp(g_last)

    br, bc = _quad_mask(RW_QUAD)
    same_head = (br // RW_HEAD_DIM) == (bc // RW_HEAD_DIM)
    eye_bd = br == bc
    tr, tc = _quad_mask(C)
    s_idx = tc % RW_HEAD_DIM
    strict = s_idx < tr
    incl = s_idx <= tr
    eye_q = jnp.where(s_idx == tr, 1.0, 0.0)

    def bd(y):
        return _block_diag(y.astype(BF16), same_head)

    def bd2(y):
        hi, lo = _split2(y)
        return _block_diag(hi, same_head), _block_diag(lo, same_head)

    def mm3(x, ybd):
        x_hi, x_lo = _split2(x)
        y_hi, y_lo = ybd
        return _dot(x_hi, y_hi) + _dot(x_hi, y_lo) + _dot(x_lo, y_hi)

    ys = []
    for qd in range(W // RW_QUAD):
        sl = slice(qd * RW_QUAD, (qd + 1) * RW_QUAD)
        ar = jnp.concatenate([a_dec[:, sl], r_dec[:, sl]], axis=0).astype(BF16)
        lm_b = _dot_nt(ar, bd(b_inv[:, sl]))
        lm_k = _dot_nt(ar, bd(k_inv[:, sl]))
        l_ab = jnp.where(strict, lm_b[:C], 0.0)
        m_rb = jnp.where(incl, lm_b[C:], 0.0).astype(BF16)
        l_ak = jnp.where(strict, lm_k[:C], 0.0).astype(BF16)
        m_rk = jnp.where(incl, lm_k[C:], 0.0).astype(BF16)
        v_bd = bd(v[:, sl])
        lv = _dot(l_ak, v_bd)

        inv_l = eye_q + l_ab
        lp = l_ab
        for _ in range(5):
            lp = mm3(lp, bd2(lp))
            inv_l = inv_l + mm3(inv_l, bd2(lp))
        w1 = mm3(inv_l, bd2(a_dec[:, sl]))
        u0 = mm3(inv_l, bd2(lv))

        q_eff = r_dec[:, sl] + _dot(m_rb, bd(w1))
        y_intra = _dot(m_rb, bd(u0)) + _dot(m_rk, v_bd)
        w1b = w1.astype(BF16)
        u0b = u0.astype(BF16)
        b_dec_b = b_dec[:, sl].astype(BF16)
        p_bd = jnp.where(same_head, _dot_tn(w1b, b_dec_b), 0.0) + jnp.where(eye_bd, e_last[:, sl], 0.0)
        d_bd = jnp.where(same_head, _dot_tn(u0b, b_dec_b)
                         + _dot_tn(v[:, sl].astype(BF16), k_dec[:, sl].astype(BF16)), 0.0)
        st = st_ref[qd]
        ys.append(y_intra + _dot_nt(q_eff.astype(BF16), st.astype(BF16)))
        st_ref[qd] = _dot(st.astype(BF16), p_bd.astype(BF16)) + d_bd

    y = jnp.concatenate(ys, axis=1)
    mean = head_sum(y) * (1.0 / RW_HEAD_DIM)
    yc = y - mean
    var = head_sum(yc * yc) * (1.0 / RW_HEAD_DIM)
    y = yc * lax.rsqrt(var + RW_GN_EPS) * gng_ref[...] + gnb_ref[...]
    o_ref[...] = ((y + bonus) * gate).astype(o_ref.dtype)


def _rwkv7(u_rw, mu, w0, w_up, a0, a_up, g_up, k_k, k_a, r_k, gn_g, gn_b, bsz, seq):
    n = bsz * seq
    nc = seq // RW_CHUNK
    vec = _resident((1, RW_WIDTH))
    return pl.pallas_call(
        _rwkv7_kernel,
        grid=(bsz, nc),
        in_specs=[pl.BlockSpec((RW_CHUNK, RW_U_COLS), lambda b, c: (b * nc + c, 0)),
                  _resident((1, RW_U_COLS)), vec, _resident(w_up.shape), vec,
                  _resident(a_up.shape), _resident(g_up.shape), vec, vec, vec, vec, vec],
        out_specs=pl.BlockSpec((RW_CHUNK, RW_WIDTH), lambda b, c: (b * nc + c, 0)),
        out_shape=jax.ShapeDtypeStruct((n, RW_WIDTH), BF16),
        scratch_shapes=[pltpu.VMEM((1, RW_U_COLS), F32),
                        pltpu.VMEM((RW_WIDTH // RW_QUAD, RW_QUAD, RW_QUAD), F32)],
        compiler_params=pltpu.CompilerParams(dimension_semantics=("arbitrary", "arbitrary"),
                                             vmem_limit_bytes=VMEM_LIMIT),
        name="rwkv7",
    )(u_rw, mu, w0, w_up, a0, a_up, g_up, k_k, k_a, r_k, gn_g, gn_b)


def _pad_cols(w, width):
    return jnp.pad(w, ((0, 0), (0, width - w.shape[1])))


def _pad_rows(w, height):
    return jnp.pad(w, ((0, height - w.shape[0]), (0, 0)))


def _split_rw_cols(w):
    o = 3 * RW_WIDTH
    parts = [w[:, :o]]
    for size, pad in zip((RW_DECAY_LORA, RW_AAA_LORA, RW_GATE_LORA), RW_LORA_PAD):
        parts.append(_pad_cols(w[:, o:o + size], pad))
        o += size
    return jnp.concatenate(parts, axis=1)


def kernel(x, p, ln_g, ln_b, ffn1_wg, ffn1_wu, ffn1_wd, w_in, rw_mu, rw_w0, rw_w_up, rw_a0, rw_a_up, rw_g_up, rw_k_k, rw_k_a, rw_r_k, rw_gn_g, rw_gn_b, hg_lb_raw, hg_norm_g, w_br_rw, w_br_sb, w_br_hg, w_out, ffn2_wg, ffn2_wu, ffn2_wd, ple_gate, ple_proj):
    bsz, seq, _ = x.shape
    n = bsz * seq
    tm = min(256, n)
    sb_blk = min(256, seq)
    hg_tile = min(128, seq)
    rw_cols = 3 * RW_WIDTH + RW_DECAY_LORA + RW_AAA_LORA + RW_GATE_LORA
    sb_end = rw_cols + 3 * SB_WIDTH
    hg_end = sb_end + 4 * HG_WIDTH

    h = x.reshape(n, D_MODEL)
    for i in range(DEPTH):
        def vec(a):
            return a[i].reshape(1, -1)

        h = _ffn_ln(h, ffn1_wg[i].astype(BF16), ffn1_wu[i].astype(BF16), ffn1_wd[i].astype(BF16),
                    vec(ln_g[:, 0]), vec(ln_b[:, 0]), tm)

        wi = w_in[i]
        u_rw, q, k, v, u_hg, gate = _in_proj(
            h, _split_rw_cols(wi[:, :rw_cols]).astype(BF16), wi[:, rw_cols:sb_end].astype(BF16),
            wi[:, sb_end:hg_end].astype(BF16), wi[:, hg_end:].astype(BF16), tm)

        y_rw = _rwkv7(u_rw, _split_rw_cols(rw_mu[i].reshape(1, -1)), vec(rw_w0),
                      _pad_rows(rw_w_up[i], RW_LORA_PAD[0]).astype(BF16), vec(rw_a0),
                      _pad_rows(rw_a_up[i], RW_LORA_PAD[1]).astype(BF16),
                      _pad_rows(rw_g_up[i], RW_LORA_PAD[2]).astype(BF16),
                      vec(rw_k_k), vec(rw_k_a), vec(rw_r_k), vec(rw_gn_g), vec(rw_gn_b), bsz, seq)
        y_sb = _sb_attention(q, k, v, bsz, seq, sb_blk)
        y_hg = _hgrn2(u_hg, hg_lb_raw, vec(hg_norm_g), i, bsz, seq, hg_tile)

        h = _merge_ln(h, y_rw, y_sb, y_hg, gate, w_br_rw[i].astype(BF16), w_br_sb[i].astype(BF16),
                      w_br_hg[i].astype(BF16), w_out[i].astype(BF16),
                      vec(ln_g[:, 1]), vec(ln_b[:, 1]), tm)
        h = _ffn_ln(h, ffn2_wg[i].astype(BF16), ffn2_wu[i].astype(BF16), ffn2_wd[i].astype(BF16),
                    vec(ln_g[:, 2]), vec(ln_b[:, 2]), tm)
        h = _ple_ln(h, p[i].reshape(n, PLE_DIM), ple_gate[i].astype(BF16), ple_proj[i].astype(BF16),
                    vec(ln_g[:, 3]), vec(ln_b[:, 3]), tm)
    return h.reshape(bsz, seq, D_MODEL)
```

```python
import functools

import jax
import jax.numpy as jnp
from jax import lax
from jax.experimental import pallas as pl
from jax.experimental.pallas import tpu as pltpu

F32 = jnp.float32
BF16 = jnp.bfloat16

D_MODEL = 1024
DEPTH = 4
PLE_DIM = 256
D_FF = 2816
RW_HEADS = 8
RW_HEAD_DIM = 64
RW_WIDTH = RW_HEADS * RW_HEAD_DIM
RW_CHUNK = 64
RW_DECAY_LORA = 64
RW_AAA_LORA = 64
RW_GATE_LORA = 160
RW_GN_EPS = 64e-5
RW_QUAD = 256
SB_HEADS = 4
SB_HEAD_DIM = 128
SB_WIDTH = SB_HEADS * SB_HEAD_DIM
SB_DEAD_LOG = -105.0
HG_HEADS = 4
HG_DIM = 128
HG_WIDTH = HG_HEADS * HG_DIM
HG_CHUNK = 16
HG_EPS = 1e-5
LN_EPS = 1e-5
ALPHA = (2 * DEPTH) ** 0.25

LANES = 128
RW_LORA_PAD = (128, 128, 256)
RW_U_COLS = 3 * RW_WIDTH + sum(RW_LORA_PAD)
VMEM_LIMIT = 56 * 1024 * 1024


def _resident(shape):
    return pl.BlockSpec(shape, lambda *_: (0,) * len(shape), pipeline_mode=pl.Buffered(1))


def _dot(a, b):
    return jnp.dot(a, b, preferred_element_type=F32)


def _dot_nt(a, b):
    return lax.dot_general(a, b, (((1,), (1,)), ((), ())), preferred_element_type=F32)


def _dot_tn(a, b):
    return lax.dot_general(a, b, (((0,), (0,)), ((), ())), preferred_element_type=F32)


def _split2(x):
    hi = x.astype(BF16)
    lo = (x - hi.astype(F32)).astype(BF16)
    return hi, lo


def _split3(x):
    hi = x.astype(BF16)
    r = x - hi.astype(F32)
    mid = r.astype(BF16)
    lo = (r - mid.astype(F32)).astype(BF16)
    return hi, mid, lo


def _sigmoid(x):
    return 1.0 / (1.0 + jnp.exp(-x))


def _softplus(x):
    return jnp.maximum(x, 0.0) + jnp.log1p(jnp.exp(-jnp.abs(x)))


def _layer_norm(x, g, b):
    mu = jnp.mean(x, axis=-1, keepdims=True)
    xc = x - mu
    var = jnp.mean(xc * xc, axis=-1, keepdims=True)
    return xc * lax.rsqrt(var + LN_EPS) * g + b


def _ffn_ln_kernel(h_ref, wg_ref, wu_ref, wd_ref, lng_ref, lnb_ref, o_ref):
    h = h_ref[...]
    hb = h.astype(BF16)
    g = _dot(hb, wg_ref[...])
    u = _dot(hb, wu_ref[...])
    a = (g * _sigmoid(g) * u).astype(BF16)
    y = _dot(a, wd_ref[...])
    o_ref[...] = _layer_norm(ALPHA * h + 0.5 * y, lng_ref[...], lnb_ref[...])


def _ffn_ln(h, wg, wu, wd, ln_g, ln_b, tm):
    n = h.shape[0]
    row = pl.BlockSpec((tm, D_MODEL), lambda i: (i, 0))
    return pl.pallas_call(
        _ffn_ln_kernel,
        grid=(n // tm,),
        in_specs=[row, _resident((D_MODEL, D_FF)), _resident((D_MODEL, D_FF)),
                  _resident((D_FF, D_MODEL)), _resident((1, D_MODEL)), _resident((1, D_MODEL))],
        out_specs=row,
        out_shape=jax.ShapeDtypeStruct((n, D_MODEL), F32),
        compiler_params=pltpu.CompilerParams(dimension_semantics=("arbitrary",),
                                             vmem_limit_bytes=VMEM_LIMIT),
        name="ffn_ln",
    )(h, wg, wu, wd, ln_g, ln_b)


def _in_proj_kernel(h_ref, wrw_ref, wsb_ref, whg_ref, wgt_ref,
                    urw_ref, q_ref, k_ref, v_ref, uhg_ref, gate_ref):
    hb = h_ref[...].astype(BF16)
    urw_ref[...] = _dot(hb, wrw_ref[...])
    sb = _dot(hb, wsb_ref[...])
    q_ref[...] = (sb[:, :SB_WIDTH] * (SB_HEAD_DIM ** -0.5)).astype(BF16)
    k_ref[...] = sb[:, SB_WIDTH:2 * SB_WIDTH].astype(BF16)
    v_ref[...] = sb[:, 2 * SB_WIDTH:].astype(BF16)
    uhg_ref[...] = _dot(hb, whg_ref[...])
    gate_ref[...] = _sigmoid(_dot(hb, wgt_ref[...]))


def _in_proj(h, w_rw, w_sb, w_hg, w_gate, tm):
    n = h.shape[0]

    def row(c):
        return pl.BlockSpec((tm, c), lambda i: (i, 0))

    return pl.pallas_call(
        _in_proj_kernel,
        grid=(n // tm,),
        in_specs=[row(D_MODEL), _resident(w_rw.shape), _resident(w_sb.shape),
                  _resident(w_hg.shape), _resident(w_gate.shape)],
        out_specs=[row(RW_U_COLS), row(SB_WIDTH), row(SB_WIDTH), row(SB_WIDTH),
                   row(4 * HG_WIDTH), row(3 * D_MODEL)],
        out_shape=[jax.ShapeDtypeStruct((n, RW_U_COLS), F32),
                   jax.ShapeDtypeStruct((n, SB_WIDTH), BF16),
                   jax.ShapeDtypeStruct((n, SB_WIDTH), BF16),
                   jax.ShapeDtypeStruct((n, SB_WIDTH), BF16),
                   jax.ShapeDtypeStruct((n, 4 * HG_WIDTH), F32),
                   jax.ShapeDtypeStruct((n, 3 * D_MODEL), F32)],
        compiler_params=pltpu.CompilerParams(dimension_semantics=("arbitrary",),
                                             vmem_limit_bytes=VMEM_LIMIT),
        name="in_proj",
    )(h, w_rw, w_sb, w_hg, w_gate)


def _merge_ln_kernel(h_ref, yrw_ref, ysb_ref, yhg_ref, gate_ref, wrw_ref, wsb_ref, whg_ref,
                     wout_ref, lng_ref, lnb_ref, o_ref):
    gate = gate_ref[...]
    merged = (gate[:, :D_MODEL] * _dot(yrw_ref[...], wrw_ref[...])
              + gate[:, D_MODEL:2 * D_MODEL] * _dot(ysb_ref[...], wsb_ref[...])
              + gate[:, 2 * D_MODEL:] * _dot(yhg_ref[...], whg_ref[...]))
    mix = _dot(merged.astype(BF16), wout_ref[...])
    o_ref[...] = _layer_norm(ALPHA * h_ref[...] + mix, lng_ref[...], lnb_ref[...])


def _merge_ln(h, y_rw, y_sb, y_hg, gate, w_br_rw, w_br_sb, w_br_hg, w_out, ln_g, ln_b, tm):
    n = h.shape[0]

    def row(c):
        return pl.BlockSpec((tm, c), lambda i: (i, 0))

    return pl.pallas_call(
        _merge_ln_kernel,
        grid=(n // tm,),
        in_specs=[row(D_MODEL), row(RW_WIDTH), row(SB_WIDTH), row(HG_WIDTH), row(3 * D_MODEL),
                  _resident(w_br_rw.shape), _resident(w_br_sb.shape), _resident(w_br_hg.shape),
                  _resident(w_out.shape), _resident((1, D_MODEL)), _resident((1, D_MODEL))],
        out_specs=row(D_MODEL),
        out_shape=jax.ShapeDtypeStruct((n, D_MODEL), F32),
        compiler_params=pltpu.CompilerParams(dimension_semantics=("arbitrary",),
                                             vmem_limit_bytes=VMEM_LIMIT),
        name="merge_ln",
    )(h, y_rw, y_sb, y_hg, gate, w_br_rw, w_br_sb, w_br_hg, w_out, ln_g, ln_b)


def _ple_ln_kernel(h_ref, p_ref, wg_ref, wp_ref, lng_ref, lnb_ref, o_ref):
    h = h_ref[...]
    gate = _sigmoid(_dot(h.astype(BF16), wg_ref[...]))
    ple = gate * _dot(p_ref[...].astype(BF16), wp_ref[...])
    o_ref[...] = _layer_norm(ALPHA * h + ple, lng_ref[...], lnb_ref[...])


def _ple_ln(h, p, w_gate, w_proj, ln_g, ln_b, tm):
    n = h.shape[0]

    def row(c):
        return pl.BlockSpec((tm, c), lambda i: (i, 0))

    return pl.pallas_call(
        _ple_ln_kernel,
        grid=(n // tm,),
        in_specs=[row(D_MODEL), row(PLE_DIM), _resident(w_gate.shape), _resident(w_proj.shape),
                  _resident((1, D_MODEL)), _resident((1, D_MODEL))],
        out_specs=row(D_MODEL),
        out_shape=jax.ShapeDtypeStruct((n, D_MODEL), F32),
        compiler_params=pltpu.CompilerParams(dimension_semantics=("arbitrary",),
                                             vmem_limit_bytes=VMEM_LIMIT),
        name="ple_ln",
    )(h, p, w_gate, w_proj, ln_g, ln_b)


def _sb_kernel(q_ref, k_ref, v_ref, o_ref, acc_ref, carry_ref, *, blk):
    i = pl.program_id(1)
    row = lax.broadcasted_iota(jnp.int32, (blk, blk), 0)
    col = lax.broadcasted_iota(jnp.int32, (blk, blk), 1)
    suffix = jnp.where(row >= col, 1.0, 0.0).astype(BF16)
    strict = col < row

    def head_block(h, start, masked):
        sl = slice(h * SB_HEAD_DIM, (h + 1) * SB_HEAD_DIM)
        kb = k_ref[pl.ds(start, blk), sl]
        vb = v_ref[pl.ds(start, blk), sl]
        z = _dot_nt(q_ref[:, sl], kb)
        m = -_softplus(z)
        if masked:
            m = jnp.where(strict, m, 0.0)
        m_hi, m_lo = _split2(m)
        within = _dot(m_hi, suffix) + _dot(m_lo, suffix)
        att = jnp.exp(z + within + carry_ref[h])
        if masked:
            att = jnp.where(strict, att, 0.0)
        acc_ref[:, sl] += _dot(att.astype(BF16), vb)
        carry_ref[h] += within[:, 0:1]

    acc_ref[...] = jnp.zeros_like(acc_ref)
    carry_ref[...] = jnp.zeros_like(carry_ref)
    for h in range(SB_HEADS):
        head_block(h, pl.multiple_of(i * blk, blk), True)

    def more(t):
        return jnp.logical_and(t < i, jnp.max(carry_ref[...]) > SB_DEAD_LOG)

    def body(t):
        start = pl.multiple_of((i - 1 - t) * blk, blk)
        for h in range(SB_HEADS):
            head_block(h, start, False)
        return t + 1

    lax.while_loop(more, body, 0)
    o_ref[...] = acc_ref[...].astype(o_ref.dtype)


def _sb_attention(q, k, v, bsz, seq, blk):
    n = bsz * seq
    nq = seq // blk
    qspec = pl.BlockSpec((blk, SB_WIDTH), lambda b, i: (b * nq + i, 0))
    kvspec = pl.BlockSpec((seq, SB_WIDTH), lambda b, i: (b, 0))
    return pl.pallas_call(
        functools.partial(_sb_kernel, blk=blk),
        grid=(bsz, nq),
        in_specs=[qspec, kvspec, kvspec],
        out_specs=qspec,
        out_shape=jax.ShapeDtypeStruct((n, SB_WIDTH), BF16),
        scratch_shapes=[pltpu.VMEM((blk, SB_WIDTH), F32), pltpu.VMEM((SB_HEADS, blk, 1), F32)],
        compiler_params=pltpu.CompilerParams(dimension_semantics=("arbitrary", "arbitrary"),
                                             vmem_limit_bytes=VMEM_LIMIT),
        name="stick_breaking",
    )(q, k, v)


def _hgrn2_kernel(u_ref, lbraw_ref, ng_ref, o_ref, st_ref, *, layer, tile):
    @pl.when(pl.program_id(1) == 0)
    def _():
        st_ref[...] = jnp.zeros_like(st_ref)

    raw = lbraw_ref[...]
    e = jnp.exp(raw - jnp.max(raw, axis=0, keepdims=True))
    sm = e / jnp.sum(e, axis=0, keepdims=True)
    lb = jnp.zeros((1, HG_WIDTH), F32)
    for l in range(1, layer + 1):
        lb = lb + sm[l:l + 1]

    u = u_ref[...]
    q = _sigmoid(u[:, :HG_WIDTH])
    f_raw = u[:, HG_WIDTH:2 * HG_WIDTH]
    val = u[:, 2 * HG_WIDTH:3 * HG_WIDTH]
    out_gate = u[:, 3 * HG_WIDTH:]

    log_sig = jnp.minimum(f_raw, 0.0) - jnp.log1p(jnp.exp(-jnp.abs(f_raw)))
    if layer == 0:
        log_f = log_sig
    else:
        a = jnp.log(lb)
        b = jnp.log1p(-lb) + log_sig
        log_f = jnp.maximum(a, b) + jnp.log1p(jnp.exp(-jnp.abs(a - b)))
    kg = (1.0 - lb) * _sigmoid(-f_raw)

    pos = lax.broadcasted_iota(jnp.int32, (tile, HG_WIDTH), 0) % HG_CHUNK
    g = log_f
    d = 1
    while d < HG_CHUNK:
        g = g + jnp.where(pos >= d, pltpu.roll(g, d, 0), 0.0)
        d *= 2

    intra = [jnp.zeros((tile, HG_DIM), F32) for _ in range(HG_HEADS)]
    for d in range(HG_CHUNK):
        if d == 0:
            w = q * kg
            vs = val
        else:
            valid = pos >= d
            dec = jnp.exp(jnp.where(valid, g - pltpu.roll(g, d, 0), 0.0))
            w = jnp.where(valid, q * dec * pltpu.roll(kg, d, 0), 0.0)
            vs = pltpu.roll(val, d, 0)
        for h in range(HG_HEADS):
            sl = slice(h * HG_DIM, (h + 1) * HG_DIM)
            intra[h] = intra[h] + jnp.sum(w[:, sl], axis=-1, keepdims=True) * vs[:, sl]

    qg = (q * jnp.exp(g)).astype(BF16)
    ng = ng_ref[...]
    for c in range(tile // HG_CHUNK):
        rows = slice(c * HG_CHUNK, (c + 1) * HG_CHUNK)
        g_c = g[rows]
        g_last = g_c[HG_CHUNK - 1:HG_CHUNK]
        kd = (kg[rows] * jnp.exp(g_last - g_c)).astype(BF16)
        gam = jnp.exp(g_last)
        v_c = val[rows].astype(BF16)
        for h in range(HG_HEADS):
            sl = slice(h * HG_DIM, (h + 1) * HG_DIM)
            st = st_ref[h]
            inter = _dot_nt(qg[rows, sl], st.astype(BF16))
            o = inter + intra[h][rows]
            o = o * lax.rsqrt(jnp.mean(o * o, axis=-1, keepdims=True) + HG_EPS) * ng[:, sl]
            gate = out_gate[rows, sl]
            o_ref[rows, sl] = (o * (gate * _sigmoid(gate))).astype(o_ref.dtype)
            st_ref[h] = st * gam[:, sl] + _dot_tn(v_c[:, sl], kd[:, sl])


def _hgrn2(u_hg, lb_raw, norm_g, layer, bsz, seq, tile):
    n = bsz * seq
    nt = seq // tile
    return pl.pallas_call(
        functools.partial(_hgrn2_kernel, layer=layer, tile=tile),
        grid=(bsz, nt),
        in_specs=[pl.BlockSpec((tile, 4 * HG_WIDTH), lambda b, t: (b * nt + t, 0)),
                  _resident(lb_raw.shape), _resident((1, HG_WIDTH))],
        out_specs=pl.BlockSpec((tile, HG_WIDTH), lambda b, t: (b * nt + t, 0)),
        out_shape=jax.ShapeDtypeStruct((n, HG_WIDTH), BF16),
        scratch_shapes=[pltpu.VMEM((HG_HEADS, HG_DIM, HG_DIM), F32)],
        compiler_params=pltpu.CompilerParams(dimension_semantics=("arbitrary", "arbitrary"),
                                             vmem_limit_bytes=VMEM_LIMIT),
        name="hgrn2",
    )(u_hg, lb_raw, norm_g)


def _quad_mask(rows):
    r = lax.broadcasted_iota(jnp.int32, (rows, RW_QUAD), 0)
    c = lax.broadcasted_iota(jnp.int32, (rows, RW_QUAD), 1)
    return r, c


def _block_diag(y, same_head):
    y4 = jnp.concatenate([y, y, y, y], axis=0)
    return jnp.where(same_head, y4, jnp.zeros_like(y4))


def _rwkv7_kernel(u_ref, mu_ref, w0_ref, wup_ref, a0_ref, aup_ref, gup_ref, kk_ref, ka_ref,
                  rk_ref, gng_ref, gnb_ref, o_ref, prev_ref, st_ref):
    C = RW_CHUNK
    W = RW_WIDTH
    T = u_ref.shape[0]

    @pl.when(pl.program_id(1) == 0)
    def _():
        prev_ref[...] = jnp.zeros_like(prev_ref)
        st_ref[...] = jnp.zeros_like(st_ref)

    u = u_ref[...]
    trow = lax.broadcasted_iota(jnp.int32, u.shape, 0)
    shifted = jnp.where(trow == 0, prev_ref[...], pltpu.roll(u, 1, 0))
    prev_ref[...] = u[T - 1:T]
    u = u + (shifted - u) * mu_ref[...]

    r = u[:, :W]
    k = u[:, W:2 * W]
    v = u[:, 2 * W:3 * W]
    o0 = 3 * W
    xw = u[:, o0:o0 + RW_LORA_PAD[0]]
    xa = u[:, o0 + RW_LORA_PAD[0]:o0 + RW_LORA_PAD[0] + RW_LORA_PAD[1]]
    xg = u[:, o0 + RW_LORA_PAD[0] + RW_LORA_PAD[1]:]

    w_log = -_softplus(-(w0_ref[...] + _dot(jnp.tanh(xw).astype(BF16), wup_ref[...]))) - 0.5
    log_w = -jnp.exp(w_log)
    a = _sigmoid(a0_ref[...] + _dot(xa.astype(BF16), aup_ref[...]))
    gate = _dot(_sigmoid(xg).astype(BF16), gup_ref[...])

    br, bc = _quad_mask(RW_QUAD)
    same_head = (br // RW_HEAD_DIM) == (bc // RW_HEAD_DIM)
    head_ones = jnp.where(same_head, 1.0, 0.0).astype(BF16)

    def head_sum(x):
        pieces = [p[:, sl] for p in _split3(x) for sl in (slice(0, RW_QUAD), slice(RW_QUAD, W))]
        s = _dot(jnp.concatenate(pieces, axis=0), head_ones)
        halves = [s[j * T:(j + 1) * T] + s[(j + 2) * T:(j + 3) * T] + s[(j + 4) * T:(j + 5) * T]
                  for j in range(2)]
        return jnp.concatenate(halves, axis=1)

    kk = k * kk_ref[...]
    kk = kk / jnp.maximum(jnp.sqrt(head_sum(kk * kk)), 1e-12)
    k = k * (1.0 + (a - 1.0) * ka_ref[...])
    bonus = head_sum(r * k * rk_ref[...]) * v
    a_vec = -kk
    b_vec = kk * a

    g = log_w
    wrow = lax.broadcasted_iota(jnp.int32, (T, W), 0) % C
    d = 1
    while d < C:
        g = g + jnp.where(wrow >= d, pltpu.roll(g, d, 0), 0.0)
        d *= 2
    g_last = jnp.concatenate(
        [jnp.broadcast_to(g[c * C + C - 1:c * C + C], (C, W)) for c in range(T // C)], axis=0)
    inv = jnp.exp(-g)
    a_dec = a_vec * jnp.exp(g - log_w)
    r_dec = r * jnp.exp(g)
    b_inv = b_vec * inv
    k_inv = k * inv
    dec = jnp.exp(g_last - g)
    b_dec = b_vec * dec
    k_dec = k * dec
    e_last = jnp.exp(g_last)

    eye_bd = br == bc
    tr, tc = _quad_mask(C)
    s_idx = tc % RW_HEAD_DIM
    strict = s_idx < tr
    incl = s_idx <= tr
    eye_q = jnp.where(s_idx == tr, 1.0, 0.0)

    def bd(y):
        return _block_diag(y.astype(BF16), same_head)

    def hmm(x, y):
        return _dot(x.astype(BF16), bd(y))

    items = [(slice(c * C, (c + 1) * C), slice(qd * RW_QUAD, (qd + 1) * RW_QUAD))
             for c in range(T // C) for qd in range(W // RW_QUAD)]
    cut = [dict(a_dec=a_dec[rs, ls], r_dec=r_dec[rs, ls], v=v[rs, ls]) for rs, ls in items]

    for it, (rs, ls) in zip(cut, items):
        ar = jnp.concatenate([it["a_dec"], it["r_dec"]], axis=0).astype(BF16)
        lm_b = _dot_nt(ar, bd(b_inv[rs, ls]))
        lm_k = _dot_nt(ar, bd(k_inv[rs, ls]))
        it["l_ab"] = jnp.where(strict, lm_b[:C], 0.0)
        it["m_rb"] = jnp.where(incl, lm_b[C:], 0.0).astype(BF16)
        it["m_rk"] = jnp.where(incl, lm_k[C:], 0.0).astype(BF16)
        it["v_bd"] = bd(it["v"])
        it["lv"] = _dot(jnp.where(strict, lm_k[:C], 0.0).astype(BF16), it["v_bd"])
        it["inv_l"] = eye_q + it["l_ab"]
        it["lp"] = it["l_ab"]

    for _ in range(5):
        for it in cut:
            it["lp"] = hmm(it["lp"], it["lp"])
        for it in cut:
            it["inv_l"] = it["inv_l"] + hmm(it["inv_l"], it["lp"])

    for it, (rs, ls) in zip(cut, items):
        inv_b = it["inv_l"].astype(BF16)
        w1 = _dot(inv_b, bd(it["a_dec"]))
        u0 = _dot(inv_b, bd(it["lv"]))
        it["q_eff"] = (it["r_dec"] + _dot(it["m_rb"], bd(w1))).astype(BF16)
        it["y_intra"] = _dot(it["m_rb"], bd(u0)) + _dot(it["m_rk"], it["v_bd"])
        b_dec_b = b_dec[rs, ls].astype(BF16)
        it["p_bd"] = (jnp.where(same_head, _dot_tn(w1.astype(BF16), b_dec_b), 0.0)
                      + jnp.where(eye_bd, e_last[rs, ls][:1], 0.0)).astype(BF16)
        it["d_bd"] = jnp.where(
            same_head, _dot_tn(u0.astype(BF16), b_dec_b)
            + _dot_tn(it["v"].astype(BF16), k_dec[rs, ls].astype(BF16)), 0.0)

    n_quads = W // RW_QUAD
    rows_out = []
    for c in range(T // C):
        ys = []
        for qd in range(n_quads):
            it = cut[c * n_quads + qd]
            st = st_ref[qd]
            ys.append(it["y_intra"] + _dot_nt(it["q_eff"], st.astype(BF16)))
            st_ref[qd] = _dot(st.astype(BF16), it["p_bd"]) + it["d_bd"]
        rows_out.append(jnp.concatenate(ys, axis=1))
    y = jnp.concatenate(rows_out, axis=0)
    mean = head_sum(y) * (1.0 / RW_HEAD_DIM)
    yc = y - mean
    var = head_sum(yc * yc) * (1.0 / RW_HEAD_DIM)
    y = yc * lax.rsqrt(var + RW_GN_EPS) * gng_ref[...] + gnb_ref[...]
    o_ref[...] = ((y + bonus) * gate).astype(o_ref.dtype)


def _rwkv7(u_rw, mu, w0, w_up, a0, a_up, g_up, k_k, k_a, r_k, gn_g, gn_b, bsz, seq, tile):
    n = bsz * seq
    nc = seq // tile
    vec = _resident((1, RW_WIDTH))
    return pl.pallas_call(
        _rwkv7_kernel,
        grid=(bsz, nc),
        in_specs=[pl.BlockSpec((tile, RW_U_COLS), lambda b, c: (b * nc + c, 0)),
                  _resident((1, RW_U_COLS)), vec, _resident(w_up.shape), vec,
                  _resident(a_up.shape), _resident(g_up.shape), vec, vec, vec, vec, vec],
        out_specs=pl.BlockSpec((tile, RW_WIDTH), lambda b, c: (b * nc + c, 0)),
        out_shape=jax.ShapeDtypeStruct((n, RW_WIDTH), BF16),
        scratch_shapes=[pltpu.VMEM((1, RW_U_COLS), F32),
                        pltpu.VMEM((RW_WIDTH // RW_QUAD, RW_QUAD, RW_QUAD), F32)],
        compiler_params=pltpu.CompilerParams(dimension_semantics=("arbitrary", "arbitrary"),
                                             vmem_limit_bytes=VMEM_LIMIT),
        name="rwkv7",
    )(u_rw, mu, w0, w_up, a0, a_up, g_up, k_k, k_a, r_k, gn_g, gn_b)


def _pad_cols(w, width):
    return jnp.pad(w, ((0, 0), (0, width - w.shape[1])))


def _pad_rows(w, height):
    return jnp.pad(w, ((0, height - w.shape[0]), (0, 0)))


def _split_rw_cols(w):
    o = 3 * RW_WIDTH
    parts = [w[:, :o]]
    for size, pad in zip((RW_DECAY_LORA, RW_AAA_LORA, RW_GATE_LORA), RW_LORA_PAD):
        parts.append(_pad_cols(w[:, o:o + size], pad))
        o += size
    return jnp.concatenate(parts, axis=1)


def kernel(x, p, ln_g, ln_b, ffn1_wg, ffn1_wu, ffn1_wd, w_in, rw_mu, rw_w0, rw_w_up, rw_a0, rw_a_up, rw_g_up, rw_k_k, rw_k_a, rw_r_k, rw_gn_g, rw_gn_b, hg_lb_raw, hg_norm_g, w_br_rw, w_br_sb, w_br_hg, w_out, ffn2_wg, ffn2_wu, ffn2_wd, ple_gate, ple_proj):
    bsz, seq, _ = x.shape
    n = bsz * seq
    tm = min(256, n)
    sb_blk = min(256, seq)
    hg_tile = min(128, seq)
    rw_tile = min(2 * RW_CHUNK, seq)
    rw_cols = 3 * RW_WIDTH + RW_DECAY_LORA + RW_AAA_LORA + RW_GATE_LORA
    sb_end = rw_cols + 3 * SB_WIDTH
    hg_end = sb_end + 4 * HG_WIDTH

    h = x.reshape(n, D_MODEL)
    for i in range(DEPTH):
        def vec(a):
            return a[i].reshape(1, -1)

        h = _ffn_ln(h, ffn1_wg[i].astype(BF16), ffn1_wu[i].astype(BF16), ffn1_wd[i].astype(BF16),
                    vec(ln_g[:, 0]), vec(ln_b[:, 0]), tm)

        wi = w_in[i]
        u_rw, q, k, v, u_hg, gate = _in_proj(
            h, _split_rw_cols(wi[:, :rw_cols]).astype(BF16), wi[:, rw_cols:sb_end].astype(BF16),
            wi[:, sb_end:hg_end].astype(BF16), wi[:, hg_end:].astype(BF16), tm)

        y_rw = _rwkv7(u_rw, _split_rw_cols(rw_mu[i].reshape(1, -1)), vec(rw_w0),
                      _pad_rows(rw_w_up[i], RW_LORA_PAD[0]).astype(BF16), vec(rw_a0),
                      _pad_rows(rw_a_up[i], RW_LORA_PAD[1]).astype(BF16),
                      _pad_rows(rw_g_up[i], RW_LORA_PAD[2]).astype(BF16),
                      vec(rw_k_k), vec(rw_k_a), vec(rw_r_k), vec(rw_gn_g), vec(rw_gn_b), bsz, seq, rw_tile)
        y_sb = _sb_attention(q, k, v, bsz, seq, sb_blk)
        y_hg = _hgrn2(u_hg, hg_lb_raw, vec(hg_norm_g), i, bsz, seq, hg_tile)

        h = _merge_ln(h, y_rw, y_sb, y_hg, gate, w_br_rw[i].astype(BF16), w_br_sb[i].astype(BF16),
                      w_br_hg[i].astype(BF16), w_out[i].astype(BF16),
                      vec(ln_g[:, 1]), vec(ln_b[:, 1]), tm)
        h = _ffn_ln(h, ffn2_wg[i].astype(BF16), ffn2_wu[i].astype(BF16), ffn2_wd[i].astype(BF16),
                    vec(ln_g[:, 2]), vec(ln_b[:, 2]), tm)
        h = _ple_ln(h, p[i].reshape(n, PLE_DIM), ple_gate[i].astype(BF16), ple_proj[i].astype(BF16),
                    vec(ln_g[:, 3]), vec(ln_b[:, 3]), tm)
    return h.reshape(bsz, seq, D_MODEL)
```

```python
import functools

import jax
import jax.numpy as jnp
from jax import lax
from jax.experimental import pallas as pl
from jax.experimental.pallas import tpu as pltpu

F32 = jnp.float32
BF16 = jnp.bfloat16

D_MODEL = 1024
DEPTH = 4
PLE_DIM = 256
D_FF = 2816
RW_HEADS = 8
RW_HEAD_DIM = 64
RW_WIDTH = RW_HEADS * RW_HEAD_DIM
RW_CHUNK = 64
RW_DECAY_LORA = 64
RW_AAA_LORA = 64
RW_GATE_LORA = 160
RW_GN_EPS = 64e-5
RW_QUAD = 256
SB_HEADS = 4
SB_HEAD_DIM = 128
SB_WIDTH = SB_HEADS * SB_HEAD_DIM
SB_DEAD_LOG = -105.0
HG_HEADS = 4
HG_DIM = 128
HG_WIDTH = HG_HEADS * HG_DIM
HG_CHUNK = 16
HG_EPS = 1e-5
LN_EPS = 1e-5
ALPHA = (2 * DEPTH) ** 0.25

LANES = 128
RW_LORA_PAD = (128, 128, 256)
RW_U_COLS = 3 * RW_WIDTH + sum(RW_LORA_PAD)
VMEM_LIMIT = 56 * 1024 * 1024


def _resident(shape):
    return pl.BlockSpec(shape, lambda *_: (0,) * len(shape), pipeline_mode=pl.Buffered(1))


def _dot(a, b):
    return jnp.dot(a, b, preferred_element_type=F32)


def _dot_nt(a, b):
    return lax.dot_general(a, b, (((1,), (1,)), ((), ())), preferred_element_type=F32)


def _dot_tn(a, b):
    return lax.dot_general(a, b, (((0,), (0,)), ((), ())), preferred_element_type=F32)


def _split2(x):
    hi = x.astype(BF16)
    lo = (x - hi.astype(F32)).astype(BF16)
    return hi, lo


def _split3(x):
    hi = x.astype(BF16)
    r = x - hi.astype(F32)
    mid = r.astype(BF16)
    lo = (r - mid.astype(F32)).astype(BF16)
    return hi, mid, lo


def _sigmoid(x):
    return 1.0 / (1.0 + jnp.exp(-x))


def _softplus(x):
    return jnp.maximum(x, 0.0) + jnp.log1p(jnp.exp(-jnp.abs(x)))


def _layer_norm(x, g, b):
    mu = jnp.mean(x, axis=-1, keepdims=True)
    xc = x - mu
    var = jnp.mean(xc * xc, axis=-1, keepdims=True)
    return xc * lax.rsqrt(var + LN_EPS) * g + b


def _ffn_ln_kernel(h_ref, wg_ref, wu_ref, wd_ref, lng_ref, lnb_ref, o_ref):
    h = h_ref[...]
    hb = h.astype(BF16)
    g = _dot(hb, wg_ref[...])
    u = _dot(hb, wu_ref[...])
    a = (g * _sigmoid(g) * u).astype(BF16)
    y = _dot(a, wd_ref[...])
    o_ref[...] = _layer_norm(ALPHA * h + 0.5 * y, lng_ref[...], lnb_ref[...])


def _ffn_ln(h, wg, wu, wd, ln_g, ln_b, tm):
    n = h.shape[0]
    row = pl.BlockSpec((tm, D_MODEL), lambda i: (i, 0))
    return pl.pallas_call(
        _ffn_ln_kernel,
        grid=(n // tm,),
        in_specs=[row, _resident((D_MODEL, D_FF)), _resident((D_MODEL, D_FF)),
                  _resident((D_FF, D_MODEL)), _resident((1, D_MODEL)), _resident((1, D_MODEL))],
        out_specs=row,
        out_shape=jax.ShapeDtypeStruct((n, D_MODEL), F32),
        compiler_params=pltpu.CompilerParams(dimension_semantics=("arbitrary",),
                                             vmem_limit_bytes=VMEM_LIMIT),
        name="ffn_ln",
    )(h, wg, wu, wd, ln_g, ln_b)


def _in_proj_kernel(h_ref, wrw_ref, wsb_ref, whg_ref, wgt_ref,
                    urw_ref, q_ref, k_ref, v_ref, uhg_ref, gate_ref):
    hb = h_ref[...].astype(BF16)
    urw_ref[...] = _dot(hb, wrw_ref[...])
    sb = _dot(hb, wsb_ref[...])
    q_ref[...] = (sb[:, :SB_WIDTH] * (SB_HEAD_DIM ** -0.5)).astype(BF16)
    k_ref[...] = sb[:, SB_WIDTH:2 * SB_WIDTH].astype(BF16)
    v_ref[...] = sb[:, 2 * SB_WIDTH:].astype(BF16)
    uhg_ref[...] = _dot(hb, whg_ref[...])
    gate_ref[...] = _sigmoid(_dot(hb, wgt_ref[...]))


def _in_proj(h, w_rw, w_sb, w_hg, w_gate, tm):
    n = h.shape[0]

    def row(c):
        return pl.BlockSpec((tm, c), lambda i: (i, 0))

    return pl.pallas_call(
        _in_proj_kernel,
        grid=(n // tm,),
        in_specs=[row(D_MODEL), _resident(w_rw.shape), _resident(w_sb.shape),
                  _resident(w_hg.shape), _resident(w_gate.shape)],
        out_specs=[row(RW_U_COLS), row(SB_WIDTH), row(SB_WIDTH), row(SB_WIDTH),
                   row(4 * HG_WIDTH), row(3 * D_MODEL)],
        out_shape=[jax.ShapeDtypeStruct((n, RW_U_COLS), F32),
                   jax.ShapeDtypeStruct((n, SB_WIDTH), BF16),
                   jax.ShapeDtypeStruct((n, SB_WIDTH), BF16),
                   jax.ShapeDtypeStruct((n, SB_WIDTH), BF16),
                   jax.ShapeDtypeStruct((n, 4 * HG_WIDTH), F32),
                   jax.ShapeDtypeStruct((n, 3 * D_MODEL), F32)],
        compiler_params=pltpu.CompilerParams(dimension_semantics=("arbitrary",),
                                             vmem_limit_bytes=VMEM_LIMIT),
        name="in_proj",
    )(h, w_rw, w_sb, w_hg, w_gate)


def _merge_ln_kernel(h_ref, yrw_ref, ysb_ref, yhg_ref, gate_ref, wrw_ref, wsb_ref, whg_ref,
                     wout_ref, lng_ref, lnb_ref, o_ref):
    gate = gate_ref[...]
    merged = (gate[:, :D_MODEL] * _dot(yrw_ref[...], wrw_ref[...])
              + gate[:, D_MODEL:2 * D_MODEL] * _dot(ysb_ref[...], wsb_ref[...])
              + gate[:, 2 * D_MODEL:] * _dot(yhg_ref[...], whg_ref[...]))
    mix = _dot(merged.astype(BF16), wout_ref[...])
    o_ref[...] = _layer_norm(ALPHA * h_ref[...] + mix, lng_ref[...], lnb_ref[...])


def _merge_ln(h, y_rw, y_sb, y_hg, gate, w_br_rw, w_br_sb, w_br_hg, w_out, ln_g, ln_b, tm):
    n = h.shape[0]

    def row(c):
        return pl.BlockSpec((tm, c), lambda i: (i, 0))

    return pl.pallas_call(
        _merge_ln_kernel,
        grid=(n // tm,),
        in_specs=[row(D_MODEL), row(RW_WIDTH), row(SB_WIDTH), row(HG_WIDTH), row(3 * D_MODEL),
                  _resident(w_br_rw.shape), _resident(w_br_sb.shape), _resident(w_br_hg.shape),
                  _resident(w_out.shape), _resident((1, D_MODEL)), _resident((1, D_MODEL))],
        out_specs=row(D_MODEL),
        out_shape=jax.ShapeDtypeStruct((n, D_MODEL), F32),
        compiler_params=pltpu.CompilerParams(dimension_semantics=("arbitrary",),
                                             vmem_limit_bytes=VMEM_LIMIT),
        name="merge_ln",
    )(h, y_rw, y_sb, y_hg, gate, w_br_rw, w_br_sb, w_br_hg, w_out, ln_g, ln_b)


def _ple_ln_kernel(h_ref, p_ref, wg_ref, wp_ref, lng_ref, lnb_ref, o_ref):
    h = h_ref[...]
    gate = _sigmoid(_dot(h.astype(BF16), wg_ref[...]))
    ple = gate * _dot(p_ref[...].astype(BF16), wp_ref[...])
    o_ref[...] = _layer_norm(ALPHA * h + ple, lng_ref[...], lnb_ref[...])


def _ple_ln(h, p, w_gate, w_proj, ln_g, ln_b, tm):
    n = h.shape[0]

    def row(c):
        return pl.BlockSpec((tm, c), lambda i: (i, 0))

    return pl.pallas_call(
        _ple_ln_kernel,
        grid=(n // tm,),
        in_specs=[row(D_MODEL), row(PLE_DIM), _resident(w_gate.shape), _resident(w_proj.shape),
                  _resident((1, D_MODEL)), _resident((1, D_MODEL))],
        out_specs=row(D_MODEL),
        out_shape=jax.ShapeDtypeStruct((n, D_MODEL), F32),
        compiler_params=pltpu.CompilerParams(dimension_semantics=("arbitrary",),
                                             vmem_limit_bytes=VMEM_LIMIT),
        name="ple_ln",
    )(h, p, w_gate, w_proj, ln_g, ln_b)


def _sb_kernel(q_ref, k_ref, v_ref, o_ref, acc_ref, carry_ref, *, blk):
    i = pl.program_id(1)
    row = lax.broadcasted_iota(jnp.int32, (blk, blk), 0)
    col = lax.broadcasted_iota(jnp.int32, (blk, blk), 1)
    suffix = jnp.where(row >= col, 1.0, 0.0).astype(BF16)
    strict = col < row

    lanes = [slice(h * SB_HEAD_DIM, (h + 1) * SB_HEAD_DIM) for h in range(SB_HEADS)]

    def key_block(start, masked):
        zs = [_dot_nt(q_ref[:, sl], k_ref[pl.ds(start, blk), sl]) for sl in lanes]
        ms = [-(jnp.maximum(z, 0.0) + jnp.log(1.0 + jnp.exp(-jnp.abs(z)))) for z in zs]
        if masked:
            ms = [jnp.where(strict, m, 0.0) for m in ms]
        withins = []
        for m in ms:
            m_hi, m_lo = _split2(m)
            withins.append(_dot(m_hi, suffix) + _dot(m_lo, suffix))
        atts = [jnp.exp(z + w + carry_ref[h]) for h, (z, w) in enumerate(zip(zs, withins))]
        if masked:
            atts = [jnp.where(strict, a, 0.0) for a in atts]
        for h, sl in enumerate(lanes):
            acc_ref[:, sl] += _dot(atts[h].astype(BF16), v_ref[pl.ds(start, blk), sl])
            carry_ref[h] += withins[h][:, 0:1]

    acc_ref[...] = jnp.zeros_like(acc_ref)
    carry_ref[...] = jnp.zeros_like(carry_ref)
    key_block(pl.multiple_of(i * blk, blk), True)

    def more(t):
        return jnp.logical_and(t < i, jnp.max(carry_ref[...]) > SB_DEAD_LOG)

    def body(t):
        key_block(pl.multiple_of((i - 1 - t) * blk, blk), False)
        return t + 1

    lax.while_loop(more, body, 0)
    o_ref[...] = acc_ref[...].astype(o_ref.dtype)


def _sb_attention(q, k, v, bsz, seq, blk):
    n = bsz * seq
    nq = seq // blk
    qspec = pl.BlockSpec((blk, SB_WIDTH), lambda b, i: (b * nq + i, 0))
    kvspec = pl.BlockSpec((seq, SB_WIDTH), lambda b, i: (b, 0))
    return pl.pallas_call(
        functools.partial(_sb_kernel, blk=blk),
        grid=(bsz, nq),
        in_specs=[qspec, kvspec, kvspec],
        out_specs=qspec,
        out_shape=jax.ShapeDtypeStruct((n, SB_WIDTH), BF16),
        scratch_shapes=[pltpu.VMEM((blk, SB_WIDTH), F32), pltpu.VMEM((SB_HEADS, blk, 1), F32)],
        compiler_params=pltpu.CompilerParams(dimension_semantics=("arbitrary", "arbitrary"),
                                             vmem_limit_bytes=VMEM_LIMIT),
        name="stick_breaking",
    )(q, k, v)


def _hgrn2_kernel(u_ref, lbraw_ref, ng_ref, o_ref, st_ref, *, layer, tile):
    @pl.when(pl.program_id(1) == 0)
    def _():
        st_ref[...] = jnp.zeros_like(st_ref)

    raw = lbraw_ref[...]
    e = jnp.exp(raw - jnp.max(raw, axis=0, keepdims=True))
    sm = e / jnp.sum(e, axis=0, keepdims=True)
    lb = jnp.zeros((1, HG_WIDTH), F32)
    for l in range(1, layer + 1):
        lb = lb + sm[l:l + 1]

    u = u_ref[...]
    q = _sigmoid(u[:, :HG_WIDTH])
    f_raw = u[:, HG_WIDTH:2 * HG_WIDTH]
    val = u[:, 2 * HG_WIDTH:3 * HG_WIDTH]
    out_gate = u[:, 3 * HG_WIDTH:]

    log_sig = jnp.minimum(f_raw, 0.0) - jnp.log1p(jnp.exp(-jnp.abs(f_raw)))
    if layer == 0:
        log_f = log_sig
    else:
        a = jnp.log(lb)
        b = jnp.log1p(-lb) + log_sig
        log_f = jnp.maximum(a, b) + jnp.log1p(jnp.exp(-jnp.abs(a - b)))
    kg = (1.0 - lb) * _sigmoid(-f_raw)

    pos = lax.broadcasted_iota(jnp.int32, (tile, HG_WIDTH), 0) % HG_CHUNK
    g = log_f
    d = 1
    while d < HG_CHUNK:
        g = g + jnp.where(pos >= d, pltpu.roll(g, d, 0), 0.0)
        d *= 2

    intra = [jnp.zeros((tile, HG_DIM), F32) for _ in range(HG_HEADS)]
    for d in range(HG_CHUNK):
        if d == 0:
            w = q * kg
            vs = val
        else:
            valid = pos >= d
            dec = jnp.exp(jnp.where(valid, g - pltpu.roll(g, d, 0), 0.0))
            w = jnp.where(valid, q * dec * pltpu.roll(kg, d, 0), 0.0)
            vs = pltpu.roll(val, d, 0)
        for h in range(HG_HEADS):
            sl = slice(h * HG_DIM, (h + 1) * HG_DIM)
            intra[h] = intra[h] + jnp.sum(w[:, sl], axis=-1, keepdims=True) * vs[:, sl]

    qg = (q * jnp.exp(g)).astype(BF16)
    ng = ng_ref[...]
    for c in range(tile // HG_CHUNK):
        rows = slice(c * HG_CHUNK, (c + 1) * HG_CHUNK)
        g_c = g[rows]
        g_last = g_c[HG_CHUNK - 1:HG_CHUNK]
        kd = (kg[rows] * jnp.exp(g_last - g_c)).astype(BF16)
        gam = jnp.exp(g_last)
        v_c = val[rows].astype(BF16)
        for h in range(HG_HEADS):
            sl = slice(h * HG_DIM, (h + 1) * HG_DIM)
            st = st_ref[h]
            inter = _dot_nt(qg[rows, sl], st.astype(BF16))
            o = inter + intra[h][rows]
            o = o * lax.rsqrt(jnp.mean(o * o, axis=-1, keepdims=True) + HG_EPS) * ng[:, sl]
            gate = out_gate[rows, sl]
            o_ref[rows, sl] = (o * (gate * _sigmoid(gate))).astype(o_ref.dtype)
            st_ref[h] = st * gam[:, sl] + _dot_tn(v_c[:, sl], kd[:, sl])


def _hgrn2(u_hg, lb_raw, norm_g, layer, bsz, seq, tile):
    n = bsz * seq
    nt = seq // tile
    return pl.pallas_call(
        functools.partial(_hgrn2_kernel, layer=layer, tile=tile),
        grid=(bsz, nt),
        in_specs=[pl.BlockSpec((tile, 4 * HG_WIDTH), lambda b, t: (b * nt + t, 0)),
                  _resident(lb_raw.shape), _resident((1, HG_WIDTH))],
        out_specs=pl.BlockSpec((tile, HG_WIDTH), lambda b, t: (b * nt + t, 0)),
        out_shape=jax.ShapeDtypeStruct((n, HG_WIDTH), BF16),
        scratch_shapes=[pltpu.VMEM((HG_HEADS, HG_DIM, HG_DIM), F32)],
        compiler_params=pltpu.CompilerParams(dimension_semantics=("arbitrary", "arbitrary"),
                                             vmem_limit_bytes=VMEM_LIMIT),
        name="hgrn2",
    )(u_hg, lb_raw, norm_g)


def _quad_mask(rows):
    r = lax.broadcasted_iota(jnp.int32, (rows, RW_QUAD), 0)
    c = lax.broadcasted_iota(jnp.int32, (rows, RW_QUAD), 1)
    return r, c


def _block_diag(y, same_head):
    y4 = jnp.concatenate([y, y, y, y], axis=0)
    return jnp.where(same_head, y4, jnp.zeros_like(y4))


def _rwkv7_kernel(u_ref, mu_ref, w0_ref, wup_ref, a0_ref, aup_ref, gup_ref, kk_ref, ka_ref,
                  rk_ref, gng_ref, gnb_ref, o_ref, prev_ref, st_ref):
    C = RW_CHUNK
    W = RW_WIDTH
    T = u_ref.shape[0]

    @pl.when(pl.program_id(1) == 0)
    def _():
        prev_ref[...] = jnp.zeros_like(prev_ref)
        st_ref[...] = jnp.zeros_like(st_ref)

    u = u_ref[...]
    trow = lax.broadcasted_iota(jnp.int32, u.shape, 0)
    shifted = jnp.where(trow == 0, prev_ref[...], pltpu.roll(u, 1, 0))
    prev_ref[...] = u[T - 1:T]
    u = u + (shifted - u) * mu_ref[...]

    r = u[:, :W]
    k = u[:, W:2 * W]
    v = u[:, 2 * W:3 * W]
    o0 = 3 * W
    xw = u[:, o0:o0 + RW_LORA_PAD[0]]
    xa = u[:, o0 + RW_LORA_PAD[0]:o0 + RW_LORA_PAD[0] + RW_LORA_PAD[1]]
    xg = u[:, o0 + RW_LORA_PAD[0] + RW_LORA_PAD[1]:]

    w_log = -_softplus(-(w0_ref[...] + _dot(jnp.tanh(xw).astype(BF16), wup_ref[...]))) - 0.5
    log_w = -jnp.exp(w_log)
    a = _sigmoid(a0_ref[...] + _dot(xa.astype(BF16), aup_ref[...]))
    gate = _dot(_sigmoid(xg).astype(BF16), gup_ref[...])

    br, bc = _quad_mask(RW_QUAD)
    same_head = (br // RW_HEAD_DIM) == (bc // RW_HEAD_DIM)
    head_ones = jnp.where(same_head, 1.0, 0.0).astype(BF16)

    def head_sum(x):
        pieces = [p[:, sl] for p in _split2(x) for sl in (slice(0, RW_QUAD), slice(RW_QUAD, W))]
        s = _dot(jnp.concatenate(pieces, axis=0), head_ones)
        halves = [s[j * T:(j + 1) * T] + s[(j + 2) * T:(j + 3) * T] for j in range(2)]
        return jnp.concatenate(halves, axis=1)

    kk = k * kk_ref[...]
    kk = kk / jnp.maximum(jnp.sqrt(head_sum(kk * kk)), 1e-12)
    k = k * (1.0 + (a - 1.0) * ka_ref[...])
    bonus = head_sum(r * k * rk_ref[...]) * v
    a_vec = -kk
    b_vec = kk * a

    g = log_w
    wrow = lax.broadcasted_iota(jnp.int32, (T, W), 0) % C
    d = 1
    while d < C:
        g = g + jnp.where(wrow >= d, pltpu.roll(g, d, 0), 0.0)
        d *= 2
    g_last = jnp.concatenate(
        [jnp.broadcast_to(g[c * C + C - 1:c * C + C], (C, W)) for c in range(T // C)], axis=0)
    inv = jnp.exp(-g)
    a_dec = a_vec * jnp.exp(g - log_w)
    r_dec = r * jnp.exp(g)
    b_inv = b_vec * inv
    k_inv = k * inv
    dec = jnp.exp(g_last - g)
    b_dec = b_vec * dec
    k_dec = k * dec
    e_last = jnp.exp(g_last)

    eye_bd = br == bc
    tr, tc = _quad_mask(C)
    s_idx = tc % RW_HEAD_DIM
    strict = s_idx < tr
    incl = s_idx <= tr
    eye_q = jnp.where(s_idx == tr, 1.0, 0.0)

    def bd(y):
        return _block_diag(y.astype(BF16), same_head)


    items = [(slice(c * C, (c + 1) * C), slice(qd * RW_QUAD, (qd + 1) * RW_QUAD))
             for c in range(T // C) for qd in range(W // RW_QUAD)]
    cut = [dict(a_dec=a_dec[rs, ls], r_dec=r_dec[rs, ls], v=v[rs, ls]) for rs, ls in items]

    for it, (rs, ls) in zip(cut, items):
        ar = jnp.concatenate([it["a_dec"], it["r_dec"]], axis=0).astype(BF16)
        lm_b = _dot_nt(ar, bd(b_inv[rs, ls]))
        lm_k = _dot_nt(ar, bd(k_inv[rs, ls]))
        it["l_ab"] = jnp.where(strict, lm_b[:C], 0.0)
        it["m_rb"] = jnp.where(incl, lm_b[C:], 0.0).astype(BF16)
        it["m_rk"] = jnp.where(incl, lm_k[C:], 0.0).astype(BF16)
        it["v_bd"] = bd(it["v"])
        it["lv"] = _dot(jnp.where(strict, lm_k[:C], 0.0).astype(BF16), it["v_bd"])
        it["inv_l"] = eye_q + it["l_ab"]
        it["lp"] = it["l_ab"]

    for it in cut:
        it["lp_bd"] = bd(it["lp"])
    for _ in range(5):
        for it in cut:
            it["lp"] = _dot(it["lp"].astype(BF16), it["lp_bd"])
            it["lp_bd"] = bd(it["lp"])
        for it in cut:
            it["inv_l"] = it["inv_l"] + _dot(it["inv_l"].astype(BF16), it["lp_bd"])

    for it, (rs, ls) in zip(cut, items):
        inv_b = it["inv_l"].astype(BF16)
        w1 = _dot(inv_b, bd(it["a_dec"]))
        u0 = _dot(inv_b, bd(it["lv"]))
        it["q_eff"] = (it["r_dec"] + _dot(it["m_rb"], bd(w1))).astype(BF16)
        it["y_intra"] = _dot(it["m_rb"], bd(u0)) + _dot(it["m_rk"], it["v_bd"])
        b_dec_b = b_dec[rs, ls].astype(BF16)
        it["p_bd"] = (jnp.where(same_head, _dot_tn(w1.astype(BF16), b_dec_b), 0.0)
                      + jnp.where(eye_bd, e_last[rs, ls][:1], 0.0)).astype(BF16)
        it["d_bd"] = jnp.where(
            same_head, _dot_tn(u0.astype(BF16), b_dec_b)
            + _dot_tn(it["v"].astype(BF16), k_dec[rs, ls].astype(BF16)), 0.0)

    n_quads = W // RW_QUAD
    rows_out = []
    for c in range(T // C):
        ys = []
        for qd in range(n_quads):
            it = cut[c * n_quads + qd]
            st = st_ref[qd]
            ys.append(it["y_intra"] + _dot_nt(it["q_eff"], st.astype(BF16)))
            st_ref[qd] = _dot(st.astype(BF16), it["p_bd"]) + it["d_bd"]
        rows_out.append(jnp.concatenate(ys, axis=1))
    y = jnp.concatenate(rows_out, axis=0)
    mean = head_sum(y) * (1.0 / RW_HEAD_DIM)
    yc = y - mean
    var = head_sum(yc * yc) * (1.0 / RW_HEAD_DIM)
    y = yc * lax.rsqrt(var + RW_GN_EPS) * gng_ref[...] + gnb_ref[...]
    o_ref[...] = ((y + bonus) * gate).astype(o_ref.dtype)


def _rwkv7(u_rw, mu, w0, w_up, a0, a_up, g_up, k_k, k_a, r_k, gn_g, gn_b, bsz, seq, tile):
    n = bsz * seq
    nc = seq // tile
    vec = _resident((1, RW_WIDTH))
    return pl.pallas_call(
        _rwkv7_kernel,
        grid=(bsz, nc),
        in_specs=[pl.BlockSpec((tile, RW_U_COLS), lambda b, c: (b * nc + c, 0)),
                  _resident((1, RW_U_COLS)), vec, _resident(w_up.shape), vec,
                  _resident(a_up.shape), _resident(g_up.shape), vec, vec, vec, vec, vec],
        out_specs=pl.BlockSpec((tile, RW_WIDTH), lambda b, c: (b * nc + c, 0)),
        out_shape=jax.ShapeDtypeStruct((n, RW_WIDTH), BF16),
        scratch_shapes=[pltpu.VMEM((1, RW_U_COLS), F32),
                        pltpu.VMEM((RW_WIDTH // RW_QUAD, RW_QUAD, RW_QUAD), F32)],
        compiler_params=pltpu.CompilerParams(dimension_semantics=("arbitrary", "arbitrary"),
                                             vmem_limit_bytes=VMEM_LIMIT),
        name="rwkv7",
    )(u_rw, mu, w0, w_up, a0, a_up, g_up, k_k, k_a, r_k, gn_g, gn_b)


def _pad_cols(w, width):
    return jnp.pad(w, ((0, 0), (0, width - w.shape[1])))


def _pad_rows(w, height):
    return jnp.pad(w, ((0, height - w.shape[0]), (0, 0)))


def _split_rw_cols(w):
    o = 3 * RW_WIDTH
    parts = [w[:, :o]]
    for size, pad in zip((RW_DECAY_LORA, RW_AAA_LORA, RW_GATE_LORA), RW_LORA_PAD):
        parts.append(_pad_cols(w[:, o:o + size], pad))
        o += size
    return jnp.concatenate(parts, axis=1)


def kernel(x, p, ln_g, ln_b, ffn1_wg, ffn1_wu, ffn1_wd, w_in, rw_mu, rw_w0, rw_w_up, rw_a0, rw_a_up, rw_g_up, rw_k_k, rw_k_a, rw_r_k, rw_gn_g, rw_gn_b, hg_lb_raw, hg_norm_g, w_br_rw, w_br_sb, w_br_hg, w_out, ffn2_wg, ffn2_wu, ffn2_wd, ple_gate, ple_proj):
    bsz, seq, _ = x.shape
    n = bsz * seq
    tm = min(256, n)
    sb_blk = min(256, seq)
    hg_tile = min(128, seq)
    rw_tile = min(4 * RW_CHUNK, seq)
    rw_cols = 3 * RW_WIDTH + RW_DECAY_LORA + RW_AAA_LORA + RW_GATE_LORA
    sb_end = rw_cols + 3 * SB_WIDTH
    hg_end = sb_end + 4 * HG_WIDTH

    h = x.reshape(n, D_MODEL)
    for i in range(DEPTH):
        def vec(a):
            return a[i].reshape(1, -1)

        h = _ffn_ln(h, ffn1_wg[i].astype(BF16), ffn1_wu[i].astype(BF16), ffn1_wd[i].astype(BF16),
                    vec(ln_g[:, 0]), vec(ln_b[:, 0]), tm)

        wi = w_in[i]
        u_rw, q, k, v, u_hg, gate = _in_proj(
            h, _split_rw_cols(wi[:, :rw_cols]).astype(BF16), wi[:, rw_cols:sb_end].astype(BF16),
            wi[:, sb_end:hg_end].astype(BF16), wi[:, hg_end:].astype(BF16), tm)

        y_rw = _rwkv7(u_rw, _split_rw_cols(rw_mu[i].reshape(1, -1)), vec(rw_w0),
                      _pad_rows(rw_w_up[i], RW_LORA_PAD[0]).astype(BF16), vec(rw_a0),
                      _pad_rows(rw_a_up[i], RW_LORA_PAD[1]).astype(BF16),
                      _pad_rows(rw_g_up[i], RW_LORA_PAD[2]).astype(BF16),
                      vec(rw_k_k), vec(rw_k_a), vec(rw_r_k), vec(rw_gn_g), vec(rw_gn_b), bsz, seq, rw_tile)
        y_sb = _sb_attention(q, k, v, bsz, seq, sb_blk)
        y_hg = _hgrn2(u_hg, hg_lb_raw, vec(hg_norm_g), i, bsz, seq, hg_tile)

        h = _merge_ln(h, y_rw, y_sb, y_hg, gate, w_br_rw[i].astype(BF16), w_br_sb[i].astype(BF16),
                      w_br_hg[i].astype(BF16), w_out[i].astype(BF16),
                      vec(ln_g[:, 1]), vec(ln_b[:, 1]), tm)
        h = _ffn_ln(h, ffn2_wg[i].astype(BF16), ffn2_wu[i].astype(BF16), ffn2_wd[i].astype(BF16),
                    vec(ln_g[:, 2]), vec(ln_b[:, 2]), tm)
        h = _ple_ln(h, p[i].reshape(n, PLE_DIM), ple_gate[i].astype(BF16), ple_proj[i].astype(BF16),
                    vec(ln_g[:, 3]), vec(ln_b[:, 3]), tm)
    return h.reshape(bsz, seq, D_MODEL)
```

```python
import functools

import jax
import jax.numpy as jnp
from jax import lax
from jax.experimental import pallas as pl
from jax.experimental.pallas import tpu as pltpu

F32 = jnp.float32
BF16 = jnp.bfloat16

D_MODEL = 1024
DEPTH = 4
PLE_DIM = 256
D_FF = 2816
RW_HEADS = 8
RW_HEAD_DIM = 64
RW_WIDTH = RW_HEADS * RW_HEAD_DIM
RW_CHUNK = 64
RW_DECAY_LORA = 64
RW_AAA_LORA = 64
RW_GATE_LORA = 160
RW_GN_EPS = 64e-5
RW_QUAD = 256
SB_HEADS = 4
SB_HEAD_DIM = 128
SB_WIDTH = SB_HEADS * SB_HEAD_DIM
SB_DEAD_LOG = -105.0
HG_HEADS = 4
HG_DIM = 128
HG_WIDTH = HG_HEADS * HG_DIM
HG_EPS = 1e-5
LOG2_E = 1.4426950408889634
LN_EPS = 1e-5
ALPHA = (2 * DEPTH) ** 0.25

LANES = 128
RW_LORA_PAD = (128, 128, 256)
RW_U_COLS = 3 * RW_WIDTH + sum(RW_LORA_PAD)
VMEM_LIMIT = 56 * 1024 * 1024


def _resident(shape):
    return pl.BlockSpec(shape, lambda *_: (0,) * len(shape), pipeline_mode=pl.Buffered(1))


def _dot(a, b):
    return jnp.dot(a, b, preferred_element_type=F32)


def _dot_nt(a, b):
    return lax.dot_general(a, b, (((1,), (1,)), ((), ())), preferred_element_type=F32)


def _dot_tn(a, b):
    return lax.dot_general(a, b, (((0,), (0,)), ((), ())), preferred_element_type=F32)


def _split2(x):
    hi = x.astype(BF16)
    lo = (x - hi.astype(F32)).astype(BF16)
    return hi, lo


def _split3(x):
    hi = x.astype(BF16)
    r = x - hi.astype(F32)
    mid = r.astype(BF16)
    lo = (r - mid.astype(F32)).astype(BF16)
    return hi, mid, lo


def _sigmoid(x):
    return 0.5 * jnp.tanh(0.5 * x) + 0.5


def _softplus(x):
    return jnp.maximum(x, 0.0) + jnp.log(1.0 + jnp.exp(-jnp.abs(x)))


def _layer_norm(x, g, b):
    mu = jnp.mean(x, axis=-1, keepdims=True)
    xc = x - mu
    var = jnp.mean(xc * xc, axis=-1, keepdims=True)
    return xc * lax.rsqrt(var + LN_EPS) * g + b


def _ffn_ln_kernel(h_ref, wg_ref, wu_ref, wd_ref, lng_ref, lnb_ref, o_ref):
    h = h_ref[...]
    hb = h.astype(BF16)
    g = _dot(hb, wg_ref[...])
    u = _dot(hb, wu_ref[...])
    a = (g * _sigmoid(g) * u).astype(BF16)
    y = _dot(a, wd_ref[...])
    o_ref[...] = _layer_norm(ALPHA * h + 0.5 * y, lng_ref[...], lnb_ref[...])


def _ffn_ln(h, wg, wu, wd, ln_g, ln_b, tm):
    n = h.shape[0]
    row = pl.BlockSpec((tm, D_MODEL), lambda i: (i, 0))
    return pl.pallas_call(
        _ffn_ln_kernel,
        grid=(n // tm,),
        in_specs=[row, _resident((D_MODEL, D_FF)), _resident((D_MODEL, D_FF)),
                  _resident((D_FF, D_MODEL)), _resident((1, D_MODEL)), _resident((1, D_MODEL))],
        out_specs=row,
        out_shape=jax.ShapeDtypeStruct((n, D_MODEL), F32),
        compiler_params=pltpu.CompilerParams(dimension_semantics=("arbitrary",),
                                             vmem_limit_bytes=VMEM_LIMIT),
        name="ffn_ln",
    )(h, wg, wu, wd, ln_g, ln_b)


def _in_proj_kernel(h_ref, wrw_ref, wsb_ref, whg_ref, wgt_ref,
                    urw_ref, q_ref, k_ref, v_ref, uhg_ref, gate_ref):
    hb = h_ref[...].astype(BF16)
    urw_ref[...] = _dot(hb, wrw_ref[...])
    sb = _dot(hb, wsb_ref[...])
    q_ref[...] = (sb[:, :SB_WIDTH] * (SB_HEAD_DIM ** -0.5)).astype(BF16)
    k_ref[...] = sb[:, SB_WIDTH:2 * SB_WIDTH].astype(BF16)
    v_ref[...] = sb[:, 2 * SB_WIDTH:].astype(BF16)
    uhg_ref[...] = _dot(hb, whg_ref[...])
    gate_ref[...] = _sigmoid(_dot(hb, wgt_ref[...]))


def _in_proj(h, w_rw, w_sb, w_hg, w_gate, tm):
    n = h.shape[0]

    def row(c):
        return pl.BlockSpec((tm, c), lambda i: (i, 0))

    return pl.pallas_call(
        _in_proj_kernel,
        grid=(n // tm,),
        in_specs=[row(D_MODEL), _resident(w_rw.shape), _resident(w_sb.shape),
                  _resident(w_hg.shape), _resident(w_gate.shape)],
        out_specs=[row(RW_U_COLS), row(SB_WIDTH), row(SB_WIDTH), row(SB_WIDTH),
                   row(4 * HG_WIDTH), row(3 * D_MODEL)],
        out_shape=[jax.ShapeDtypeStruct((n, RW_U_COLS), F32),
                   jax.ShapeDtypeStruct((n, SB_WIDTH), BF16),
                   jax.ShapeDtypeStruct((n, SB_WIDTH), BF16),
                   jax.ShapeDtypeStruct((n, SB_WIDTH), BF16),
                   jax.ShapeDtypeStruct((n, 4 * HG_WIDTH), F32),
                   jax.ShapeDtypeStruct((n, 3 * D_MODEL), F32)],
        compiler_params=pltpu.CompilerParams(dimension_semantics=("arbitrary",),
                                             vmem_limit_bytes=VMEM_LIMIT),
        name="in_proj",
    )(h, w_rw, w_sb, w_hg, w_gate)


def _merge_ln_kernel(h_ref, yrw_ref, ysb_ref, yhg_ref, gate_ref, wrw_ref, wsb_ref, whg_ref,
                     wout_ref, lng_ref, lnb_ref, o_ref):
    gate = gate_ref[...]
    merged = (gate[:, :D_MODEL] * _dot(yrw_ref[...], wrw_ref[...])
              + gate[:, D_MODEL:2 * D_MODEL] * _dot(ysb_ref[...], wsb_ref[...])
              + gate[:, 2 * D_MODEL:] * _dot(yhg_ref[...], whg_ref[...]))
    mix = _dot(merged.astype(BF16), wout_ref[...])
    o_ref[...] = _layer_norm(ALPHA * h_ref[...] + mix, lng_ref[...], lnb_ref[...])


def _merge_ln(h, y_rw, y_sb, y_hg, gate, w_br_rw, w_br_sb, w_br_hg, w_out, ln_g, ln_b, tm):
    n = h.shape[0]

    def row(c):
        return pl.BlockSpec((tm, c), lambda i: (i, 0))

    return pl.pallas_call(
        _merge_ln_kernel,
        grid=(n // tm,),
        in_specs=[row(D_MODEL), row(RW_WIDTH), row(SB_WIDTH), row(HG_WIDTH), row(3 * D_MODEL),
                  _resident(w_br_rw.shape), _resident(w_br_sb.shape), _resident(w_br_hg.shape),
                  _resident(w_out.shape), _resident((1, D_MODEL)), _resident((1, D_MODEL))],
        out_specs=row(D_MODEL),
        out_shape=jax.ShapeDtypeStruct((n, D_MODEL), F32),
        compiler_params=pltpu.CompilerParams(dimension_semantics=("arbitrary",),
                                             vmem_limit_bytes=VMEM_LIMIT),
        name="merge_ln",
    )(h, y_rw, y_sb, y_hg, gate, w_br_rw, w_br_sb, w_br_hg, w_out, ln_g, ln_b)


def _ple_ln_kernel(h_ref, p_ref, wg_ref, wp_ref, lng_ref, lnb_ref, o_ref):
    h = h_ref[...]
    gate = _sigmoid(_dot(h.astype(BF16), wg_ref[...]))
    ple = gate * _dot(p_ref[...].astype(BF16), wp_ref[...])
    o_ref[...] = _layer_norm(ALPHA * h + ple, lng_ref[...], lnb_ref[...])


def _ple_ln(h, p, w_gate, w_proj, ln_g, ln_b, tm):
    n = h.shape[0]

    def row(c):
        return pl.BlockSpec((tm, c), lambda i: (i, 0))

    return pl.pallas_call(
        _ple_ln_kernel,
        grid=(n // tm,),
        in_specs=[row(D_MODEL), row(PLE_DIM), _resident(w_gate.shape), _resident(w_proj.shape),
                  _resident((1, D_MODEL)), _resident((1, D_MODEL))],
        out_specs=row(D_MODEL),
        out_shape=jax.ShapeDtypeStruct((n, D_MODEL), F32),
        compiler_params=pltpu.CompilerParams(dimension_semantics=("arbitrary",),
                                             vmem_limit_bytes=VMEM_LIMIT),
        name="ple_ln",
    )(h, p, w_gate, w_proj, ln_g, ln_b)


def _sb_kernel(q_ref, k_ref, v_ref, o_ref, acc_ref, carry_ref, *, blk):
    i = pl.program_id(1)
    row = lax.broadcasted_iota(jnp.int32, (blk, blk), 0)
    col = lax.broadcasted_iota(jnp.int32, (blk, blk), 1)
    suffix = jnp.where(row >= col, 1.0, 0.0).astype(BF16)
    strict = col < row

    lanes = [slice(h * SB_HEAD_DIM, (h + 1) * SB_HEAD_DIM) for h in range(SB_HEADS)]

    def key_block(start, masked):
        zs = [_dot_nt(q_ref[:, sl], k_ref[pl.ds(start, blk), sl]) for sl in lanes]
        ms = [-(jnp.maximum(z, 0.0) + jnp.log(1.0 + jnp.exp(-jnp.abs(z)))) for z in zs]
        if masked:
            ms = [jnp.where(strict, m, 0.0) for m in ms]
        withins = []
        for m in ms:
            m_hi, m_lo = _split2(m)
            withins.append(_dot(m_hi, suffix) + _dot(m_lo, suffix))
        atts = [jnp.exp(z + w + carry_ref[h]) for h, (z, w) in enumerate(zip(zs, withins))]
        if masked:
            atts = [jnp.where(strict, a, 0.0) for a in atts]
        for h, sl in enumerate(lanes):
            acc_ref[:, sl] += _dot(atts[h].astype(BF16), v_ref[pl.ds(start, blk), sl])
            carry_ref[h] += withins[h][:, 0:1]

    acc_ref[...] = jnp.zeros_like(acc_ref)
    carry_ref[...] = jnp.zeros_like(carry_ref)
    key_block(pl.multiple_of(i * blk, blk), True)

    def more(t):
        return jnp.logical_and(t < i, jnp.max(carry_ref[...]) > SB_DEAD_LOG)

    def body(t):
        key_block(pl.multiple_of((i - 1 - t) * blk, blk), False)
        return t + 1

    lax.while_loop(more, body, 0)
    o_ref[...] = acc_ref[...].astype(o_ref.dtype)


def _sb_attention(q, k, v, bsz, seq, blk):
    n = bsz * seq
    nq = seq // blk
    qspec = pl.BlockSpec((blk, SB_WIDTH), lambda b, i: (b * nq + i, 0))
    kvspec = pl.BlockSpec((seq, SB_WIDTH), lambda b, i: (b, 0))
    return pl.pallas_call(
        functools.partial(_sb_kernel, blk=blk),
        grid=(bsz, nq),
        in_specs=[qspec, kvspec, kvspec],
        out_specs=qspec,
        out_shape=jax.ShapeDtypeStruct((n, SB_WIDTH), BF16),
        scratch_shapes=[pltpu.VMEM((blk, SB_WIDTH), F32), pltpu.VMEM((SB_HEADS, blk, 1), F32)],
        compiler_params=pltpu.CompilerParams(dimension_semantics=("arbitrary", "arbitrary"),
                                             vmem_limit_bytes=VMEM_LIMIT),
        name="stick_breaking",
    )(q, k, v)


def _hgrn2_kernel(u_ref, lbraw_ref, ng_ref, o_ref, st_ref, *, layer, tile):
    @pl.when(pl.program_id(1) == 0)
    def _():
        st_ref[...] = jnp.zeros_like(st_ref)

    raw = lbraw_ref[...]
    e = jnp.exp(raw - jnp.max(raw, axis=0, keepdims=True))
    sm = e / jnp.sum(e, axis=0, keepdims=True)
    lb = jnp.zeros((1, HG_WIDTH), F32)
    for l in range(1, layer + 1):
        lb = lb + sm[l:l + 1]

    u = u_ref[...]
    q = _sigmoid(u[:, :HG_WIDTH])
    f_raw = u[:, HG_WIDTH:2 * HG_WIDTH]
    val = u[:, 2 * HG_WIDTH:3 * HG_WIDTH]
    out_gate = u[:, 3 * HG_WIDTH:]

    half_tanh = 0.5 * jnp.tanh(0.5 * f_raw)
    sig_pos = 0.5 + half_tanh
    sig_neg = 0.5 - half_tanh
    log_sig = jnp.minimum(f_raw, 0.0) - jnp.log(1.0 + jnp.exp(-jnp.abs(f_raw)))
    if layer == 0:
        log_f = log_sig
    else:
        log_f = jnp.where(lb > 0.0, jnp.log(lb + (1.0 - lb) * sig_pos), log_sig)
    kg = (1.0 - lb) * sig_neg

    pos = lax.broadcasted_iota(jnp.int32, (tile, HG_WIDTH), 0)
    ti = lax.broadcasted_iota(jnp.int32, (tile, tile), 0)
    si = lax.broadcasted_iota(jnp.int32, (tile, tile), 1)
    prefix = jnp.where(si <= ti, 1.0, 0.0).astype(BF16)
    g = sum(_dot(prefix, piece) for piece in _split3(log_f))

    heads = [slice(h * HG_DIM, (h + 1) * HG_DIM) for h in range(HG_HEADS)]
    qb = q.astype(BF16)
    kgb = kg.astype(BF16)
    att = [jnp.where(ti == si, _dot_nt(qb[:, sl], kgb[:, sl]), 0.0) for sl in heads]
    last = g
    w = 1
    while w < tile:
        upper = (pos & (2 * w - 1)) >= w
        g_ref = jnp.where(upper, pltpu.roll(last, w, 0), last)
        last = jnp.where(upper, last, pltpu.roll(last, tile - w, 0))
        decay = jnp.exp2(jnp.abs(g - g_ref) * -LOG2_E)
        z = (jnp.where(upper, q, kg) * decay).astype(BF16)
        pair = ((ti & -(2 * w)) == (si & -(2 * w))) & ((ti & w) != 0) & ((si & w) == 0)
        att = [a + jnp.where(pair, _dot_nt(z[:, sl], z[:, sl]), 0.0) for a, sl in zip(att, heads)]
        w *= 2

    valb = val.astype(BF16)
    qg = (q * jnp.exp(g)).astype(BF16)
    g_last = g[tile - 1:tile]
    kd = (kg * jnp.exp(g_last - g)).astype(BF16)
    gam = jnp.exp(g_last)
    ng = ng_ref[...]
    for h, sl in enumerate(heads):
        st = st_ref[h]
        o = _dot(att[h].astype(BF16), valb[:, sl]) + _dot_nt(qg[:, sl], st.astype(BF16))
        o = o * lax.rsqrt(jnp.mean(o * o, axis=-1, keepdims=True) + HG_EPS) * ng[:, sl]
        gate = out_gate[:, sl]
        o_ref[:, sl] = (o * (gate * _sigmoid(gate))).astype(o_ref.dtype)
        st_ref[h] = st * gam[:, sl] + _dot_tn(valb[:, sl], kd[:, sl])


def _hgrn2(u_hg, lb_raw, norm_g, layer, bsz, seq, tile):
    n = bsz * seq
    nt = seq // tile
    return pl.pallas_call(
        functools.partial(_hgrn2_kernel, layer=layer, tile=tile),
        grid=(bsz, nt),
        in_specs=[pl.BlockSpec((tile, 4 * HG_WIDTH), lambda b, t: (b * nt + t, 0)),
                  _resident(lb_raw.shape), _resident((1, HG_WIDTH))],
        out_specs=pl.BlockSpec((tile, HG_WIDTH), lambda b, t: (b * nt + t, 0)),
        out_shape=jax.ShapeDtypeStruct((n, HG_WIDTH), BF16),
        scratch_shapes=[pltpu.VMEM((HG_HEADS, HG_DIM, HG_DIM), F32)],
        compiler_params=pltpu.CompilerParams(dimension_semantics=("arbitrary", "arbitrary"),
                                             vmem_limit_bytes=VMEM_LIMIT),
        name="hgrn2",
    )(u_hg, lb_raw, norm_g)


def _quad_mask(rows):
    r = lax.broadcasted_iota(jnp.int32, (rows, RW_QUAD), 0)
    c = lax.broadcasted_iota(jnp.int32, (rows, RW_QUAD), 1)
    return r, c


def _block_diag(y, head_ones):
    return jnp.concatenate([y, y, y, y], axis=0) * head_ones


def _rwkv7_kernel(u_ref, mu_ref, w0_ref, wup_ref, a0_ref, aup_ref, gup_ref, kk_ref, ka_ref,
                  rk_ref, gng_ref, gnb_ref, o_ref, prev_ref, st_ref):
    C = RW_CHUNK
    W = RW_WIDTH
    T = u_ref.shape[0]

    @pl.when(pl.program_id(1) == 0)
    def _():
        prev_ref[...] = jnp.zeros_like(prev_ref)
        st_ref[...] = jnp.zeros_like(st_ref)

    u = u_ref[...]
    trow = lax.broadcasted_iota(jnp.int32, u.shape, 0)
    shifted = jnp.where(trow == 0, prev_ref[...], pltpu.roll(u, 1, 0))
    prev_ref[...] = u[T - 1:T]
    u = u + (shifted - u) * mu_ref[...]

    r = u[:, :W]
    k = u[:, W:2 * W]
    v = u[:, 2 * W:3 * W]
    o0 = 3 * W
    xw = u[:, o0:o0 + RW_LORA_PAD[0]]
    xa = u[:, o0 + RW_LORA_PAD[0]:o0 + RW_LORA_PAD[0] + RW_LORA_PAD[1]]
    xg = u[:, o0 + RW_LORA_PAD[0] + RW_LORA_PAD[1]:]

    w_log = -_softplus(-(w0_ref[...] + _dot(jnp.tanh(xw).astype(BF16), wup_ref[...]))) - 0.5
    log_w = -jnp.exp(w_log)
    a = _sigmoid(a0_ref[...] + _dot(xa.astype(BF16), aup_ref[...]))
    gate = _dot(_sigmoid(xg).astype(BF16), gup_ref[...])

    br, bc = _quad_mask(RW_QUAD)
    same_head = (br // RW_HEAD_DIM) == (bc // RW_HEAD_DIM)
    head_ones = jnp.where(same_head, 1.0, 0.0).astype(BF16)

    def head_sum(x):
        pieces = [p[:, sl] for p in _split2(x) for sl in (slice(0, RW_QUAD), slice(RW_QUAD, W))]
        s = _dot(jnp.concatenate(pieces, axis=0), head_ones)
        halves = [s[j * T:(j + 1) * T] + s[(j + 2) * T:(j + 3) * T] for j in range(2)]
        return jnp.concatenate(halves, axis=1)

    kk = k * kk_ref[...]
    kk = kk * jnp.minimum(lax.rsqrt(head_sum(kk * kk)), 1e12)
    k = k * (1.0 + (a - 1.0) * ka_ref[...])
    bonus = head_sum(r * k * rk_ref[...]) * v
    a_vec = -kk
    b_vec = kk * a

    g = log_w
    wrow = lax.broadcasted_iota(jnp.int32, (T, W), 0) % C
    d = 1
    while d < C:
        g = g + jnp.where(wrow >= d, pltpu.roll(g, d, 0), 0.0)
        d *= 2
    g_last = jnp.concatenate(
        [jnp.broadcast_to(g[c * C + C - 1:c * C + C], (C, W)) for c in range(T // C)], axis=0)
    inv = jnp.exp(-g)
    a_dec = a_vec * jnp.exp(g - log_w)
    r_dec = r * jnp.exp(g)
    b_inv = b_vec * inv
    k_inv = k * inv
    dec = jnp.exp(g_last - g)
    b_dec = b_vec * dec
    k_dec = k * dec
    e_last = jnp.exp(g_last)

    eye_bd = br == bc
    tr, tc = _quad_mask(C)
    s_idx = tc % RW_HEAD_DIM
    strict = s_idx < tr
    incl = s_idx <= tr
    eye_q = jnp.where(s_idx == tr, 1.0, 0.0)

    def bd(y):
        return _block_diag(y.astype(BF16), head_ones)


    items = [(slice(c * C, (c + 1) * C), slice(qd * RW_QUAD, (qd + 1) * RW_QUAD))
             for c in range(T // C) for qd in range(W // RW_QUAD)]
    cut = [dict(a_dec=a_dec[rs, ls], r_dec=r_dec[rs, ls], v=v[rs, ls]) for rs, ls in items]

    for it, (rs, ls) in zip(cut, items):
        ar = jnp.concatenate([it["a_dec"], it["r_dec"]], axis=0).astype(BF16)
        lm_b = _dot_nt(ar, bd(b_inv[rs, ls]))
        lm_k = _dot_nt(ar, bd(k_inv[rs, ls]))
        it["l_ab"] = jnp.where(strict, lm_b[:C], 0.0)
        it["m_rb"] = jnp.where(incl, lm_b[C:], 0.0).astype(BF16)
        it["m_rk"] = jnp.where(incl, lm_k[C:], 0.0).astype(BF16)
        it["v_bd"] = bd(it["v"])
        it["lv"] = _dot(jnp.where(strict, lm_k[:C], 0.0).astype(BF16), it["v_bd"])
        it["inv_l"] = eye_q + it["l_ab"]
        it["lp"] = it["l_ab"]

    for it in cut:
        it["lp_bd"] = bd(it["lp"])
    for _ in range(5):
        for it in cut:
            it["lp"] = _dot(it["lp"].astype(BF16), it["lp_bd"])
            it["lp_bd"] = bd(it["lp"])
        for it in cut:
            it["inv_l"] = it["inv_l"] + _dot(it["inv_l"].astype(BF16), it["lp_bd"])

    for it, (rs, ls) in zip(cut, items):
        inv_b = it["inv_l"].astype(BF16)
        w1 = _dot(inv_b, bd(it["a_dec"]))
        u0 = _dot(inv_b, bd(it["lv"]))
        it["q_eff"] = (it["r_dec"] + _dot(it["m_rb"], bd(w1))).astype(BF16)
        it["y_intra"] = _dot(it["m_rb"], bd(u0)) + _dot(it["m_rk"], it["v_bd"])
        b_dec_b = b_dec[rs, ls].astype(BF16)
        it["p_bd"] = (jnp.where(same_head, _dot_tn(w1.astype(BF16), b_dec_b), 0.0)
                      + jnp.where(eye_bd, e_last[rs, ls][:1], 0.0)).astype(BF16)
        it["d_bd"] = jnp.where(
            same_head, _dot_tn(u0.astype(BF16), b_dec_b)
            + _dot_tn(it["v"].astype(BF16), k_dec[rs, ls].astype(BF16)), 0.0)

    n_quads = W // RW_QUAD
    rows_out = []
    for c in range(T // C):
        ys = []
        for qd in range(n_quads):
            it = cut[c * n_quads + qd]
            st = st_ref[qd]
            ys.append(it["y_intra"] + _dot_nt(it["q_eff"], st.astype(BF16)))
            st_ref[qd] = _dot(st.astype(BF16), it["p_bd"]) + it["d_bd"]
        rows_out.append(jnp.concatenate(ys, axis=1))
    y = jnp.concatenate(rows_out, axis=0)
    mean = head_sum(y) * (1.0 / RW_HEAD_DIM)
    yc = y - mean
    var = head_sum(yc * yc) * (1.0 / RW_HEAD_DIM)
    y = yc * lax.rsqrt(var + RW_GN_EPS) * gng_ref[...] + gnb_ref[...]
    o_ref[...] = ((y + bonus) * gate).astype(o_ref.dtype)


def _rwkv7(u_rw, mu, w0, w_up, a0, a_up, g_up, k_k, k_a, r_k, gn_g, gn_b, bsz, seq, tile):
    n = bsz * seq
    nc = seq // tile
    vec = _resident((1, RW_WIDTH))
    return pl.pallas_call(
        _rwkv7_kernel,
        grid=(bsz, nc),
        in_specs=[pl.BlockSpec((tile, RW_U_COLS), lambda b, c: (b * nc + c, 0)),
                  _resident((1, RW_U_COLS)), vec, _resident(w_up.shape), vec,
                  _resident(a_up.shape), _resident(g_up.shape), vec, vec, vec, vec, vec],
        out_specs=pl.BlockSpec((tile, RW_WIDTH), lambda b, c: (b * nc + c, 0)),
        out_shape=jax.ShapeDtypeStruct((n, RW_WIDTH), BF16),
        scratch_shapes=[pltpu.VMEM((1, RW_U_COLS), F32),
                        pltpu.VMEM((RW_WIDTH // RW_QUAD, RW_QUAD, RW_QUAD), F32)],
        compiler_params=pltpu.CompilerParams(dimension_semantics=("arbitrary", "arbitrary"),
                                             vmem_limit_bytes=VMEM_LIMIT),
        name="rwkv7",
    )(u_rw, mu, w0, w_up, a0, a_up, g_up, k_k, k_a, r_k, gn_g, gn_b)


def _pad_cols(w, width):
    return jnp.pad(w, ((0, 0), (0, width - w.shape[1])))


def _pad_rows(w, height):
    return jnp.pad(w, ((0, height - w.shape[0]), (0, 0)))


def _split_rw_cols(w):
    o = 3 * RW_WIDTH
    parts = [w[:, :o]]
    for size, pad in zip((RW_DECAY_LORA, RW_AAA_LORA, RW_GATE_LORA), RW_LORA_PAD):
        parts.append(_pad_cols(w[:, o:o + size], pad))
        o += size
    return jnp.concatenate(parts, axis=1)


def kernel(x, p, ln_g, ln_b, ffn1_wg, ffn1_wu, ffn1_wd, w_in, rw_mu, rw_w0, rw_w_up, rw_a0, rw_a_up, rw_g_up, rw_k_k, rw_k_a, rw_r_k, rw_gn_g, rw_gn_b, hg_lb_raw, hg_norm_g, w_br_rw, w_br_sb, w_br_hg, w_out, ffn2_wg, ffn2_wu, ffn2_wd, ple_gate, ple_proj):
    bsz, seq, _ = x.shape
    n = bsz * seq
    tm = min(256, n)
    sb_blk = min(256, seq)
    hg_tile = min(128, seq)
    rw_tile = min(4 * RW_CHUNK, seq)
    rw_cols = 3 * RW_WIDTH + RW_DECAY_LORA + RW_AAA_LORA + RW_GATE_LORA
    sb_end = rw_cols + 3 * SB_WIDTH
    hg_end = sb_end + 4 * HG_WIDTH

    h = x.reshape(n, D_MODEL)
    for i in range(DEPTH):
        def vec(a):
            return a[i].reshape(1, -1)

        h = _ffn_ln(h, ffn1_wg[i].astype(BF16), ffn1_wu[i].astype(BF16), ffn1_wd[i].astype(BF16),
                    vec(ln_g[:, 0]), vec(ln_b[:, 0]), tm)

        wi = w_in[i]
        u_rw, q, k, v, u_hg, gate = _in_proj(
            h, _split_rw_cols(wi[:, :rw_cols]).astype(BF16), wi[:, rw_cols:sb_end].astype(BF16),
            wi[:, sb_end:hg_end].astype(BF16), wi[:, hg_end:].astype(BF16), tm)

        y_rw = _rwkv7(u_rw, _split_rw_cols(rw_mu[i].reshape(1, -1)), vec(rw_w0),
                      _pad_rows(rw_w_up[i], RW_LORA_PAD[0]).astype(BF16), vec(rw_a0),
                      _pad_rows(rw_a_up[i], RW_LORA_PAD[1]).astype(BF16),
                      _pad_rows(rw_g_up[i], RW_LORA_PAD[2]).astype(BF16),
                      vec(rw_k_k), vec(rw_k_a), vec(rw_r_k), vec(rw_gn_g), vec(rw_gn_b), bsz, seq, rw_tile)
        y_sb = _sb_attention(q, k, v, bsz, seq, sb_blk)
        y_hg = _hgrn2(u_hg, hg_lb_raw, vec(hg_norm_g), i, bsz, seq, hg_tile)

        h = _merge_ln(h, y_rw, y_sb, y_hg, gate, w_br_rw[i].astype(BF16), w_br_sb[i].astype(BF16),
                      w_br_hg[i].astype(BF16), w_out[i].astype(BF16),
                      vec(ln_g[:, 1]), vec(ln_b[:, 1]), tm)
        h = _ffn_ln(h, ffn2_wg[i].astype(BF16), ffn2_wu[i].astype(BF16), ffn2_wd[i].astype(BF16),
                    vec(ln_g[:, 2]), vec(ln_b[:, 2]), tm)
        h = _ple_ln(h, p[i].reshape(n, PLE_DIM), ple_gate[i].astype(BF16), ple_proj[i].astype(BF16),
                    vec(ln_g[:, 3]), vec(ln_b[:, 3]), tm)
    return h.reshape(bsz, seq, D_MODEL)
```

```python
import functools

import jax
import jax.numpy as jnp
from jax import lax
from jax.experimental import pallas as pl
from jax.experimental.pallas import tpu as pltpu

F32 = jnp.float32
BF16 = jnp.bfloat16

D_MODEL = 1024
DEPTH = 4
PLE_DIM = 256
D_FF = 2816
RW_HEADS = 8
RW_HEAD_DIM = 64
RW_WIDTH = RW_HEADS * RW_HEAD_DIM
RW_CHUNK = 64
RW_DECAY_LORA = 64
RW_AAA_LORA = 64
RW_GATE_LORA = 160
RW_GN_EPS = 64e-5
RW_QUAD = 256
SB_HEADS = 4
SB_HEAD_DIM = 128
SB_WIDTH = SB_HEADS * SB_HEAD_DIM
SB_DEAD_LOG = -105.0
HG_HEADS = 4
HG_DIM = 128
HG_WIDTH = HG_HEADS * HG_DIM
HG_EPS = 1e-5
LOG2_E = 1.4426950408889634
LN_EPS = 1e-5
ALPHA = (2 * DEPTH) ** 0.25

MXU_COLS = 256
RW_LORA_PAD = (128, 128, 256)
RW_U_COLS = 3 * RW_WIDTH + sum(RW_LORA_PAD)
VMEM_LIMIT = 56 * 1024 * 1024


def _resident(shape):
    return pl.BlockSpec(shape, lambda *_: (0,) * len(shape), pipeline_mode=pl.Buffered(1))


def _dot(a, b):
    return jnp.dot(a, b, preferred_element_type=F32)


def _dot_nt(a, b):
    return lax.dot_general(a, b, (((1,), (1,)), ((), ())), preferred_element_type=F32)


def _dot_tn(a, b):
    return lax.dot_general(a, b, (((0,), (0,)), ((), ())), preferred_element_type=F32)


def _split2(x):
    hi = x.astype(BF16)
    lo = (x - hi.astype(F32)).astype(BF16)
    return hi, lo


def _split3(x):
    hi = x.astype(BF16)
    r = x - hi.astype(F32)
    mid = r.astype(BF16)
    lo = (r - mid.astype(F32)).astype(BF16)
    return hi, mid, lo


def _sigmoid(x):
    return 0.5 * jnp.tanh(0.5 * x) + 0.5


def _softplus(x):
    return jnp.maximum(x, 0.0) + jnp.log(1.0 + jnp.exp(-jnp.abs(x)))


def _layer_norm(x, g, b):
    mu = jnp.mean(x, axis=-1, keepdims=True)
    xc = x - mu
    var = jnp.mean(xc * xc, axis=-1, keepdims=True)
    return xc * lax.rsqrt(var + LN_EPS) * g + b


def _ffn_ln_kernel(h_ref, wg_ref, wu_ref, wd_ref, lng_ref, lnb_ref, o_ref):
    h = h_ref[...]
    hb = h.astype(BF16)
    g = _dot(hb, wg_ref[...])
    u = _dot(hb, wu_ref[...])
    a = (g * _sigmoid(g) * u).astype(BF16)
    y = _dot(a, wd_ref[...])
    o_ref[...] = _layer_norm(ALPHA * h + 0.5 * y, lng_ref[...], lnb_ref[...])


def _ffn_ln(h, wg, wu, wd, ln_g, ln_b, tm):
    n = h.shape[0]
    row = pl.BlockSpec((tm, D_MODEL), lambda i: (i, 0))
    return pl.pallas_call(
        _ffn_ln_kernel,
        grid=(n // tm,),
        in_specs=[row, _resident((D_MODEL, D_FF)), _resident((D_MODEL, D_FF)),
                  _resident((D_FF, D_MODEL)), _resident((1, D_MODEL)), _resident((1, D_MODEL))],
        out_specs=row,
        out_shape=jax.ShapeDtypeStruct((n, D_MODEL), F32),
        compiler_params=pltpu.CompilerParams(dimension_semantics=("arbitrary",),
                                             vmem_limit_bytes=VMEM_LIMIT),
        name="ffn_ln",
    )(h, wg, wu, wd, ln_g, ln_b)


def _merge_ln_kernel(h_ref, yrw_ref, ysb_ref, yhg_ref, gate_ref, wrw_ref, wsb_ref, whg_ref,
                     wout_ref, lng_ref, lnb_ref, o_ref):
    gate = gate_ref[...]
    merged = (gate[:, :D_MODEL] * _dot(yrw_ref[...], wrw_ref[...])
              + gate[:, D_MODEL:2 * D_MODEL] * _dot(ysb_ref[...], wsb_ref[...])
              + gate[:, 2 * D_MODEL:] * _dot(yhg_ref[...], whg_ref[...]))
    mix = _dot(merged.astype(BF16), wout_ref[...])
    o_ref[...] = _layer_norm(ALPHA * h_ref[...] + mix, lng_ref[...], lnb_ref[...])


def _merge_ln(h, y_rw, y_sb, y_hg, gate, w_br_rw, w_br_sb, w_br_hg, w_out, ln_g, ln_b, tm):
    n = h.shape[0]

    def row(c):
        return pl.BlockSpec((tm, c), lambda i: (i, 0))

    return pl.pallas_call(
        _merge_ln_kernel,
        grid=(n // tm,),
        in_specs=[row(D_MODEL), row(RW_WIDTH), row(SB_WIDTH), row(HG_WIDTH), row(3 * D_MODEL),
                  _resident(w_br_rw.shape), _resident(w_br_sb.shape), _resident(w_br_hg.shape),
                  _resident(w_out.shape), _resident((1, D_MODEL)), _resident((1, D_MODEL))],
        out_specs=row(D_MODEL),
        out_shape=jax.ShapeDtypeStruct((n, D_MODEL), F32),
        compiler_params=pltpu.CompilerParams(dimension_semantics=("arbitrary",),
                                             vmem_limit_bytes=VMEM_LIMIT),
        name="merge_ln",
    )(h, y_rw, y_sb, y_hg, gate, w_br_rw, w_br_sb, w_br_hg, w_out, ln_g, ln_b)


def _ple_ln_kernel(h_ref, p_ref, wg_ref, wp_ref, lng_ref, lnb_ref, o_ref):
    h = h_ref[...]
    gate = _sigmoid(_dot(h.astype(BF16), wg_ref[...]))
    ple = gate * _dot(p_ref[...].astype(BF16), wp_ref[...])
    o_ref[...] = _layer_norm(ALPHA * h + ple, lng_ref[...], lnb_ref[...])


def _ple_ln(h, p, w_gate, w_proj, ln_g, ln_b, tm):
    n = h.shape[0]

    def row(c):
        return pl.BlockSpec((tm, c), lambda i: (i, 0))

    return pl.pallas_call(
        _ple_ln_kernel,
        grid=(n // tm,),
        in_specs=[row(D_MODEL), row(PLE_DIM), _resident(w_gate.shape), _resident(w_proj.shape),
                  _resident((1, D_MODEL)), _resident((1, D_MODEL))],
        out_specs=row(D_MODEL),
        out_shape=jax.ShapeDtypeStruct((n, D_MODEL), F32),
        compiler_params=pltpu.CompilerParams(dimension_semantics=("arbitrary",),
                                             vmem_limit_bytes=VMEM_LIMIT),
        name="ple_ln",
    )(h, p, w_gate, w_proj, ln_g, ln_b)


def _sb_kernel(q_ref, k_ref, v_ref, o_ref, acc_ref, carry_ref, *, blk):
    i = pl.program_id(1)
    row = lax.broadcasted_iota(jnp.int32, (blk, blk), 0)
    col = lax.broadcasted_iota(jnp.int32, (blk, blk), 1)
    suffix = jnp.where(row >= col, 1.0, 0.0).astype(BF16)
    strict = col < row

    lanes = [slice(h * SB_HEAD_DIM, (h + 1) * SB_HEAD_DIM) for h in range(SB_HEADS)]

    def key_block(start, masked):
        zs = [_dot_nt(q_ref[:, sl], k_ref[pl.ds(start, blk), sl]) for sl in lanes]
        ms = [-(jnp.maximum(z, 0.0) + jnp.log(1.0 + jnp.exp(-jnp.abs(z)))) for z in zs]
        if masked:
            ms = [jnp.where(strict, m, 0.0) for m in ms]
        withins = []
        for m in ms:
            m_hi, m_lo = _split2(m)
            withins.append(_dot(m_hi, suffix) + _dot(m_lo, suffix))
        atts = [jnp.exp(z + w + carry_ref[h]) for h, (z, w) in enumerate(zip(zs, withins))]
        if masked:
            atts = [jnp.where(strict, a, 0.0) for a in atts]
        for h, sl in enumerate(lanes):
            acc_ref[:, sl] += _dot(atts[h].astype(BF16), v_ref[pl.ds(start, blk), sl])
            carry_ref[h] += withins[h][:, 0:1]

    acc_ref[...] = jnp.zeros_like(acc_ref)
    carry_ref[...] = jnp.zeros_like(carry_ref)
    key_block(pl.multiple_of(i * blk, blk), True)

    def more(t):
        return jnp.logical_and(t < i, jnp.max(carry_ref[...]) > SB_DEAD_LOG)

    def body(t):
        key_block(pl.multiple_of((i - 1 - t) * blk, blk), False)
        return t + 1

    lax.while_loop(more, body, 0)
    o_ref[...] = acc_ref[...].astype(o_ref.dtype)


def _sb_attention(q, k, v, bsz, seq, blk):
    n = bsz * seq
    nq = seq // blk
    qspec = pl.BlockSpec((blk, SB_WIDTH), lambda b, i: (b * nq + i, 0))
    kvspec = pl.BlockSpec((seq, SB_WIDTH), lambda b, i: (b, 0))
    return pl.pallas_call(
        functools.partial(_sb_kernel, blk=blk),
        grid=(bsz, nq),
        in_specs=[qspec, kvspec, kvspec],
        out_specs=qspec,
        out_shape=jax.ShapeDtypeStruct((n, SB_WIDTH), BF16),
        scratch_shapes=[pltpu.VMEM((blk, SB_WIDTH), F32), pltpu.VMEM((SB_HEADS, blk, 1), F32)],
        compiler_params=pltpu.CompilerParams(dimension_semantics=("arbitrary", "arbitrary"),
                                             vmem_limit_bytes=VMEM_LIMIT),
        name="stick_breaking",
    )(q, k, v)


def _hgrn2_tile(u, lbraw_ref, ng_ref, o_ref, rows, st_ref, layer, fill):
    tile = u.shape[0]
    raw = lbraw_ref[...]
    e = jnp.exp(raw - jnp.max(raw, axis=0, keepdims=True))
    sm = e / jnp.sum(e, axis=0, keepdims=True)
    lb = jnp.zeros((1, HG_WIDTH), F32)
    for l in range(1, layer + 1):
        lb = lb + sm[l:l + 1]

    q = _sigmoid(u[:, :HG_WIDTH])
    f_raw = u[:, HG_WIDTH:2 * HG_WIDTH]
    val = u[:, 2 * HG_WIDTH:3 * HG_WIDTH]
    out_gate = u[:, 3 * HG_WIDTH:]

    half_tanh = 0.5 * jnp.tanh(0.5 * f_raw)
    sig_pos = 0.5 + half_tanh
    sig_neg = 0.5 - half_tanh
    log_sig = jnp.minimum(f_raw, 0.0) - jnp.log(1.0 + jnp.exp(-jnp.abs(f_raw)))
    if layer == 0:
        log_f = log_sig
    else:
        log_f = jnp.where(lb > 0.0, jnp.log(lb + (1.0 - lb) * sig_pos), log_sig)
    kg = (1.0 - lb) * sig_neg

    pos = lax.broadcasted_iota(jnp.int32, (tile, HG_WIDTH), 0)
    ti = lax.broadcasted_iota(jnp.int32, (tile, tile), 0)
    si = lax.broadcasted_iota(jnp.int32, (tile, tile), 1)
    prefix = jnp.where(si <= ti, 1.0, 0.0).astype(BF16)
    g = sum(_dot(prefix, piece) for piece in _split3(log_f))

    heads = [slice(h * HG_DIM, (h + 1) * HG_DIM) for h in range(HG_HEADS)]
    qb = q.astype(BF16)
    kgb = kg.astype(BF16)
    att = [jnp.where(ti == si, _dot_nt(qb[:, sl], kgb[:, sl]), 0.0) for sl in heads]
    last = g
    w = 1
    while w < tile:
        upper = (pos & (2 * w - 1)) >= w
        g_ref = jnp.where(upper, pltpu.roll(last, w, 0), last)
        last = jnp.where(upper, last, pltpu.roll(last, tile - w, 0))
        decay = jnp.exp2(jnp.abs(g - g_ref) * -LOG2_E)
        z = (jnp.where(upper, q, kg) * decay).astype(BF16)
        pair = ((ti & -(2 * w)) == (si & -(2 * w))) & ((ti & w) != 0) & ((si & w) == 0)
        att = [a + jnp.where(pair, _dot_nt(z[:, sl], z[:, sl]), 0.0) for a, sl in zip(att, heads)]
        fill()
        w *= 2

    valb = val.astype(BF16)
    qg = (q * jnp.exp(g)).astype(BF16)
    g_last = g[tile - 1:tile]
    kd = (kg * jnp.exp(g_last - g)).astype(BF16)
    gam = jnp.exp(g_last)
    ng = ng_ref[...]
    for h, sl in enumerate(heads):
        st = st_ref[h]
        o = _dot(att[h].astype(BF16), valb[:, sl]) + _dot_nt(qg[:, sl], st.astype(BF16))
        o = o * lax.rsqrt(jnp.mean(o * o, axis=-1, keepdims=True) + HG_EPS) * ng[:, sl]
        gate = out_gate[:, sl]
        o_ref[rows, sl] = (o * (gate * _sigmoid(gate))).astype(o_ref.dtype)
        st_ref[h] = st * gam[:, sl] + _dot_tn(valb[:, sl], kd[:, sl])


def _quad_mask(rows):
    r = lax.broadcasted_iota(jnp.int32, (rows, RW_QUAD), 0)
    c = lax.broadcasted_iota(jnp.int32, (rows, RW_QUAD), 1)
    return r, c


def _block_diag(y, head_ones):
    return jnp.concatenate([y, y, y, y], axis=0) * head_ones


def _rwkv7_tile(u, mu_ref, w0_ref, wup_ref, a0_ref, aup_ref, gup_ref, kk_ref, ka_ref,
                rk_ref, gng_ref, gnb_ref, o_ref, prev_ref, st_ref, fill):
    C = RW_CHUNK
    W = RW_WIDTH
    T = u.shape[0]
    trow = lax.broadcasted_iota(jnp.int32, u.shape, 0)
    shifted = jnp.where(trow == 0, prev_ref[...], pltpu.roll(u, 1, 0))
    prev_ref[...] = u[T - 1:T]
    u = u + (shifted - u) * mu_ref[...]
    fill()

    r = u[:, :W]
    k = u[:, W:2 * W]
    v = u[:, 2 * W:3 * W]
    o0 = 3 * W
    xw = u[:, o0:o0 + RW_LORA_PAD[0]]
    xa = u[:, o0 + RW_LORA_PAD[0]:o0 + RW_LORA_PAD[0] + RW_LORA_PAD[1]]
    xg = u[:, o0 + RW_LORA_PAD[0] + RW_LORA_PAD[1]:]

    w_log = -_softplus(-(w0_ref[...] + _dot(jnp.tanh(xw).astype(BF16), wup_ref[...]))) - 0.5
    log_w = -jnp.exp(w_log)
    a = _sigmoid(a0_ref[...] + _dot(xa.astype(BF16), aup_ref[...]))
    gate = _dot(_sigmoid(xg).astype(BF16), gup_ref[...])

    br, bc = _quad_mask(RW_QUAD)
    same_head = (br // RW_HEAD_DIM) == (bc // RW_HEAD_DIM)
    head_ones = jnp.where(same_head, 1.0, 0.0).astype(BF16)

    def head_sum(x):
        pieces = [p[:, sl] for p in _split2(x) for sl in (slice(0, RW_QUAD), slice(RW_QUAD, W))]
        s = _dot(jnp.concatenate(pieces, axis=0), head_ones)
        halves = [s[j * T:(j + 1) * T] + s[(j + 2) * T:(j + 3) * T] for j in range(2)]
        return jnp.concatenate(halves, axis=1)

    kk = k * kk_ref[...]
    kk = kk * jnp.minimum(lax.rsqrt(head_sum(kk * kk)), 1e12)
    k = k * (1.0 + (a - 1.0) * ka_ref[...])
    bonus = head_sum(r * k * rk_ref[...]) * v
    a_vec = -kk
    b_vec = kk * a
    fill()

    g = log_w
    wrow = lax.broadcasted_iota(jnp.int32, (T, W), 0) % C
    d = 1
    while d < C:
        g = g + jnp.where(wrow >= d, pltpu.roll(g, d, 0), 0.0)
        d *= 2
    fill()
    g_last = jnp.concatenate(
        [jnp.broadcast_to(g[c * C + C - 1:c * C + C], (C, W)) for c in range(T // C)], axis=0)
    inv = jnp.exp(-g)
    a_dec = a_vec * jnp.exp(g - log_w)
    r_dec = r * jnp.exp(g)
    b_inv = b_vec * inv
    k_inv = k * inv
    dec = jnp.exp(g_last - g)
    b_dec = b_vec * dec
    k_dec = k * dec
    e_last = jnp.exp(g_last)
    fill()

    eye_bd = br == bc
    tr, tc = _quad_mask(C)
    s_idx = tc % RW_HEAD_DIM
    strict = s_idx < tr
    incl = s_idx <= tr
    eye_q = jnp.where(s_idx == tr, 1.0, 0.0)

    def bd(y):
        return _block_diag(y.astype(BF16), head_ones)


    items = [(slice(c * C, (c + 1) * C), slice(qd * RW_QUAD, (qd + 1) * RW_QUAD))
             for c in range(T // C) for qd in range(W // RW_QUAD)]
    cut = [dict(a_dec=a_dec[rs, ls], r_dec=r_dec[rs, ls], v=v[rs, ls]) for rs, ls in items]

    for it, (rs, ls) in zip(cut, items):
        ar = jnp.concatenate([it["a_dec"], it["r_dec"]], axis=0).astype(BF16)
        lm_b = _dot_nt(ar, bd(b_inv[rs, ls]))
        lm_k = _dot_nt(ar, bd(k_inv[rs, ls]))
        it["l_ab"] = jnp.where(strict, lm_b[:C], 0.0)
        it["m_rb"] = jnp.where(incl, lm_b[C:], 0.0).astype(BF16)
        it["m_rk"] = jnp.where(incl, lm_k[C:], 0.0).astype(BF16)
        it["v_bd"] = bd(it["v"])
        it["lv"] = _dot(jnp.where(strict, lm_k[:C], 0.0).astype(BF16), it["v_bd"])
        it["inv_l"] = eye_q + it["l_ab"]
        it["lp"] = it["l_ab"]

    n_levels = 5
    for level in range(n_levels):
        for it in cut:
            lp_bd = bd(it["lp"])
            if level == 0:
                it["lp"] = _dot(it["lp"].astype(BF16), lp_bd)
            else:
                both = jnp.concatenate([it["inv_l"], it["lp"]], axis=0).astype(BF16)
                both = _dot(both, lp_bd)
                it["inv_l"] = it["inv_l"] + both[:C]
                it["lp"] = both[C:]
    for it in cut:
        it["inv_l"] = it["inv_l"] + _dot(it["inv_l"].astype(BF16), bd(it["lp"]))

    for it, (rs, ls) in zip(cut, items):
        inv_b = it["inv_l"].astype(BF16)
        w1 = _dot(inv_b, bd(it["a_dec"]))
        u0 = _dot(inv_b, bd(it["lv"]))
        it["q_eff"] = (it["r_dec"] + _dot(it["m_rb"], bd(w1))).astype(BF16)
        it["y_intra"] = _dot(it["m_rb"], bd(u0)) + _dot(it["m_rk"], it["v_bd"])
        b_dec_b = b_dec[rs, ls].astype(BF16)
        it["pt_bd"] = (jnp.where(same_head, _dot_tn(b_dec_b, w1.astype(BF16)), 0.0)
                       + jnp.where(eye_bd, e_last[rs, ls][:1], 0.0)).astype(BF16)
        keys = jnp.concatenate([b_dec_b, k_dec[rs, ls].astype(BF16)], axis=0)
        vals = jnp.concatenate([u0, it["v"]], axis=0).astype(BF16)
        it["dt_bd"] = jnp.where(same_head, _dot_tn(keys, vals), 0.0)

    n_quads = W // RW_QUAD
    rows_out = []
    for c in range(T // C):
        ys = []
        for qd in range(n_quads):
            it = cut[c * n_quads + qd]
            st = st_ref[qd].astype(BF16)
            ys.append(it["y_intra"] + _dot(it["q_eff"], st))
            st_ref[qd] = _dot(it["pt_bd"], st) + it["dt_bd"]
        rows_out.append(jnp.concatenate(ys, axis=1))
    y = jnp.concatenate(rows_out, axis=0)
    fill()
    mean = head_sum(y) * (1.0 / RW_HEAD_DIM)
    yc = y - mean
    var = head_sum(yc * yc) * (1.0 / RW_HEAD_DIM)
    y = yc * lax.rsqrt(var + RW_GN_EPS) * gng_ref[...] + gnb_ref[...]
    o_ref[...] = ((y + bonus) * gate).astype(o_ref.dtype)


def _proj_mix_kernel(h_ref, wrw_ref, wsb_ref, whg_ref, wgt_ref,
                     mu_ref, w0_ref, wup_ref, a0_ref, aup_ref, gup_ref, kk_ref, ka_ref, rk_ref,
                     gng_ref, gnb_ref, lbraw_ref, ng_ref,
                     q_ref, k_ref, v_ref, gate_ref, yrw_ref, yhg_ref,
                     prev_ref, rwst_ref, hgst_ref, *, layer, tiles_per_seq, hg_tile):
    @pl.when(pl.program_id(0) % tiles_per_seq == 0)
    def _():
        prev_ref[...] = jnp.zeros_like(prev_ref)
        rwst_ref[...] = jnp.zeros_like(rwst_ref)
        hgst_ref[...] = jnp.zeros_like(hgst_ref)

    hb = h_ref[...].astype(BF16)

    def sb_piece(o_ref, base, c, scale):
        cols = slice(c * MXU_COLS, (c + 1) * MXU_COLS)
        y = _dot(hb, wsb_ref[:, base + c * MXU_COLS:base + (c + 1) * MXU_COLS])
        o_ref[:, cols] = (y if scale is None else y * scale).astype(BF16)

    def gate_piece(c):
        cols = slice(c * MXU_COLS, (c + 1) * MXU_COLS)
        gate_ref[:, cols] = _sigmoid(_dot(hb, wgt_ref[:, cols]))

    pieces = [functools.partial(sb_piece, o_ref, base, c, scale)
              for o_ref, base, scale in ((q_ref, 0, SB_HEAD_DIM ** -0.5), (k_ref, SB_WIDTH, None),
                                         (v_ref, 2 * SB_WIDTH, None))
              for c in range(SB_WIDTH // MXU_COLS)]
    pieces += [functools.partial(gate_piece, c) for c in range(3 * D_MODEL // MXU_COLS)]

    def fill():
        if pieces:
            pieces.pop(0)()

    _rwkv7_tile(_dot(hb, wrw_ref[...]), mu_ref, w0_ref, wup_ref, a0_ref, aup_ref, gup_ref, kk_ref,
                ka_ref, rk_ref, gng_ref, gnb_ref, yrw_ref, prev_ref, rwst_ref, fill)
    u_hg = _dot(hb, whg_ref[...])
    for j in range(hb.shape[0] // hg_tile):
        rows = slice(j * hg_tile, (j + 1) * hg_tile)
        _hgrn2_tile(u_hg[rows], lbraw_ref, ng_ref, yhg_ref, rows, hgst_ref, layer, fill)
    while pieces:
        fill()


def _proj_mix(h, w_rw, w_sb, w_hg, w_gate, rw_params, lb_raw, norm_g, layer, seq, tm, hg_tile):
    n = h.shape[0]

    def row(c):
        return pl.BlockSpec((tm, c), lambda i: (i, 0))

    params = (w_rw, w_sb, w_hg, w_gate) + tuple(rw_params) + (lb_raw, norm_g)
    return pl.pallas_call(
        functools.partial(_proj_mix_kernel, layer=layer, tiles_per_seq=seq // tm, hg_tile=hg_tile),
        grid=(n // tm,),
        in_specs=[row(D_MODEL)] + [_resident(a.shape) for a in params],
        out_specs=[row(SB_WIDTH), row(SB_WIDTH), row(SB_WIDTH), row(3 * D_MODEL),
                   row(RW_WIDTH), row(HG_WIDTH)],
        out_shape=[jax.ShapeDtypeStruct((n, SB_WIDTH), BF16),
                   jax.ShapeDtypeStruct((n, SB_WIDTH), BF16),
                   jax.ShapeDtypeStruct((n, SB_WIDTH), BF16),
                   jax.ShapeDtypeStruct((n, 3 * D_MODEL), F32),
                   jax.ShapeDtypeStruct((n, RW_WIDTH), BF16),
                   jax.ShapeDtypeStruct((n, HG_WIDTH), BF16)],
        scratch_shapes=[pltpu.VMEM((1, RW_U_COLS), F32),
                        pltpu.VMEM((RW_WIDTH // RW_QUAD, RW_QUAD, RW_QUAD), F32),
                        pltpu.VMEM((HG_HEADS, HG_DIM, HG_DIM), F32)],
        compiler_params=pltpu.CompilerParams(dimension_semantics=("arbitrary",),
                                             vmem_limit_bytes=VMEM_LIMIT),
        name="proj_mix",
    )(h, *params)


def _pad_cols(w, width):
    return jnp.pad(w, ((0, 0), (0, width - w.shape[1])))


def _pad_rows(w, height):
    return jnp.pad(w, ((0, height - w.shape[0]), (0, 0)))


def _split_rw_cols(w):
    o = 3 * RW_WIDTH
    parts = [w[:, :o]]
    for size, pad in zip((RW_DECAY_LORA, RW_AAA_LORA, RW_GATE_LORA), RW_LORA_PAD):
        parts.append(_pad_cols(w[:, o:o + size], pad))
        o += size
    return jnp.concatenate(parts, axis=1)


def kernel(x, p, ln_g, ln_b, ffn1_wg, ffn1_wu, ffn1_wd, w_in, rw_mu, rw_w0, rw_w_up, rw_a0, rw_a_up, rw_g_up, rw_k_k, rw_k_a, rw_r_k, rw_gn_g, rw_gn_b, hg_lb_raw, hg_norm_g, w_br_rw, w_br_sb, w_br_hg, w_out, ffn2_wg, ffn2_wu, ffn2_wd, ple_gate, ple_proj):
    bsz, seq, _ = x.shape
    n = bsz * seq
    tm = min(256, seq)
    sb_blk = min(256, seq)
    hg_tile = min(128, seq)
    rw_cols = 3 * RW_WIDTH + RW_DECAY_LORA + RW_AAA_LORA + RW_GATE_LORA
    sb_end = rw_cols + 3 * SB_WIDTH
    hg_end = sb_end + 4 * HG_WIDTH

    h = x.reshape(n, D_MODEL)
    for i in range(DEPTH):
        def vec(a):
            return a[i].reshape(1, -1)

        h = _ffn_ln(h, ffn1_wg[i].astype(BF16), ffn1_wu[i].astype(BF16), ffn1_wd[i].astype(BF16),
                    vec(ln_g[:, 0]), vec(ln_b[:, 0]), tm)

        wi = w_in[i]
        rw_params = (_split_rw_cols(rw_mu[i].reshape(1, -1)), vec(rw_w0),
                     _pad_rows(rw_w_up[i], RW_LORA_PAD[0]).astype(BF16), vec(rw_a0),
                     _pad_rows(rw_a_up[i], RW_LORA_PAD[1]).astype(BF16),
                     _pad_rows(rw_g_up[i], RW_LORA_PAD[2]).astype(BF16),
                     vec(rw_k_k), vec(rw_k_a), vec(rw_r_k), vec(rw_gn_g), vec(rw_gn_b))
        q, k, v, gate, y_rw, y_hg = _proj_mix(
            h, _split_rw_cols(wi[:, :rw_cols]).astype(BF16), wi[:, rw_cols:sb_end].astype(BF16),
            wi[:, sb_end:hg_end].astype(BF16), wi[:, hg_end:].astype(BF16),
            rw_params, hg_lb_raw, vec(hg_norm_g), i, seq, tm, hg_tile)
        y_sb = _sb_attention(q, k, v, bsz, seq, sb_blk)

        h = _merge_ln(h, y_rw, y_sb, y_hg, gate, w_br_rw[i].astype(BF16), w_br_sb[i].astype(BF16),
                      w_br_hg[i].astype(BF16), w_out[i].astype(BF16),
                      vec(ln_g[:, 1]), vec(ln_b[:, 1]), tm)
        h = _ffn_ln(h, ffn2_wg[i].astype(BF16), ffn2_wu[i].astype(BF16), ffn2_wd[i].astype(BF16),
                    vec(ln_g[:, 2]), vec(ln_b[:, 2]), tm)
        h = _ple_ln(h, p[i].reshape(n, PLE_DIM), ple_gate[i].astype(BF16), ple_proj[i].astype(BF16),
                    vec(ln_g[:, 3]), vec(ln_b[:, 3]), tm)
    return h.reshape(bsz, seq, D_MODEL)
```

```python
import functools

import jax
import jax.numpy as jnp
from jax import lax
from jax.experimental import pallas as pl
from jax.experimental.pallas import tpu as pltpu

F32 = jnp.float32
BF16 = jnp.bfloat16

D_MODEL = 1024
DEPTH = 4
PLE_DIM = 256
D_FF = 2816
RW_HEADS = 8
RW_HEAD_DIM = 64
RW_WIDTH = RW_HEADS * RW_HEAD_DIM
RW_CHUNK = 64
RW_DECAY_LORA = 64
RW_AAA_LORA = 64
RW_GATE_LORA = 160
RW_GN_EPS = 64e-5
RW_QUAD = 256
SB_HEADS = 4
SB_HEAD_DIM = 128
SB_WIDTH = SB_HEADS * SB_HEAD_DIM
SB_DEAD_LOG = -105.0
HG_HEADS = 4
HG_DIM = 128
HG_WIDTH = HG_HEADS * HG_DIM
HG_EPS = 1e-5
LOG2_E = 1.4426950408889634
LN_EPS = 1e-5
ALPHA = (2 * DEPTH) ** 0.25

MXU_COLS = 256
FFN_COL_GROUP = 3 * MXU_COLS
RW_LORA_PAD = (128, 128, 256)
RW_U_COLS = 3 * RW_WIDTH + sum(RW_LORA_PAD)
VMEM_LIMIT = 56 * 1024 * 1024


def _resident(shape):
    return pl.BlockSpec(shape, lambda *_: (0,) * len(shape), pipeline_mode=pl.Buffered(1))


def _dot(a, b):
    return jnp.dot(a, b, preferred_element_type=F32)


def _dot_nt(a, b):
    return lax.dot_general(a, b, (((1,), (1,)), ((), ())), preferred_element_type=F32)


def _dot_tn(a, b):
    return lax.dot_general(a, b, (((0,), (0,)), ((), ())), preferred_element_type=F32)


def _split2(x):
    hi = x.astype(BF16)
    lo = (x - hi.astype(F32)).astype(BF16)
    return hi, lo


def _split3(x):
    hi = x.astype(BF16)
    r = x - hi.astype(F32)
    mid = r.astype(BF16)
    lo = (r - mid.astype(F32)).astype(BF16)
    return hi, mid, lo


def _sigmoid(x):
    return 0.5 * jnp.tanh(0.5 * x) + 0.5


def _softplus(x):
    return jnp.maximum(x, 0.0) + jnp.log(1.0 + jnp.exp(-jnp.abs(x)))


def _layer_norm(x, g, b):
    mu = jnp.mean(x, axis=-1, keepdims=True)
    xc = x - mu
    var = jnp.mean(xc * xc, axis=-1, keepdims=True)
    return xc * lax.rsqrt(var + LN_EPS) * g + b


def _swiglu(hb, wg_ref, wu_ref, wd_ref):
    y = None
    for c0 in range(0, D_FF, FFN_COL_GROUP):
        cols = slice(c0, min(c0 + FFN_COL_GROUP, D_FF))
        g = _dot(hb, wg_ref[:, cols])
        u = _dot(hb, wu_ref[:, cols])
        part = _dot((g * _sigmoid(g) * u).astype(BF16), wd_ref[cols, :])
        y = part if y is None else y + part
    return y


def _ffn_ln_kernel(h_ref, wg_ref, wu_ref, wd_ref, lng_ref, lnb_ref, o_ref):
    h = h_ref[...]
    y = _swiglu(h.astype(BF16), wg_ref, wu_ref, wd_ref)
    o_ref[...] = _layer_norm(ALPHA * h + 0.5 * y, lng_ref[...], lnb_ref[...])


def _ffn_ln(h, wg, wu, wd, ln_g, ln_b, tm):
    n = h.shape[0]
    row = pl.BlockSpec((tm, D_MODEL), lambda i: (i, 0))
    return pl.pallas_call(
        _ffn_ln_kernel,
        grid=(n // tm,),
        in_specs=[row, _resident((D_MODEL, D_FF)), _resident((D_MODEL, D_FF)),
                  _resident((D_FF, D_MODEL)), _resident((1, D_MODEL)), _resident((1, D_MODEL))],
        out_specs=row,
        out_shape=jax.ShapeDtypeStruct((n, D_MODEL), F32),
        compiler_params=pltpu.CompilerParams(dimension_semantics=("arbitrary",),
                                             vmem_limit_bytes=VMEM_LIMIT),
        name="ffn_ln",
    )(h, wg, wu, wd, ln_g, ln_b)


def _layer_tail_kernel(h_ref, yrw_ref, ysb_ref, yhg_ref, gate_ref, p_ref,
                       wrw_ref, wsb_ref, whg_ref, wout_ref, wg_ref, wu_ref, wd_ref, pg_ref, pp_ref,
                       lng_ref, lnb_ref, o_ref):
    gate = gate_ref[...]
    merged = (gate[:, :D_MODEL] * _dot(yrw_ref[...], wrw_ref[...])
              + gate[:, D_MODEL:2 * D_MODEL] * _dot(ysb_ref[...], wsb_ref[...])
              + gate[:, 2 * D_MODEL:] * _dot(yhg_ref[...], whg_ref[...]))
    mix = _dot(merged.astype(BF16), wout_ref[...])
    h = _layer_norm(ALPHA * h_ref[...] + mix, lng_ref[0:1], lnb_ref[0:1])

    y = _swiglu(h.astype(BF16), wg_ref, wu_ref, wd_ref)
    h = _layer_norm(ALPHA * h + 0.5 * y, lng_ref[1:2], lnb_ref[1:2])

    ple = _sigmoid(_dot(h.astype(BF16), pg_ref[...])) * _dot(p_ref[...].astype(BF16), pp_ref[...])
    o_ref[...] = _layer_norm(ALPHA * h + ple, lng_ref[2:3], lnb_ref[2:3])


def _layer_tail(h, y_rw, y_sb, y_hg, gate, p, weights, ln_g, ln_b, tm):
    n = h.shape[0]

    def row(c):
        return pl.BlockSpec((tm, c), lambda i: (i, 0))

    return pl.pallas_call(
        _layer_tail_kernel,
        grid=(n // tm,),
        in_specs=[row(D_MODEL), row(RW_WIDTH), row(SB_WIDTH), row(HG_WIDTH), row(3 * D_MODEL),
                  row(PLE_DIM)] + [_resident(w.shape) for w in weights]
                 + [_resident(ln_g.shape), _resident(ln_b.shape)],
        out_specs=row(D_MODEL),
        out_shape=jax.ShapeDtypeStruct((n, D_MODEL), F32),
        compiler_params=pltpu.CompilerParams(dimension_semantics=("arbitrary",),
                                             vmem_limit_bytes=VMEM_LIMIT),
        name="layer_tail",
    )(h, y_rw, y_sb, y_hg, gate, p, *weights, ln_g, ln_b)


def _sb_kernel(q_ref, k_ref, v_ref, o_ref, acc_ref, carry_ref, *, blk):
    i = pl.program_id(1)
    row = lax.broadcasted_iota(jnp.int32, (blk, blk), 0)
    col = lax.broadcasted_iota(jnp.int32, (blk, blk), 1)
    suffix = jnp.where(row >= col, 1.0, 0.0).astype(BF16)
    strict = col < row

    lanes = [slice(h * SB_HEAD_DIM, (h + 1) * SB_HEAD_DIM) for h in range(SB_HEADS)]

    def key_block(start, masked):
        zs = [_dot_nt(q_ref[:, sl], k_ref[pl.ds(start, blk), sl]) for sl in lanes]
        ms = [-(jnp.maximum(z, 0.0) + jnp.log(1.0 + jnp.exp(-jnp.abs(z)))) for z in zs]
        if masked:
            ms = [jnp.where(strict, m, 0.0) for m in ms]
        withins = []
        for m in ms:
            m_hi, m_lo = _split2(m)
            withins.append(_dot(m_hi, suffix) + _dot(m_lo, suffix))
        atts = [jnp.exp(z + w + carry_ref[h]) for h, (z, w) in enumerate(zip(zs, withins))]
        if masked:
            atts = [jnp.where(strict, a, 0.0) for a in atts]
        for h, sl in enumerate(lanes):
            acc_ref[:, sl] += _dot(atts[h].astype(BF16), v_ref[pl.ds(start, blk), sl])
            carry_ref[h] += withins[h][:, 0:1]

    acc_ref[...] = jnp.zeros_like(acc_ref)
    carry_ref[...] = jnp.zeros_like(carry_ref)
    key_block(pl.multiple_of(i * blk, blk), True)

    def more(t):
        return jnp.logical_and(t < i, jnp.max(carry_ref[...]) > SB_DEAD_LOG)

    def body(t):
        key_block(pl.multiple_of((i - 1 - t) * blk, blk), False)
        return t + 1

    lax.while_loop(more, body, 0)
    o_ref[...] = acc_ref[...].astype(o_ref.dtype)


def _sb_attention(q, k, v, bsz, seq, blk):
    n = bsz * seq
    nq = seq // blk
    qspec = pl.BlockSpec((blk, SB_WIDTH), lambda b, i: (b * nq + i, 0))
    kvspec = pl.BlockSpec((seq, SB_WIDTH), lambda b, i: (b, 0))
    return pl.pallas_call(
        functools.partial(_sb_kernel, blk=blk),
        grid=(bsz, nq),
        in_specs=[qspec, kvspec, kvspec],
        out_specs=qspec,
        out_shape=jax.ShapeDtypeStruct((n, SB_WIDTH), BF16),
        scratch_shapes=[pltpu.VMEM((blk, SB_WIDTH), F32), pltpu.VMEM((SB_HEADS, blk, 1), F32)],
        compiler_params=pltpu.CompilerParams(dimension_semantics=("arbitrary", "arbitrary"),
                                             vmem_limit_bytes=VMEM_LIMIT),
        name="stick_breaking",
    )(q, k, v)


def _hgrn2_tile(u_ref, lbraw_ref, ng_ref, o_ref, rows, st_ref, layer):
    u = u_ref[rows]
    tile = u.shape[0]
    raw = lbraw_ref[...]
    e = jnp.exp(raw - jnp.max(raw, axis=0, keepdims=True))
    sm = e / jnp.sum(e, axis=0, keepdims=True)
    lb = jnp.zeros((1, HG_WIDTH), F32)
    for l in range(1, layer + 1):
        lb = lb + sm[l:l + 1]

    q = _sigmoid(u[:, :HG_WIDTH])
    f_raw = u[:, HG_WIDTH:2 * HG_WIDTH]
    val = u[:, 2 * HG_WIDTH:3 * HG_WIDTH]
    out_gate = u[:, 3 * HG_WIDTH:]

    half_tanh = 0.5 * jnp.tanh(0.5 * f_raw)
    sig_pos = 0.5 + half_tanh
    sig_neg = 0.5 - half_tanh
    log_sig = jnp.minimum(f_raw, 0.0) - jnp.log(1.0 + jnp.exp(-jnp.abs(f_raw)))
    if layer == 0:
        log_f = log_sig
    else:
        log_f = jnp.where(lb > 0.0, jnp.log(lb + (1.0 - lb) * sig_pos), log_sig)
    kg = (1.0 - lb) * sig_neg
    yield

    pos = lax.broadcasted_iota(jnp.int32, (tile, HG_WIDTH), 0)
    ti = lax.broadcasted_iota(jnp.int32, (tile, tile), 0)
    si = lax.broadcasted_iota(jnp.int32, (tile, tile), 1)
    prefix = jnp.where(si <= ti, 1.0, 0.0).astype(BF16)
    g = sum(_dot(prefix, piece) for piece in _split3(log_f))

    heads = [slice(h * HG_DIM, (h + 1) * HG_DIM) for h in range(HG_HEADS)]
    qb = q.astype(BF16)
    kgb = kg.astype(BF16)
    att = [jnp.where(ti == si, _dot_nt(qb[:, sl], kgb[:, sl]), 0.0) for sl in heads]
    last = g
    w = 1
    while w < tile:
        upper = (pos & (2 * w - 1)) >= w
        g_ref = jnp.where(upper, pltpu.roll(last, w, 0), last)
        last = jnp.where(upper, last, pltpu.roll(last, tile - w, 0))
        decay = jnp.exp2(jnp.abs(g - g_ref) * -LOG2_E)
        z = (jnp.where(upper, q, kg) * decay).astype(BF16)
        pair = ((ti & -(2 * w)) == (si & -(2 * w))) & ((ti & w) != 0) & ((si & w) == 0)
        att = [a + jnp.where(pair, _dot_nt(z[:, sl], z[:, sl]), 0.0) for a, sl in zip(att, heads)]
        yield
        w *= 2

    valb = val.astype(BF16)
    qg = (q * jnp.exp(g)).astype(BF16)
    g_last = g[tile - 1:tile]
    kd = (kg * jnp.exp(g_last - g)).astype(BF16)
    gam = jnp.exp(g_last)
    ng = ng_ref[...]
    yield
    for h, sl in enumerate(heads):
        st = st_ref[h]
        o = _dot(att[h].astype(BF16), valb[:, sl]) + _dot_nt(qg[:, sl], st.astype(BF16))
        o = o * lax.rsqrt(jnp.mean(o * o, axis=-1, keepdims=True) + HG_EPS) * ng[:, sl]
        gate = out_gate[:, sl]
        o_ref[rows, sl] = (o * (gate * _sigmoid(gate))).astype(o_ref.dtype)
        st_ref[h] = st * gam[:, sl] + _dot_tn(valb[:, sl], kd[:, sl])


def _quad_mask(rows):
    r = lax.broadcasted_iota(jnp.int32, (rows, RW_QUAD), 0)
    c = lax.broadcasted_iota(jnp.int32, (rows, RW_QUAD), 1)
    return r, c


def _block_diag(y, head_ones):
    return jnp.concatenate([y, y, y, y], axis=0) * head_ones


def _rwkv7_tile(u, mu_ref, w0_ref, wup_ref, a0_ref, aup_ref, gup_ref, kk_ref, ka_ref,
                rk_ref, gng_ref, gnb_ref, o_ref, prev_ref, st_ref, fill):
    C = RW_CHUNK
    W = RW_WIDTH
    T = u.shape[0]
    trow = lax.broadcasted_iota(jnp.int32, u.shape, 0)
    shifted = jnp.where(trow == 0, prev_ref[...], pltpu.roll(u, 1, 0))
    prev_ref[...] = u[T - 1:T]
    u = u + (shifted - u) * mu_ref[...]
    fill(4)

    r = u[:, :W]
    k = u[:, W:2 * W]
    v = u[:, 2 * W:3 * W]
    o0 = 3 * W
    xw = u[:, o0:o0 + RW_LORA_PAD[0]]
    xa = u[:, o0 + RW_LORA_PAD[0]:o0 + RW_LORA_PAD[0] + RW_LORA_PAD[1]]
    xg = u[:, o0 + RW_LORA_PAD[0] + RW_LORA_PAD[1]:]

    w_log = -_softplus(-(w0_ref[...] + _dot(jnp.tanh(xw).astype(BF16), wup_ref[...]))) - 0.5
    log_w = -jnp.exp(w_log)
    a = _sigmoid(a0_ref[...] + _dot(xa.astype(BF16), aup_ref[...]))
    gate = _dot(_sigmoid(xg).astype(BF16), gup_ref[...])

    br, bc = _quad_mask(RW_QUAD)
    same_head = (br // RW_HEAD_DIM) == (bc // RW_HEAD_DIM)
    head_ones = jnp.where(same_head, 1.0, 0.0).astype(BF16)

    def head_sum(x):
        pieces = [p[:, sl] for p in _split2(x) for sl in (slice(0, RW_QUAD), slice(RW_QUAD, W))]
        s = _dot(jnp.concatenate(pieces, axis=0), head_ones)
        halves = [s[j * T:(j + 1) * T] + s[(j + 2) * T:(j + 3) * T] for j in range(2)]
        return jnp.concatenate(halves, axis=1)

    kk = k * kk_ref[...]
    kk = kk * jnp.minimum(lax.rsqrt(head_sum(kk * kk)), 1e12)
    k = k * (1.0 + (a - 1.0) * ka_ref[...])
    bonus = head_sum(r * k * rk_ref[...]) * v
    a_vec = -kk
    b_vec = kk * a
    fill(4)

    g = log_w
    wrow = lax.broadcasted_iota(jnp.int32, (T, W), 0) % C
    d = 1
    while d < C:
        g = g + jnp.where(wrow >= d, pltpu.roll(g, d, 0), 0.0)
        d *= 2
    fill(4)
    g_last = jnp.concatenate(
        [jnp.broadcast_to(g[c * C + C - 1:c * C + C], (C, W)) for c in range(T // C)], axis=0)
    inv = jnp.exp(-g)
    a_dec = a_vec * jnp.exp(g - log_w)
    r_dec = r * jnp.exp(g)
    b_inv = b_vec * inv
    k_inv = k * inv
    dec = jnp.exp(g_last - g)
    b_dec = b_vec * dec
    k_dec = k * dec
    e_last = jnp.exp(g_last)
    fill(4)

    eye_bd = br == bc
    tr, tc = _quad_mask(C)
    s_idx = tc % RW_HEAD_DIM
    strict = s_idx < tr
    incl = s_idx <= tr
    eye_q = jnp.where(s_idx == tr, 1.0, 0.0)

    def bd(y):
        return _block_diag(y.astype(BF16), head_ones)


    items = [(slice(c * C, (c + 1) * C), slice(qd * RW_QUAD, (qd + 1) * RW_QUAD))
             for c in range(T // C) for qd in range(W // RW_QUAD)]
    cut = [dict(a_dec=a_dec[rs, ls], r_dec=r_dec[rs, ls], v=v[rs, ls]) for rs, ls in items]

    for it, (rs, ls) in zip(cut, items):
        ar = jnp.concatenate([it["a_dec"], it["r_dec"]], axis=0).astype(BF16)
        lm_b = _dot_nt(ar, bd(b_inv[rs, ls]))
        lm_k = _dot_nt(ar, bd(k_inv[rs, ls]))
        it["l_ab"] = jnp.where(strict, lm_b[:C], 0.0)
        it["m_rb"] = jnp.where(incl, lm_b[C:], 0.0).astype(BF16)
        it["m_rk"] = jnp.where(incl, lm_k[C:], 0.0).astype(BF16)
        it["v_bd"] = bd(it["v"])
        it["lv"] = _dot(jnp.where(strict, lm_k[:C], 0.0).astype(BF16), it["v_bd"])
        it["inv_l"] = eye_q + it["l_ab"]
        it["lp"] = it["l_ab"]
        fill()

    n_levels = 5
    for level in range(n_levels):
        for it in cut:
            lp_bd = bd(it["lp"])
            if level == 0:
                it["lp"] = _dot(it["lp"].astype(BF16), lp_bd)
            else:
                both = jnp.concatenate([it["inv_l"], it["lp"]], axis=0).astype(BF16)
                both = _dot(both, lp_bd)
                it["inv_l"] = it["inv_l"] + both[:C]
                it["lp"] = both[C:]
        fill(3)
    for it in cut:
        it["inv_l"] = it["inv_l"] + _dot(it["inv_l"].astype(BF16), bd(it["lp"]))
    fill(3)

    for it, (rs, ls) in zip(cut, items):
        inv_b = it["inv_l"].astype(BF16)
        w1 = _dot(inv_b, bd(it["a_dec"]))
        u0 = _dot(inv_b, bd(it["lv"]))
        it["q_eff"] = (it["r_dec"] + _dot(it["m_rb"], bd(w1))).astype(BF16)
        it["y_intra"] = _dot(it["m_rb"], bd(u0)) + _dot(it["m_rk"], it["v_bd"])
        b_dec_b = b_dec[rs, ls].astype(BF16)
        it["pt_bd"] = (jnp.where(same_head, _dot_tn(b_dec_b, w1.astype(BF16)), 0.0)
                       + jnp.where(eye_bd, e_last[rs, ls][:1], 0.0)).astype(BF16)
        keys = jnp.concatenate([b_dec_b, k_dec[rs, ls].astype(BF16)], axis=0)
        vals = jnp.concatenate([u0, it["v"]], axis=0).astype(BF16)
        it["dt_bd"] = jnp.where(same_head, _dot_tn(keys, vals), 0.0)
        fill()

    n_quads = W // RW_QUAD
    rows_out = []
    for c in range(T // C):
        ys = []
        for qd in range(n_quads):
            it = cut[c * n_quads + qd]
            st = st_ref[qd].astype(BF16)
            ys.append(it["y_intra"] + _dot(it["q_eff"], st))
            st_ref[qd] = _dot(it["pt_bd"], st) + it["dt_bd"]
        rows_out.append(jnp.concatenate(ys, axis=1))
        fill()
    y = jnp.concatenate(rows_out, axis=0)
    mean = head_sum(y) * (1.0 / RW_HEAD_DIM)
    yc = y - mean
    var = head_sum(yc * yc) * (1.0 / RW_HEAD_DIM)
    y = yc * lax.rsqrt(var + RW_GN_EPS) * gng_ref[...] + gnb_ref[...]
    o_ref[...] = ((y + bonus) * gate).astype(o_ref.dtype)


def _proj_mix_kernel(h_ref, wrw_ref, wsb_ref, whg_ref, wgt_ref,
                     mu_ref, w0_ref, wup_ref, a0_ref, aup_ref, gup_ref, kk_ref, ka_ref, rk_ref,
                     gng_ref, gnb_ref, lbraw_ref, ng_ref,
                     q_ref, k_ref, v_ref, gate_ref, yrw_ref, yhg_ref,
                     prev_ref, rwst_ref, hgst_ref, urw_ref, uhg_ref,
                     *, layer, tiles_per_seq, hg_tile):
    i = pl.program_id(0)

    @pl.when(i == 0)
    def _():
        urw_ref[...] = jnp.zeros_like(urw_ref)
        uhg_ref[...] = jnp.zeros_like(uhg_ref)

    @pl.when(jnp.maximum(i - 1, 0) % tiles_per_seq == 0)
    def _():
        prev_ref[...] = jnp.zeros_like(prev_ref)
        rwst_ref[...] = jnp.zeros_like(rwst_ref)
        hgst_ref[...] = jnp.zeros_like(hgst_ref)

    hb = h_ref[...].astype(BF16)

    def piece(w_ref, c, store):
        cols = slice(c * MXU_COLS, (c + 1) * MXU_COLS)
        store(cols, _dot(hb, w_ref[:, cols]))

    def store_rw(cols, y):
        urw_ref[:, cols] = y

    def store_hg(cols, y):
        uhg_ref[:, cols] = y

    def store_gate(cols, y):
        gate_ref[:, cols] = _sigmoid(y)

    def store_sb(cols, y):
        o_ref = (q_ref, k_ref, v_ref)[cols.start // SB_WIDTH]
        local = slice(cols.start % SB_WIDTH, cols.start % SB_WIDTH + MXU_COLS)
        o_ref[:, local] = (y * (SB_HEAD_DIM ** -0.5) if o_ref is q_ref else y).astype(BF16)

    def pieces_of(w_ref, store):
        return [functools.partial(piece, w_ref, c, store) for c in range(w_ref.shape[1] // MXU_COLS)]

    def hgrn2_stages():
        tiles = []
        for j in range(hb.shape[0] // hg_tile):
            rows = slice(j * hg_tile, (j + 1) * hg_tile)
            tiles.append(_hgrn2_tile(uhg_ref, lbraw_ref, ng_ref, yhg_ref, rows, hgst_ref, layer))
            next(tiles[-1])
            yield
        for t in tiles:
            yield from t
            yield

    hgrn2 = hgrn2_stages()
    hgrn2_step = functools.partial(next, hgrn2, None)
    for _ in range(hb.shape[0] // hg_tile):
        hgrn2_step()
    queue = pieces_of(whg_ref, store_hg) + pieces_of(wrw_ref, store_rw)
    for p in pieces_of(wsb_ref, store_sb) + pieces_of(wgt_ref, store_gate):
        queue += [hgrn2_step, p]

    def fill(n=1):
        for _ in range(n):
            if queue:
                queue.pop(0)()

    _rwkv7_tile(urw_ref[...], mu_ref, w0_ref, wup_ref, a0_ref, aup_ref, gup_ref, kk_ref,
                ka_ref, rk_ref, gng_ref, gnb_ref, yrw_ref, prev_ref, rwst_ref, fill)
    fill(len(queue))
    for _ in hgrn2:
        pass


def _proj_mix(h, w_rw, w_sb, w_hg, w_gate, rw_params, lb_raw, norm_g, layer, seq, tm, hg_tile):
    n = h.shape[0]
    nt = n // tm

    def projected(c):
        return pl.BlockSpec((tm, c), lambda i: (jnp.minimum(i, nt - 1), 0))

    def mixed(c):
        return pl.BlockSpec((tm, c), lambda i: (jnp.maximum(i - 1, 0), 0))

    params = (w_rw, w_sb, w_hg, w_gate) + tuple(rw_params) + (lb_raw, norm_g)
    return pl.pallas_call(
        functools.partial(_proj_mix_kernel, layer=layer, tiles_per_seq=seq // tm, hg_tile=hg_tile),
        grid=(nt + 1,),
        in_specs=[projected(D_MODEL)] + [_resident(a.shape) for a in params],
        out_specs=[projected(SB_WIDTH), projected(SB_WIDTH), projected(SB_WIDTH),
                   projected(3 * D_MODEL), mixed(RW_WIDTH), mixed(HG_WIDTH)],
        out_shape=[jax.ShapeDtypeStruct((n, SB_WIDTH), BF16),
                   jax.ShapeDtypeStruct((n, SB_WIDTH), BF16),
                   jax.ShapeDtypeStruct((n, SB_WIDTH), BF16),
                   jax.ShapeDtypeStruct((n, 3 * D_MODEL), F32),
                   jax.ShapeDtypeStruct((n, RW_WIDTH), BF16),
                   jax.ShapeDtypeStruct((n, HG_WIDTH), BF16)],
        scratch_shapes=[pltpu.VMEM((1, RW_U_COLS), F32),
                        pltpu.VMEM((RW_WIDTH // RW_QUAD, RW_QUAD, RW_QUAD), F32),
                        pltpu.VMEM((HG_HEADS, HG_DIM, HG_DIM), F32),
                        pltpu.VMEM((tm, RW_U_COLS), F32),
                        pltpu.VMEM((tm, 4 * HG_WIDTH), F32)],
        compiler_params=pltpu.CompilerParams(dimension_semantics=("arbitrary",),
                                             vmem_limit_bytes=VMEM_LIMIT),
        name="proj_mix",
    )(h, *params)


def _pad_cols(w, width):
    return jnp.pad(w, ((0, 0), (0, width - w.shape[1])))


def _pad_rows(w, height):
    return jnp.pad(w, ((0, height - w.shape[0]), (0, 0)))


def _split_rw_cols(w):
    o = 3 * RW_WIDTH
    parts = [w[:, :o]]
    for size, pad in zip((RW_DECAY_LORA, RW_AAA_LORA, RW_GATE_LORA), RW_LORA_PAD):
        parts.append(_pad_cols(w[:, o:o + size], pad))
        o += size
    return jnp.concatenate(parts, axis=1)


def kernel(x, p, ln_g, ln_b, ffn1_wg, ffn1_wu, ffn1_wd, w_in, rw_mu, rw_w0, rw_w_up, rw_a0, rw_a_up, rw_g_up, rw_k_k, rw_k_a, rw_r_k, rw_gn_g, rw_gn_b, hg_lb_raw, hg_norm_g, w_br_rw, w_br_sb, w_br_hg, w_out, ffn2_wg, ffn2_wu, ffn2_wd, ple_gate, ple_proj):
    bsz, seq, _ = x.shape
    n = bsz * seq
    tm = min(256, seq)
    sb_blk = min(256, seq)
    hg_tile = min(128, seq)
    rw_cols = 3 * RW_WIDTH + RW_DECAY_LORA + RW_AAA_LORA + RW_GATE_LORA
    sb_end = rw_cols + 3 * SB_WIDTH
    hg_end = sb_end + 4 * HG_WIDTH

    h = x.reshape(n, D_MODEL)
    for i in range(DEPTH):
        def vec(a):
            return a[i].reshape(1, -1)

        h = _ffn_ln(h, ffn1_wg[i].astype(BF16), ffn1_wu[i].astype(BF16), ffn1_wd[i].astype(BF16),
                    vec(ln_g[:, 0]), vec(ln_b[:, 0]), min(512, n))

        wi = w_in[i]
        rw_params = (_split_rw_cols(rw_mu[i].reshape(1, -1)), vec(rw_w0),
                     _pad_rows(rw_w_up[i], RW_LORA_PAD[0]).astype(BF16), vec(rw_a0),
                     _pad_rows(rw_a_up[i], RW_LORA_PAD[1]).astype(BF16),
                     _pad_rows(rw_g_up[i], RW_LORA_PAD[2]).astype(BF16),
                     vec(rw_k_k), vec(rw_k_a), vec(rw_r_k), vec(rw_gn_g), vec(rw_gn_b))
        q, k, v, gate, y_rw, y_hg = _proj_mix(
            h, _split_rw_cols(wi[:, :rw_cols]).astype(BF16), wi[:, rw_cols:sb_end].astype(BF16),
            wi[:, sb_end:hg_end].astype(BF16), wi[:, hg_end:].astype(BF16),
            rw_params, hg_lb_raw, vec(hg_norm_g), i, seq, tm, hg_tile)
        y_sb = _sb_attention(q, k, v, bsz, seq, sb_blk)

        tail_weights = [w[i].astype(BF16) for w in (w_br_rw, w_br_sb, w_br_hg, w_out, ffn2_wg, ffn2_wu,
                                                    ffn2_wd, ple_gate, ple_proj)]
        h = _layer_tail(h, y_rw, y_sb, y_hg, gate, p[i].reshape(n, PLE_DIM), tail_weights,
                        ln_g[i, 1:], ln_b[i, 1:], tm)
    return h.reshape(bsz, seq, D_MODEL)
```

```python
import functools

import jax
import jax.numpy as jnp
from jax import lax
from jax.experimental import pallas as pl
from jax.experimental.pallas import tpu as pltpu

F32 = jnp.float32
BF16 = jnp.bfloat16

D_MODEL = 1024
DEPTH = 4
PLE_DIM = 256
D_FF = 2816
RW_HEADS = 8
RW_HEAD_DIM = 64
RW_WIDTH = RW_HEADS * RW_HEAD_DIM
RW_CHUNK = 64
RW_DECAY_LORA = 64
RW_AAA_LORA = 64
RW_GATE_LORA = 160
RW_GN_EPS = 64e-5
RW_QUAD = 256
SB_HEADS = 4
SB_HEAD_DIM = 128
SB_WIDTH = SB_HEADS * SB_HEAD_DIM
SB_DEAD_LOG = -105.0
HG_HEADS = 4
HG_DIM = 128
HG_WIDTH = HG_HEADS * HG_DIM
HG_EPS = 1e-5
LOG2_E = 1.4426950408889634
LN_EPS = 1e-5
ALPHA = (2 * DEPTH) ** 0.25

MXU_COLS = 256
FFN_COL_GROUP = 3 * MXU_COLS
RW_LORA_PAD = (128, 128, 256)
RW_U_COLS = 3 * RW_WIDTH + sum(RW_LORA_PAD)
VMEM_LIMIT = 56 * 1024 * 1024


def _resident(shape):
    return pl.BlockSpec(shape, lambda *_: (0,) * len(shape), pipeline_mode=pl.Buffered(1))


def _dot(a, b):
    return jnp.dot(a, b, preferred_element_type=F32)


def _dot_nt(a, b):
    return lax.dot_general(a, b, (((1,), (1,)), ((), ())), preferred_element_type=F32)


def _dot_tn(a, b):
    return lax.dot_general(a, b, (((0,), (0,)), ((), ())), preferred_element_type=F32)


def _split2(x):
    hi = x.astype(BF16)
    lo = (x - hi.astype(F32)).astype(BF16)
    return hi, lo


def _split3(x):
    hi = x.astype(BF16)
    r = x - hi.astype(F32)
    mid = r.astype(BF16)
    lo = (r - mid.astype(F32)).astype(BF16)
    return hi, mid, lo


def _sigmoid(x):
    return 0.5 * jnp.tanh(0.5 * x) + 0.5


def _softplus(x):
    return jnp.maximum(x, 0.0) + jnp.log(1.0 + jnp.exp(-jnp.abs(x)))


def _layer_norm(x, g, b):
    mu = jnp.mean(x, axis=-1, keepdims=True)
    xc = x - mu
    var = jnp.mean(xc * xc, axis=-1, keepdims=True)
    return xc * lax.rsqrt(var + LN_EPS) * g + b


def _swiglu(hb, wg_ref, wu_ref, wd_ref):
    y = None
    for c0 in range(0, D_FF, FFN_COL_GROUP):
        cols = slice(c0, min(c0 + FFN_COL_GROUP, D_FF))
        g = _dot(hb, wg_ref[:, cols])
        u = _dot(hb, wu_ref[:, cols])
        part = _dot((g * _sigmoid(g) * u).astype(BF16), wd_ref[cols, :])
        y = part if y is None else y + part
    return y


def _ffn_ln_kernel(h_ref, wg_ref, wu_ref, wd_ref, lng_ref, lnb_ref, o_ref):
    h = h_ref[...]
    y = _swiglu(h.astype(BF16), wg_ref, wu_ref, wd_ref)
    o_ref[...] = _layer_norm(ALPHA * h + 0.5 * y, lng_ref[...], lnb_ref[...])


def _ffn_ln(h, wg, wu, wd, ln_g, ln_b, tm):
    n = h.shape[0]
    row = pl.BlockSpec((tm, D_MODEL), lambda i: (i, 0))
    return pl.pallas_call(
        _ffn_ln_kernel,
        grid=(n // tm,),
        in_specs=[row, _resident((D_MODEL, D_FF)), _resident((D_MODEL, D_FF)),
                  _resident((D_FF, D_MODEL)), _resident((1, D_MODEL)), _resident((1, D_MODEL))],
        out_specs=row,
        out_shape=jax.ShapeDtypeStruct((n, D_MODEL), F32),
        compiler_params=pltpu.CompilerParams(dimension_semantics=("arbitrary",),
                                             vmem_limit_bytes=VMEM_LIMIT),
        name="ffn_ln",
    )(h, wg, wu, wd, ln_g, ln_b)


def _layer_tail_kernel(h_ref, yrw_ref, ysb_ref, yhg_ref, gate_ref, p_ref,
                       wrw_ref, wsb_ref, whg_ref, wout_ref, wg_ref, wu_ref, wd_ref, pg_ref, pp_ref,
                       lng_ref, lnb_ref, o_ref):
    gate = gate_ref[...]
    merged = (gate[:, :D_MODEL] * _dot(yrw_ref[...], wrw_ref[...])
              + gate[:, D_MODEL:2 * D_MODEL] * _dot(ysb_ref[...], wsb_ref[...])
              + gate[:, 2 * D_MODEL:] * _dot(yhg_ref[...], whg_ref[...]))
    mix = _dot(merged.astype(BF16), wout_ref[...])
    h = _layer_norm(ALPHA * h_ref[...] + mix, lng_ref[0:1], lnb_ref[0:1])

    y = _swiglu(h.astype(BF16), wg_ref, wu_ref, wd_ref)
    h = _layer_norm(ALPHA * h + 0.5 * y, lng_ref[1:2], lnb_ref[1:2])

    ple = _sigmoid(_dot(h.astype(BF16), pg_ref[...])) * _dot(p_ref[...].astype(BF16), pp_ref[...])
    o_ref[...] = _layer_norm(ALPHA * h + ple, lng_ref[2:3], lnb_ref[2:3])


def _layer_tail(h, y_rw, y_sb, y_hg, gate, p, weights, ln_g, ln_b, tm):
    n = h.shape[0]

    def row(c):
        return pl.BlockSpec((tm, c), lambda i: (i, 0))

    return pl.pallas_call(
        _layer_tail_kernel,
        grid=(n // tm,),
        in_specs=[row(D_MODEL), row(RW_WIDTH), row(SB_WIDTH), row(HG_WIDTH), row(3 * D_MODEL),
                  row(PLE_DIM)] + [_resident(w.shape) for w in weights]
                 + [_resident(ln_g.shape), _resident(ln_b.shape)],
        out_specs=row(D_MODEL),
        out_shape=jax.ShapeDtypeStruct((n, D_MODEL), F32),
        compiler_params=pltpu.CompilerParams(dimension_semantics=("arbitrary",),
                                             vmem_limit_bytes=VMEM_LIMIT),
        name="layer_tail",
    )(h, y_rw, y_sb, y_hg, gate, p, *weights, ln_g, ln_b)


def _sb_kernel(q_ref, k_ref, v_ref, o_ref, acc_ref, carry_ref, *, blk):
    i = pl.program_id(1)
    row = lax.broadcasted_iota(jnp.int32, (blk, blk), 0)
    col = lax.broadcasted_iota(jnp.int32, (blk, blk), 1)
    suffix = jnp.where(row >= col, 1.0, 0.0).astype(BF16)
    strict = col < row

    lanes = [slice(h * SB_HEAD_DIM, (h + 1) * SB_HEAD_DIM) for h in range(SB_HEADS)]

    def key_block(start, masked):
        zs = [_dot_nt(q_ref[:, sl], k_ref[pl.ds(start, blk), sl]) for sl in lanes]
        ms = [-(jnp.maximum(z, 0.0) + jnp.log(1.0 + jnp.exp(-jnp.abs(z)))) for z in zs]
        if masked:
            ms = [jnp.where(strict, m, 0.0) for m in ms]
        withins = []
        for m in ms:
            m_hi, m_lo = _split2(m)
            withins.append(_dot(m_hi, suffix) + _dot(m_lo, suffix))
        atts = [jnp.exp(z + w + carry_ref[h]) for h, (z, w) in enumerate(zip(zs, withins))]
        if masked:
            atts = [jnp.where(strict, a, 0.0) for a in atts]
        for h, sl in enumerate(lanes):
            acc_ref[:, sl] += _dot(atts[h].astype(BF16), v_ref[pl.ds(start, blk), sl])
            carry_ref[h] += withins[h][:, 0:1]

    acc_ref[...] = jnp.zeros_like(acc_ref)
    carry_ref[...] = jnp.zeros_like(carry_ref)
    key_block(pl.multiple_of(i * blk, blk), True)

    def more(t):
        return jnp.logical_and(t < i, jnp.max(carry_ref[...]) > SB_DEAD_LOG)

    def body(t):
        key_block(pl.multiple_of((i - 1 - t) * blk, blk), False)
        return t + 1

    lax.while_loop(more, body, 0)
    o_ref[...] = acc_ref[...].astype(o_ref.dtype)


def _sb_attention(q, k, v, bsz, seq, blk):
    n = bsz * seq
    nq = seq // blk
    qspec = pl.BlockSpec((blk, SB_WIDTH), lambda b, i: (b * nq + i, 0))
    kvspec = pl.BlockSpec((seq, SB_WIDTH), lambda b, i: (b, 0))
    return pl.pallas_call(
        functools.partial(_sb_kernel, blk=blk),
        grid=(bsz, nq),
        in_specs=[qspec, kvspec, kvspec],
        out_specs=qspec,
        out_shape=jax.ShapeDtypeStruct((n, SB_WIDTH), BF16),
        scratch_shapes=[pltpu.VMEM((blk, SB_WIDTH), F32), pltpu.VMEM((SB_HEADS, blk, 1), F32)],
        compiler_params=pltpu.CompilerParams(dimension_semantics=("arbitrary", "arbitrary"),
                                             vmem_limit_bytes=VMEM_LIMIT),
        name="stick_breaking",
    )(q, k, v)


def _hgrn2_tile(u_ref, lbraw_ref, ng_ref, o_ref, rows, st_ref, layer):
    u = u_ref[rows]
    tile = u.shape[0]
    raw = lbraw_ref[...]
    e = jnp.exp(raw - jnp.max(raw, axis=0, keepdims=True))
    sm = e / jnp.sum(e, axis=0, keepdims=True)
    lb = jnp.zeros((1, HG_WIDTH), F32)
    for l in range(1, layer + 1):
        lb = lb + sm[l:l + 1]

    q = _sigmoid(u[:, :HG_WIDTH])
    f_raw = u[:, HG_WIDTH:2 * HG_WIDTH]
    val = u[:, 2 * HG_WIDTH:3 * HG_WIDTH]
    out_gate = u[:, 3 * HG_WIDTH:]

    half_tanh = 0.5 * jnp.tanh(0.5 * f_raw)
    sig_pos = 0.5 + half_tanh
    sig_neg = 0.5 - half_tanh
    log_sig = jnp.minimum(f_raw, 0.0) - jnp.log(1.0 + jnp.exp(-jnp.abs(f_raw)))
    if layer == 0:
        log_f = log_sig
    else:
        log_f = jnp.where(lb > 0.0, jnp.log(lb + (1.0 - lb) * sig_pos), log_sig)
    kg = (1.0 - lb) * sig_neg
    yield

    pos = lax.broadcasted_iota(jnp.int32, (tile, HG_WIDTH), 0)
    ti = lax.broadcasted_iota(jnp.int32, (tile, tile), 0)
    si = lax.broadcasted_iota(jnp.int32, (tile, tile), 1)
    prefix = jnp.where(si <= ti, 1.0, 0.0).astype(BF16)
    g = sum(_dot(prefix, piece) for piece in _split3(log_f))

    heads = [slice(h * HG_DIM, (h + 1) * HG_DIM) for h in range(HG_HEADS)]
    qb = q.astype(BF16)
    kgb = kg.astype(BF16)
    att = [jnp.where(ti == si, _dot_nt(qb[:, sl], kgb[:, sl]), 0.0) for sl in heads]
    last = g
    w = 1
    while w < tile:
        upper = (pos & (2 * w - 1)) >= w
        g_ref = jnp.where(upper, pltpu.roll(last, w, 0), last)
        last = jnp.where(upper, last, pltpu.roll(last, tile - w, 0))
        decay = jnp.exp2(jnp.abs(g - g_ref) * -LOG2_E)
        z = (jnp.where(upper, q, kg) * decay).astype(BF16)
        pair = ((ti & -(2 * w)) == (si & -(2 * w))) & ((ti & w) != 0) & ((si & w) == 0)
        att = [a + jnp.where(pair, _dot_nt(z[:, sl], z[:, sl]), 0.0) for a, sl in zip(att, heads)]
        yield
        w *= 2

    valb = val.astype(BF16)
    qg = (q * jnp.exp(g)).astype(BF16)
    g_last = g[tile - 1:tile]
    kd = (kg * jnp.exp(g_last - g)).astype(BF16)
    gam = jnp.exp(g_last)
    ng = ng_ref[...]
    yield
    for h, sl in enumerate(heads):
        st = st_ref[h]
        o = _dot(att[h].astype(BF16), valb[:, sl]) + _dot_nt(qg[:, sl], st.astype(BF16))
        o = o * lax.rsqrt(jnp.mean(o * o, axis=-1, keepdims=True) + HG_EPS) * ng[:, sl]
        gate = out_gate[:, sl]
        o_ref[rows, sl] = (o * (gate * _sigmoid(gate))).astype(o_ref.dtype)
        st_ref[h] = st * gam[:, sl] + _dot_tn(valb[:, sl], kd[:, sl])


def _quad_mask(rows):
    r = lax.broadcasted_iota(jnp.int32, (rows, RW_QUAD), 0)
    c = lax.broadcasted_iota(jnp.int32, (rows, RW_QUAD), 1)
    return r, c


def _block_diag(y, head_ones):
    return jnp.concatenate([y, y, y, y], axis=0) * head_ones


def _rwkv7_tile(u, mu_ref, w0_ref, wup_ref, a0_ref, aup_ref, gup_ref, kk_ref, ka_ref,
                rk_ref, gng_ref, gnb_ref, o_ref, prev_ref, st_ref, fill):
    C = RW_CHUNK
    W = RW_WIDTH
    T = u.shape[0]
    trow = lax.broadcasted_iota(jnp.int32, u.shape, 0)
    shifted = jnp.where(trow == 0, prev_ref[...], pltpu.roll(u, 1, 0))
    prev_ref[...] = u[T - 1:T]
    u = u + (shifted - u) * mu_ref[...]
    fill(4)

    r = u[:, :W]
    k = u[:, W:2 * W]
    v = u[:, 2 * W:3 * W]
    o0 = 3 * W
    xw = u[:, o0:o0 + RW_LORA_PAD[0]]
    xa = u[:, o0 + RW_LORA_PAD[0]:o0 + RW_LORA_PAD[0] + RW_LORA_PAD[1]]
    xg = u[:, o0 + RW_LORA_PAD[0] + RW_LORA_PAD[1]:]

    w_log = -_softplus(-(w0_ref[...] + _dot(jnp.tanh(xw).astype(BF16), wup_ref[...]))) - 0.5
    log_w = -jnp.exp(w_log)
    a = _sigmoid(a0_ref[...] + _dot(xa.astype(BF16), aup_ref[...]))
    gate = _dot(_sigmoid(xg).astype(BF16), gup_ref[...])

    br, bc = _quad_mask(RW_QUAD)
    same_head = (br // RW_HEAD_DIM) == (bc // RW_HEAD_DIM)
    head_ones = jnp.where(same_head, 1.0, 0.0).astype(BF16)

    def head_sum(x):
        pieces = [p[:, sl] for p in _split2(x) for sl in (slice(0, RW_QUAD), slice(RW_QUAD, W))]
        s = _dot(jnp.concatenate(pieces, axis=0), head_ones)
        halves = [s[j * T:(j + 1) * T] + s[(j + 2) * T:(j + 3) * T] for j in range(2)]
        return jnp.concatenate(halves, axis=1)

    kk = k * kk_ref[...]
    kk = kk * jnp.minimum(lax.rsqrt(head_sum(kk * kk)), 1e12)
    k = k * (1.0 + (a - 1.0) * ka_ref[...])
    bonus = head_sum(r * k * rk_ref[...]) * v
    a_vec = -kk
    b_vec = kk * a
    fill(4)

    g = log_w
    wrow = lax.broadcasted_iota(jnp.int32, (T, W), 0) % C
    d = 1
    while d < C:
        g = g + jnp.where(wrow >= d, pltpu.roll(g, d, 0), 0.0)
        d *= 2
    fill(4)
    g_last = jnp.concatenate(
        [jnp.broadcast_to(g[c * C + C - 1:c * C + C], (C, W)) for c in range(T // C)], axis=0)
    inv = jnp.exp(-g)
    a_dec = a_vec * jnp.exp(g - log_w)
    r_dec = r * jnp.exp(g)
    b_inv = b_vec * inv
    k_inv = k * inv
    dec = jnp.exp(g_last - g)
    b_dec = b_vec * dec
    k_dec = k * dec
    e_last = jnp.exp(g_last)
    fill(4)

    eye_bd = br == bc
    tr, tc = _quad_mask(C)
    s_idx = tc % RW_HEAD_DIM
    strict = s_idx < tr
    incl = s_idx <= tr
    eye_q = jnp.where(s_idx == tr, 1.0, 0.0)

    def bd(y):
        return _block_diag(y.astype(BF16), head_ones)


    items = [(slice(c * C, (c + 1) * C), slice(qd * RW_QUAD, (qd + 1) * RW_QUAD))
             for c in range(T // C) for qd in range(W // RW_QUAD)]
    cut = [dict(a_dec=a_dec[rs, ls], r_dec=r_dec[rs, ls], v=v[rs, ls]) for rs, ls in items]

    for it, (rs, ls) in zip(cut, items):
        ar = jnp.concatenate([it["a_dec"], it["r_dec"]], axis=0).astype(BF16)
        lm_b = _dot_nt(ar, bd(b_inv[rs, ls]))
        lm_k = _dot_nt(ar, bd(k_inv[rs, ls]))
        it["l_ab"] = jnp.where(strict, lm_b[:C], 0.0)
        it["m_rb"] = jnp.where(incl, lm_b[C:], 0.0).astype(BF16)
        it["m_rk"] = jnp.where(incl, lm_k[C:], 0.0).astype(BF16)
        it["v_bd"] = bd(it["v"])
        it["lv"] = _dot(jnp.where(strict, lm_k[:C], 0.0).astype(BF16), it["v_bd"])
        it["inv_l"] = eye_q + it["l_ab"]
        it["lp"] = it["l_ab"]
        fill()

    n_levels = 5
    for level in range(n_levels):
        for it in cut:
            lp_bd = bd(it["lp"])
            if level == 0:
                it["lp"] = _dot(it["lp"].astype(BF16), lp_bd)
            else:
                both = jnp.concatenate([it["inv_l"], it["lp"]], axis=0).astype(BF16)
                both = _dot(both, lp_bd)
                it["inv_l"] = it["inv_l"] + both[:C]
                it["lp"] = both[C:]
        fill(3)
    for it in cut:
        it["inv_l"] = it["inv_l"] + _dot(it["inv_l"].astype(BF16), bd(it["lp"]))
    fill(3)

    for it, (rs, ls) in zip(cut, items):
        inv_b = it["inv_l"].astype(BF16)
        w1 = _dot(inv_b, bd(it["a_dec"]))
        u0 = _dot(inv_b, bd(it["lv"]))
        it["q_eff"] = (it["r_dec"] + _dot(it["m_rb"], bd(w1))).astype(BF16)
        it["y_intra"] = _dot(it["m_rb"], bd(u0)) + _dot(it["m_rk"], it["v_bd"])
        b_dec_b = b_dec[rs, ls].astype(BF16)
        it["pt_bd"] = (jnp.where(same_head, _dot_tn(b_dec_b, w1.astype(BF16)), 0.0)
                       + jnp.where(eye_bd, e_last[rs, ls][:1], 0.0)).astype(BF16)
        keys = jnp.concatenate([b_dec_b, k_dec[rs, ls].astype(BF16)], axis=0)
        vals = jnp.concatenate([u0, it["v"]], axis=0).astype(BF16)
        it["dt_bd"] = jnp.where(same_head, _dot_tn(keys, vals), 0.0)
        fill()

    n_quads = W // RW_QUAD
    rows_out = []
    for c in range(T // C):
        ys = []
        for qd in range(n_quads):
            it = cut[c * n_quads + qd]
            st = st_ref[qd].astype(BF16)
            ys.append(it["y_intra"] + _dot(it["q_eff"], st))
            st_ref[qd] = _dot(it["pt_bd"], st) + it["dt_bd"]
        rows_out.append(jnp.concatenate(ys, axis=1))
        fill()
    y = jnp.concatenate(rows_out, axis=0)
    mean = head_sum(y) * (1.0 / RW_HEAD_DIM)
    yc = y - mean
    var = head_sum(yc * yc) * (1.0 / RW_HEAD_DIM)
    y = yc * lax.rsqrt(var + RW_GN_EPS) * gng_ref[...] + gnb_ref[...]
    o_ref[...] = ((y + bonus) * gate).astype(o_ref.dtype)


def _proj_mix_kernel(h_ref, wrw_ref, wsb_ref, whg_ref, wgt_ref,
                     mu_ref, w0_ref, wup_ref, a0_ref, aup_ref, gup_ref, kk_ref, ka_ref, rk_ref,
                     gng_ref, gnb_ref, lbraw_ref, ng_ref,
                     q_ref, k_ref, v_ref, gate_ref, yrw_ref, yhg_ref,
                     prev_ref, rwst_ref, hgst_ref, urw_ref, uhg_ref,
                     *, layer, tiles_per_seq, hg_tile):
    i = pl.program_id(0)

    @pl.when(i == 0)
    def _():
        urw_ref[...] = jnp.zeros_like(urw_ref)
        uhg_ref[...] = jnp.zeros_like(uhg_ref)

    @pl.when(jnp.maximum(i - 1, 0) % tiles_per_seq == 0)
    def _():
        prev_ref[...] = jnp.zeros_like(prev_ref)
        rwst_ref[...] = jnp.zeros_like(rwst_ref)
        hgst_ref[...] = jnp.zeros_like(hgst_ref)

    hb = h_ref[...].astype(BF16)

    def piece(w_ref, c, store):
        cols = slice(c * MXU_COLS, (c + 1) * MXU_COLS)
        store(cols, _dot(hb, w_ref[:, cols]))

    def store_rw(cols, y):
        urw_ref[:, cols] = y

    def store_hg(cols, y):
        uhg_ref[:, cols] = y

    def store_gate(cols, y):
        gate_ref[:, cols] = _sigmoid(y).astype(BF16)

    def store_sb(cols, y):
        o_ref = (q_ref, k_ref, v_ref)[cols.start // SB_WIDTH]
        local = slice(cols.start % SB_WIDTH, cols.start % SB_WIDTH + MXU_COLS)
        o_ref[:, local] = (y * (SB_HEAD_DIM ** -0.5) if o_ref is q_ref else y).astype(BF16)

    def pieces_of(w_ref, store):
        return [functools.partial(piece, w_ref, c, store) for c in range(w_ref.shape[1] // MXU_COLS)]

    def hgrn2_stages():
        tiles = []
        for j in range(hb.shape[0] // hg_tile):
            rows = slice(j * hg_tile, (j + 1) * hg_tile)
            tiles.append(_hgrn2_tile(uhg_ref, lbraw_ref, ng_ref, yhg_ref, rows, hgst_ref, layer))
            next(tiles[-1])
            yield
        for t in tiles:
            yield from t
            yield

    hgrn2 = hgrn2_stages()
    hgrn2_step = functools.partial(next, hgrn2, None)
    for _ in range(hb.shape[0] // hg_tile):
        hgrn2_step()
    queue = pieces_of(whg_ref, store_hg) + pieces_of(wrw_ref, store_rw)
    for p in pieces_of(wsb_ref, store_sb) + pieces_of(wgt_ref, store_gate):
        queue += [hgrn2_step, p]

    def fill(n=1):
        for _ in range(n):
            if queue:
                queue.pop(0)()

    _rwkv7_tile(urw_ref[...], mu_ref, w0_ref, wup_ref, a0_ref, aup_ref, gup_ref, kk_ref,
                ka_ref, rk_ref, gng_ref, gnb_ref, yrw_ref, prev_ref, rwst_ref, fill)
    fill(len(queue))
    for _ in hgrn2:
        pass


def _proj_mix(h, w_rw, w_sb, w_hg, w_gate, rw_params, lb_raw, norm_g, layer, seq, tm, hg_tile):
    n = h.shape[0]
    nt = n // tm

    def projected(c):
        return pl.BlockSpec((tm, c), lambda i: (jnp.minimum(i, nt - 1), 0))

    def mixed(c):
        return pl.BlockSpec((tm, c), lambda i: (jnp.maximum(i - 1, 0), 0))

    params = (w_rw, w_sb, w_hg, w_gate) + tuple(rw_params) + (lb_raw, norm_g)
    return pl.pallas_call(
        functools.partial(_proj_mix_kernel, layer=layer, tiles_per_seq=seq // tm, hg_tile=hg_tile),
        grid=(nt + 1,),
        in_specs=[projected(D_MODEL)] + [_resident(a.shape) for a in params],
        out_specs=[projected(SB_WIDTH), projected(SB_WIDTH), projected(SB_WIDTH),
                   projected(3 * D_MODEL), mixed(RW_WIDTH), mixed(HG_WIDTH)],
        out_shape=[jax.ShapeDtypeStruct((n, SB_WIDTH), BF16),
                   jax.ShapeDtypeStruct((n, SB_WIDTH), BF16),
                   jax.ShapeDtypeStruct((n, SB_WIDTH), BF16),
                   jax.ShapeDtypeStruct((n, 3 * D_MODEL), BF16),
                   jax.ShapeDtypeStruct((n, RW_WIDTH), BF16),
                   jax.ShapeDtypeStruct((n, HG_WIDTH), BF16)],
        scratch_shapes=[pltpu.VMEM((1, RW_U_COLS), F32),
                        pltpu.VMEM((RW_WIDTH // RW_QUAD, RW_QUAD, RW_QUAD), F32),
                        pltpu.VMEM((HG_HEADS, HG_DIM, HG_DIM), F32),
                        pltpu.VMEM((tm, RW_U_COLS), F32),
                        pltpu.VMEM((tm, 4 * HG_WIDTH), F32)],
        compiler_params=pltpu.CompilerParams(dimension_semantics=("arbitrary",),
                                             vmem_limit_bytes=VMEM_LIMIT),
        name="proj_mix",
    )(h, *params)


def _pad_axis(w, axis, size):
    pads = [(0, 0)] * w.ndim
    pads[axis] = (0, size - w.shape[axis])
    return jnp.pad(w, pads)


def _split_rw_cols(w):
    o = 3 * RW_WIDTH
    parts = [w[..., :o]]
    for size, pad in zip((RW_DECAY_LORA, RW_AAA_LORA, RW_GATE_LORA), RW_LORA_PAD):
        parts.append(_pad_axis(w[..., o:o + size], -1, pad))
        o += size
    return jnp.concatenate(parts, axis=-1)


def kernel(x, p, ln_g, ln_b, ffn1_wg, ffn1_wu, ffn1_wd, w_in, rw_mu, rw_w0, rw_w_up, rw_a0, rw_a_up, rw_g_up, rw_k_k, rw_k_a, rw_r_k, rw_gn_g, rw_gn_b, hg_lb_raw, hg_norm_g, w_br_rw, w_br_sb, w_br_hg, w_out, ffn2_wg, ffn2_wu, ffn2_wd, ple_gate, ple_proj):
    bsz, seq, _ = x.shape
    n = bsz * seq
    tm = min(256, seq)
    sb_blk = min(256, seq)
    hg_tile = min(128, seq)
    rw_cols = 3 * RW_WIDTH + RW_DECAY_LORA + RW_AAA_LORA + RW_GATE_LORA
    sb_end = rw_cols + 3 * SB_WIDTH
    hg_end = sb_end + 4 * HG_WIDTH

    ffn1 = [w.astype(BF16) for w in (ffn1_wg, ffn1_wu, ffn1_wd)]
    proj = [_split_rw_cols(w_in[..., :rw_cols]).astype(BF16), w_in[..., rw_cols:sb_end].astype(BF16),
            w_in[..., sb_end:hg_end].astype(BF16), w_in[..., hg_end:].astype(BF16)]
    loras = [_pad_axis(w, 1, pad).astype(BF16)
             for w, pad in zip((rw_w_up, rw_a_up, rw_g_up), RW_LORA_PAD)]
    rw_mu_cols = _split_rw_cols(rw_mu)
    tail = [w.astype(BF16) for w in (w_br_rw, w_br_sb, w_br_hg, w_out, ffn2_wg, ffn2_wu, ffn2_wd,
                                     ple_gate, ple_proj)]

    h = x.reshape(n, D_MODEL)
    for i in range(DEPTH):
        def vec(a):
            return a[i].reshape(1, -1)

        h = _ffn_ln(h, *(w[i] for w in ffn1), vec(ln_g[:, 0]), vec(ln_b[:, 0]), min(512, n))
        rw_params = (vec(rw_mu_cols), vec(rw_w0), loras[0][i], vec(rw_a0), loras[1][i], loras[2][i],
                     vec(rw_k_k), vec(rw_k_a), vec(rw_r_k), vec(rw_gn_g), vec(rw_gn_b))
        q, k, v, gate, y_rw, y_hg = _proj_mix(h, *(w[i] for w in proj), rw_params, hg_lb_raw,
                                              vec(hg_norm_g), i, seq, tm, hg_tile)
        y_sb = _sb_attention(q, k, v, bsz, seq, sb_blk)
        h = _layer_tail(h, y_rw, y_sb, y_hg, gate, p[i].reshape(n, PLE_DIM), [w[i] for w in tail],
                        ln_g[i, 1:], ln_b[i, 1:], min(512, n))
    return h.reshape(bsz, seq, D_MODEL)
```

```python
import functools

import jax
import jax.numpy as jnp
from jax import lax
from jax.experimental import pallas as pl
from jax.experimental.pallas import tpu as pltpu

F32 = jnp.float32
BF16 = jnp.bfloat16

D_MODEL = 1024
DEPTH = 4
PLE_DIM = 256
D_FF = 2816
RW_HEADS = 8
RW_HEAD_DIM = 64
RW_WIDTH = RW_HEADS * RW_HEAD_DIM
RW_CHUNK = 64
RW_DECAY_LORA = 64
RW_AAA_LORA = 64
RW_GATE_LORA = 160
RW_GN_EPS = 64e-5
RW_QUAD = 256
SB_HEADS = 4
SB_HEAD_DIM = 128
SB_WIDTH = SB_HEADS * SB_HEAD_DIM
SB_DEAD_LOG = -105.0
HG_HEADS = 4
HG_DIM = 128
HG_WIDTH = HG_HEADS * HG_DIM
HG_EPS = 1e-5
LOG2_E = 1.4426950408889634
LN_EPS = 1e-5
ALPHA = (2 * DEPTH) ** 0.25

MXU_COLS = 256
FFN_COL_GROUP = 6 * MXU_COLS
RW_LORA_PAD = (128, 128, 256)
RW_U_COLS = 3 * RW_WIDTH + sum(RW_LORA_PAD)
VMEM_LIMIT = 56 * 1024 * 1024


def _resident(shape):
    return pl.BlockSpec(shape, lambda *_: (0,) * len(shape), pipeline_mode=pl.Buffered(1))


def _dot(a, b):
    return jnp.dot(a, b, preferred_element_type=F32)


def _dot_nt(a, b):
    return lax.dot_general(a, b, (((1,), (1,)), ((), ())), preferred_element_type=F32)


def _dot_tn(a, b):
    return lax.dot_general(a, b, (((0,), (0,)), ((), ())), preferred_element_type=F32)


def _split2(x):
    hi = x.astype(BF16)
    lo = (x - hi.astype(F32)).astype(BF16)
    return hi, lo


def _split3(x):
    hi = x.astype(BF16)
    r = x - hi.astype(F32)
    mid = r.astype(BF16)
    lo = (r - mid.astype(F32)).astype(BF16)
    return hi, mid, lo


def _sigmoid(x):
    return 0.5 * jnp.tanh(0.5 * x) + 0.5


def _softplus(x):
    return jnp.maximum(x, 0.0) + jnp.log(1.0 + jnp.exp(-jnp.abs(x)))


def _layer_norm(x, g, b):
    mu = jnp.mean(x, axis=-1, keepdims=True)
    xc = x - mu
    var = jnp.mean(xc * xc, axis=-1, keepdims=True)
    return xc * lax.rsqrt(var + LN_EPS) * g + b


def _swiglu(hb, wg_ref, wu_ref, wd_ref, out):
    y = None
    for c0 in range(0, D_FF, FFN_COL_GROUP):
        cols = slice(c0, min(c0 + FFN_COL_GROUP, D_FF))
        g = _dot(hb, wg_ref[:, cols])
        u = _dot(hb, wu_ref[:, cols])
        yield
        part = _dot((g * _sigmoid(g) * u).astype(BF16), wd_ref[cols, :])
        y = part if y is None else y + part
        yield
    out.append(y)


def _round_robin(generators):
    live = list(generators)
    while live:
        for gen in list(live):
            if next(gen, StopIteration) is StopIteration:
                live.remove(gen)


def _row_groups(rows):
    half = rows // 2 if rows % 16 == 0 else rows
    return [slice(r, r + half) for r in range(0, rows, half)]


def _ffn_ln_kernel(h_ref, wg_ref, wu_ref, wd_ref, lng_ref, lnb_ref, o_ref):
    def rows_stage(rows):
        h = h_ref[rows]
        y = []
        yield from _swiglu(h.astype(BF16), wg_ref, wu_ref, wd_ref, y)
        o_ref[rows] = _layer_norm(ALPHA * h + 0.5 * y[0], lng_ref[...], lnb_ref[...])

    _round_robin(rows_stage(rows) for rows in _row_groups(h_ref.shape[0]))


def _ffn_ln(h, wg, wu, wd, ln_g, ln_b, tm):
    n = h.shape[0]
    row = pl.BlockSpec((tm, D_MODEL), lambda i: (i, 0))
    return pl.pallas_call(
        _ffn_ln_kernel,
        grid=(n // tm,),
        in_specs=[row, _resident((D_MODEL, D_FF)), _resident((D_MODEL, D_FF)),
                  _resident((D_FF, D_MODEL)), _resident((1, D_MODEL)), _resident((1, D_MODEL))],
        out_specs=row,
        out_shape=jax.ShapeDtypeStruct((n, D_MODEL), F32),
        compiler_params=pltpu.CompilerParams(dimension_semantics=("arbitrary",),
                                             vmem_limit_bytes=VMEM_LIMIT),
        name="ffn_ln",
    )(h, wg, wu, wd, ln_g, ln_b)


def _layer_tail_kernel(h_ref, yrw_ref, ysb_ref, yhg_ref, gate_ref, p_ref,
                       wrw_ref, wsb_ref, whg_ref, wout_ref, wg_ref, wu_ref, wd_ref, pg_ref, pp_ref,
                       lng_ref, lnb_ref, o_ref):
    def rows_stage(rows):
        branches = [_dot(y_ref[rows], w_ref[...]) for y_ref, w_ref in
                    ((yrw_ref, wrw_ref), (ysb_ref, wsb_ref), (yhg_ref, whg_ref))]
        yield
        gate = gate_ref[rows]
        merged = sum(gate[:, j * D_MODEL:(j + 1) * D_MODEL] * b for j, b in enumerate(branches))
        mix = _dot(merged.astype(BF16), wout_ref[...])
        yield
        h = _layer_norm(ALPHA * h_ref[rows] + mix, lng_ref[0:1], lnb_ref[0:1])
        y = []
        yield from _swiglu(h.astype(BF16), wg_ref, wu_ref, wd_ref, y)
        h = _layer_norm(ALPHA * h + 0.5 * y[0], lng_ref[1:2], lnb_ref[1:2])
        ple_gate = _dot(h.astype(BF16), pg_ref[...])
        ple_in = _dot(p_ref[rows].astype(BF16), pp_ref[...])
        yield
        o_ref[rows] = _layer_norm(ALPHA * h + _sigmoid(ple_gate) * ple_in, lng_ref[2:3], lnb_ref[2:3])

    _round_robin(rows_stage(rows) for rows in _row_groups(h_ref.shape[0]))


def _layer_tail(h, y_rw, y_sb, y_hg, gate, p, weights, ln_g, ln_b, tm):
    n = h.shape[0]

    def row(c):
        return pl.BlockSpec((tm, c), lambda i: (i, 0))

    return pl.pallas_call(
        _layer_tail_kernel,
        grid=(n // tm,),
        in_specs=[row(D_MODEL), row(RW_WIDTH), row(SB_WIDTH), row(HG_WIDTH), row(3 * D_MODEL),
                  row(PLE_DIM)] + [_resident(w.shape) for w in weights]
                 + [_resident(ln_g.shape), _resident(ln_b.shape)],
        out_specs=row(D_MODEL),
        out_shape=jax.ShapeDtypeStruct((n, D_MODEL), F32),
        compiler_params=pltpu.CompilerParams(dimension_semantics=("arbitrary",),
                                             vmem_limit_bytes=VMEM_LIMIT),
        name="layer_tail",
    )(h, y_rw, y_sb, y_hg, gate, p, *weights, ln_g, ln_b)


def _sb_kernel(q_ref, k_ref, v_ref, o_ref, acc_ref, carry_ref, *, blk):
    i = pl.program_id(1)
    row = lax.broadcasted_iota(jnp.int32, (blk, blk), 0)
    col = lax.broadcasted_iota(jnp.int32, (blk, blk), 1)
    suffix = jnp.where(row >= col, 1.0, 0.0).astype(BF16)
    strict = col < row

    lanes = [slice(h * SB_HEAD_DIM, (h + 1) * SB_HEAD_DIM) for h in range(SB_HEADS)]

    def key_block(start, masked):
        zs = [_dot_nt(q_ref[:, sl], k_ref[pl.ds(start, blk), sl]) for sl in lanes]
        ms = [-(jnp.maximum(z, 0.0) + jnp.log(1.0 + jnp.exp(-jnp.abs(z)))) for z in zs]
        if masked:
            ms = [jnp.where(strict, m, 0.0) for m in ms]
        withins = [_dot(m.astype(BF16), suffix) for m in ms]
        atts = [jnp.exp(z + w + carry_ref[h]) for h, (z, w) in enumerate(zip(zs, withins))]
        if masked:
            atts = [jnp.where(strict, a, 0.0) for a in atts]
        for h, sl in enumerate(lanes):
            acc_ref[:, sl] += _dot(atts[h].astype(BF16), v_ref[pl.ds(start, blk), sl])
            carry_ref[h] += withins[h][:, 0:1]

    acc_ref[...] = jnp.zeros_like(acc_ref)
    carry_ref[...] = jnp.zeros_like(carry_ref)
    key_block(pl.multiple_of(i * blk, blk), True)

    def more(t):
        return jnp.logical_and(t < i, jnp.max(carry_ref[...]) > SB_DEAD_LOG)

    def body(t):
        key_block(pl.multiple_of((i - 1 - t) * blk, blk), False)
        return t + 1

    lax.while_loop(more, body, 0)
    o_ref[...] = acc_ref[...].astype(o_ref.dtype)


def _sb_attention(q, k, v, bsz, seq, blk):
    n = bsz * seq
    nq = seq // blk
    qspec = pl.BlockSpec((blk, SB_WIDTH), lambda b, i: (b * nq + i, 0))
    kvspec = pl.BlockSpec((seq, SB_WIDTH), lambda b, i: (b, 0))
    return pl.pallas_call(
        functools.partial(_sb_kernel, blk=blk),
        grid=(bsz, nq),
        in_specs=[qspec, kvspec, kvspec],
        out_specs=qspec,
        out_shape=jax.ShapeDtypeStruct((n, SB_WIDTH), BF16),
        scratch_shapes=[pltpu.VMEM((blk, SB_WIDTH), F32), pltpu.VMEM((SB_HEADS, blk, 1), F32)],
        compiler_params=pltpu.CompilerParams(dimension_semantics=("arbitrary", "arbitrary"),
                                             vmem_limit_bytes=VMEM_LIMIT),
        name="stick_breaking",
    )(q, k, v)


def _hgrn2_tile(u_ref, lbraw_ref, ng_ref, o_ref, rows, st_ref, layer):
    u = u_ref[rows]
    tile = u.shape[0]
    raw = lbraw_ref[...]
    e = jnp.exp(raw - jnp.max(raw, axis=0, keepdims=True))
    sm = e / jnp.sum(e, axis=0, keepdims=True)
    lb = jnp.zeros((1, HG_WIDTH), F32)
    for l in range(1, layer + 1):
        lb = lb + sm[l:l + 1]

    q = _sigmoid(u[:, :HG_WIDTH])
    f_raw = u[:, HG_WIDTH:2 * HG_WIDTH]
    val = u[:, 2 * HG_WIDTH:3 * HG_WIDTH]
    out_gate = u[:, 3 * HG_WIDTH:]

    half_tanh = 0.5 * jnp.tanh(0.5 * f_raw)
    sig_pos = 0.5 + half_tanh
    sig_neg = 0.5 - half_tanh
    log_sig = jnp.minimum(f_raw, 0.0) - jnp.log(1.0 + jnp.exp(-jnp.abs(f_raw)))
    if layer == 0:
        log_f = log_sig
    else:
        log_f = jnp.where(lb > 0.0, jnp.log(lb + (1.0 - lb) * sig_pos), log_sig)
    kg = (1.0 - lb) * sig_neg
    yield

    pos = lax.broadcasted_iota(jnp.int32, (tile, HG_WIDTH), 0)
    ti = lax.broadcasted_iota(jnp.int32, (tile, tile), 0)
    si = lax.broadcasted_iota(jnp.int32, (tile, tile), 1)
    prefix = jnp.where(si <= ti, 1.0, 0.0).astype(BF16)
    g = sum(_dot(prefix, piece) for piece in _split3(log_f))

    heads = [slice(h * HG_DIM, (h + 1) * HG_DIM) for h in range(HG_HEADS)]
    qb = q.astype(BF16)
    kgb = kg.astype(BF16)
    att = [jnp.where(ti == si, _dot_nt(qb[:, sl], kgb[:, sl]), 0.0) for sl in heads]
    last = g
    w = 1
    while w < tile:
        upper = (pos & (2 * w - 1)) >= w
        g_ref = jnp.where(upper, pltpu.roll(last, w, 0), last)
        last = jnp.where(upper, last, pltpu.roll(last, tile - w, 0))
        decay = jnp.exp2(jnp.abs(g - g_ref) * -LOG2_E)
        z = (jnp.where(upper, q, kg) * decay).astype(BF16)
        pair = ((ti & -(2 * w)) == (si & -(2 * w))) & ((ti & w) != 0) & ((si & w) == 0)
        att = [a + jnp.where(pair, _dot_nt(z[:, sl], z[:, sl]), 0.0) for a, sl in zip(att, heads)]
        yield
        w *= 2

    valb = val.astype(BF16)
    qg = (q * jnp.exp(g)).astype(BF16)
    g_last = g[tile - 1:tile]
    kd = (kg * jnp.exp(g_last - g)).astype(BF16)
    gam = jnp.exp(g_last)
    ng = ng_ref[...]
    yield
    for h, sl in enumerate(heads):
        st = st_ref[h]
        o = _dot(att[h].astype(BF16), valb[:, sl]) + _dot_nt(qg[:, sl], st.astype(BF16))
        o = o * lax.rsqrt(jnp.mean(o * o, axis=-1, keepdims=True) + HG_EPS) * ng[:, sl]
        gate = out_gate[:, sl]
        o_ref[rows, sl] = (o * (gate * _sigmoid(gate))).astype(o_ref.dtype)
        st_ref[h] = st * gam[:, sl] + _dot_tn(valb[:, sl], kd[:, sl])


def _quad_mask(rows):
    r = lax.broadcasted_iota(jnp.int32, (rows, RW_QUAD), 0)
    c = lax.broadcasted_iota(jnp.int32, (rows, RW_QUAD), 1)
    return r, c


def _block_diag(y, head_ones):
    return jnp.concatenate([y, y, y, y], axis=0) * head_ones


def _rwkv7_tile(u, mu_ref, w0_ref, wup_ref, a0_ref, aup_ref, gup_ref, kk_ref, ka_ref,
                rk_ref, gng_ref, gnb_ref, o_ref, prev_ref, st_ref, fill):
    C = RW_CHUNK
    W = RW_WIDTH
    T = u.shape[0]
    trow = lax.broadcasted_iota(jnp.int32, u.shape, 0)
    shifted = jnp.where(trow == 0, prev_ref[...], pltpu.roll(u, 1, 0))
    prev_ref[...] = u[T - 1:T]
    u = u + (shifted - u) * mu_ref[...]
    fill(4)

    r = u[:, :W]
    k = u[:, W:2 * W]
    v = u[:, 2 * W:3 * W]
    o0 = 3 * W
    xw = u[:, o0:o0 + RW_LORA_PAD[0]]
    xa = u[:, o0 + RW_LORA_PAD[0]:o0 + RW_LORA_PAD[0] + RW_LORA_PAD[1]]
    xg = u[:, o0 + RW_LORA_PAD[0] + RW_LORA_PAD[1]:]

    w_log = -_softplus(-(w0_ref[...] + _dot(jnp.tanh(xw).astype(BF16), wup_ref[...]))) - 0.5
    log_w = -jnp.exp(w_log)
    a = _sigmoid(a0_ref[...] + _dot(xa.astype(BF16), aup_ref[...]))
    gate = _dot(_sigmoid(xg).astype(BF16), gup_ref[...])

    br, bc = _quad_mask(RW_QUAD)
    same_head = (br // RW_HEAD_DIM) == (bc // RW_HEAD_DIM)
    head_ones = jnp.where(same_head, 1.0, 0.0).astype(BF16)

    def head_sum(x):
        pieces = [p[:, sl] for p in _split2(x) for sl in (slice(0, RW_QUAD), slice(RW_QUAD, W))]
        s = _dot(jnp.concatenate(pieces, axis=0), head_ones)
        halves = [s[j * T:(j + 1) * T] + s[(j + 2) * T:(j + 3) * T] for j in range(2)]
        return jnp.concatenate(halves, axis=1)

    kk = k * kk_ref[...]
    kk = kk * jnp.minimum(lax.rsqrt(head_sum(kk * kk)), 1e12)
    k = k * (1.0 + (a - 1.0) * ka_ref[...])
    bonus = head_sum(r * k * rk_ref[...]) * v
    a_vec = -kk
    b_vec = kk * a
    fill(4)

    g = log_w
    wrow = lax.broadcasted_iota(jnp.int32, (T, W), 0) % C
    d = 1
    while d < C:
        g = g + jnp.where(wrow >= d, pltpu.roll(g, d, 0), 0.0)
        d *= 2
    fill(4)
    g_last = jnp.concatenate(
        [jnp.broadcast_to(g[c * C + C - 1:c * C + C], (C, W)) for c in range(T // C)], axis=0)
    inv = jnp.exp(-g)
    a_dec = a_vec * jnp.exp(g - log_w)
    r_dec = r * jnp.exp(g)
    b_inv = b_vec * inv
    k_inv = k * inv
    dec = jnp.exp(g_last - g)
    b_dec = b_vec * dec
    k_dec = k * dec
    e_last = jnp.exp(g_last)
    fill(4)

    eye_bd = br == bc
    tr, tc = _quad_mask(C)
    s_idx = tc % RW_HEAD_DIM
    strict = s_idx < tr
    incl = s_idx <= tr
    eye_q = jnp.where(s_idx == tr, 1.0, 0.0)

    def bd(y):
        return _block_diag(y.astype(BF16), head_ones)


    items = [(slice(c * C, (c + 1) * C), slice(qd * RW_QUAD, (qd + 1) * RW_QUAD))
             for c in range(T // C) for qd in range(W // RW_QUAD)]
    cut = [dict(a_dec=a_dec[rs, ls], r_dec=r_dec[rs, ls], v=v[rs, ls]) for rs, ls in items]

    for it, (rs, ls) in zip(cut, items):
        ar = jnp.concatenate([it["a_dec"], it["r_dec"]], axis=0).astype(BF16)
        lm_b = _dot_nt(ar, bd(b_inv[rs, ls]))
        lm_k = _dot_nt(ar, bd(k_inv[rs, ls]))
        it["l_ab"] = jnp.where(strict, lm_b[:C], 0.0)
        it["m_rb"] = jnp.where(incl, lm_b[C:], 0.0).astype(BF16)
        it["m_rk"] = jnp.where(incl, lm_k[C:], 0.0).astype(BF16)
        it["v_bd"] = bd(it["v"])
        it["lv"] = _dot(jnp.where(strict, lm_k[:C], 0.0).astype(BF16), it["v_bd"])
        it["inv_l"] = eye_q + it["l_ab"]
        it["lp"] = it["l_ab"]
        fill()

    n_levels = 5
    for level in range(n_levels):
        for it in cut:
            lp_bd = bd(it["lp"])
            if level == 0:
                it["lp"] = _dot(it["lp"].astype(BF16), lp_bd)
            else:
                both = jnp.concatenate([it["inv_l"], it["lp"]], axis=0).astype(BF16)
                both = _dot(both, lp_bd)
                it["inv_l"] = it["inv_l"] + both[:C]
                it["lp"] = both[C:]
        fill(3)
    for it in cut:
        it["inv_l"] = it["inv_l"] + _dot(it["inv_l"].astype(BF16), bd(it["lp"]))
    fill(3)

    for it, (rs, ls) in zip(cut, items):
        inv_b = it["inv_l"].astype(BF16)
        w1 = _dot(inv_b, bd(it["a_dec"]))
        u0 = _dot(inv_b, bd(it["lv"]))
        it["q_eff"] = (it["r_dec"] + _dot(it["m_rb"], bd(w1))).astype(BF16)
        it["y_intra"] = _dot(it["m_rb"], bd(u0)) + _dot(it["m_rk"], it["v_bd"])
        b_dec_b = b_dec[rs, ls].astype(BF16)
        it["pt_bd"] = (jnp.where(same_head, _dot_tn(b_dec_b, w1.astype(BF16)), 0.0)
                       + jnp.where(eye_bd, e_last[rs, ls][:1], 0.0)).astype(BF16)
        keys = jnp.concatenate([b_dec_b, k_dec[rs, ls].astype(BF16)], axis=0)
        vals = jnp.concatenate([u0, it["v"]], axis=0).astype(BF16)
        it["dt_bd"] = jnp.where(same_head, _dot_tn(keys, vals), 0.0)
        fill()

    n_quads = W // RW_QUAD
    rows_out = []
    for c in range(T // C):
        ys = []
        for qd in range(n_quads):
            it = cut[c * n_quads + qd]
            st = st_ref[qd].astype(BF16)
            ys.append(it["y_intra"] + _dot(it["q_eff"], st))
            st_ref[qd] = _dot(it["pt_bd"], st) + it["dt_bd"]
        rows_out.append(jnp.concatenate(ys, axis=1))
        fill()
    y = jnp.concatenate(rows_out, axis=0)
    mean = head_sum(y) * (1.0 / RW_HEAD_DIM)
    yc = y - mean
    var = head_sum(yc * yc) * (1.0 / RW_HEAD_DIM)
    y = yc * lax.rsqrt(var + RW_GN_EPS) * gng_ref[...] + gnb_ref[...]
    o_ref[...] = ((y + bonus) * gate).astype(o_ref.dtype)


def _proj_mix_kernel(h_ref, wrw_ref, wsb_ref, whg_ref, wgt_ref,
                     mu_ref, w0_ref, wup_ref, a0_ref, aup_ref, gup_ref, kk_ref, ka_ref, rk_ref,
                     gng_ref, gnb_ref, lbraw_ref, ng_ref,
                     q_ref, k_ref, v_ref, gate_ref, yrw_ref, yhg_ref,
                     prev_ref, rwst_ref, hgst_ref, urw_ref, uhg_ref,
                     *, layer, tiles_per_seq, hg_tile):
    i = pl.program_id(0)

    @pl.when(i == 0)
    def _():
        urw_ref[...] = jnp.zeros_like(urw_ref)
        uhg_ref[...] = jnp.zeros_like(uhg_ref)

    @pl.when(jnp.maximum(i - 1, 0) % tiles_per_seq == 0)
    def _():
        prev_ref[...] = jnp.zeros_like(prev_ref)
        rwst_ref[...] = jnp.zeros_like(rwst_ref)
        hgst_ref[...] = jnp.zeros_like(hgst_ref)

    hb = h_ref[...].astype(BF16)

    def piece(w_ref, c, store):
        cols = slice(c * MXU_COLS, (c + 1) * MXU_COLS)
        store(cols, _dot(hb, w_ref[:, cols]))

    def store_rw(cols, y):
        urw_ref[:, cols] = y

    def store_hg(cols, y):
        uhg_ref[:, cols] = y

    def store_gate(cols, y):
        gate_ref[:, cols] = _sigmoid(y).astype(BF16)

    def store_sb(cols, y):
        o_ref = (q_ref, k_ref, v_ref)[cols.start // SB_WIDTH]
        local = slice(cols.start % SB_WIDTH, cols.start % SB_WIDTH + MXU_COLS)
        o_ref[:, local] = (y * (SB_HEAD_DIM ** -0.5) if o_ref is q_ref else y).astype(BF16)

    def pieces_of(w_ref, store):
        return [functools.partial(piece, w_ref, c, store) for c in range(w_ref.shape[1] // MXU_COLS)]

    def hgrn2_stages():
        tiles = []
        for j in range(hb.shape[0] // hg_tile):
            rows = slice(j * hg_tile, (j + 1) * hg_tile)
            tiles.append(_hgrn2_tile(uhg_ref, lbraw_ref, ng_ref, yhg_ref, rows, hgst_ref, layer))
            next(tiles[-1])
            yield
        for t in tiles:
            yield from t
            yield

    hgrn2 = hgrn2_stages()
    hgrn2_step = functools.partial(next, hgrn2, None)
    for _ in range(hb.shape[0] // hg_tile):
        hgrn2_step()
    queue = pieces_of(whg_ref, store_hg) + pieces_of(wrw_ref, store_rw)
    for p in pieces_of(wsb_ref, store_sb) + pieces_of(wgt_ref, store_gate):
        queue += [hgrn2_step, p]

    def fill(n=1):
        for _ in range(n):
            if queue:
                queue.pop(0)()

    _rwkv7_tile(urw_ref[...], mu_ref, w0_ref, wup_ref, a0_ref, aup_ref, gup_ref, kk_ref,
                ka_ref, rk_ref, gng_ref, gnb_ref, yrw_ref, prev_ref, rwst_ref, fill)
    fill(len(queue))
    for _ in hgrn2:
        pass


def _proj_mix(h, w_rw, w_sb, w_hg, w_gate, rw_params, lb_raw, norm_g, layer, seq, tm, hg_tile):
    n = h.shape[0]
    nt = n // tm

    def projected(c):
        return pl.BlockSpec((tm, c), lambda i: (jnp.minimum(i, nt - 1), 0))

    def mixed(c):
        return pl.BlockSpec((tm, c), lambda i: (jnp.maximum(i - 1, 0), 0))

    params = (w_rw, w_sb, w_hg, w_gate) + tuple(rw_params) + (lb_raw, norm_g)
    return pl.pallas_call(
        functools.partial(_proj_mix_kernel, layer=layer, tiles_per_seq=seq // tm, hg_tile=hg_tile),
        grid=(nt + 1,),
        in_specs=[projected(D_MODEL)] + [_resident(a.shape) for a in params],
        out_specs=[projected(SB_WIDTH), projected(SB_WIDTH), projected(SB_WIDTH),
                   projected(3 * D_MODEL), mixed(RW_WIDTH), mixed(HG_WIDTH)],
        out_shape=[jax.ShapeDtypeStruct((n, SB_WIDTH), BF16),
                   jax.ShapeDtypeStruct((n, SB_WIDTH), BF16),
                   jax.ShapeDtypeStruct((n, SB_WIDTH), BF16),
                   jax.ShapeDtypeStruct((n, 3 * D_MODEL), BF16),
                   jax.ShapeDtypeStruct((n, RW_WIDTH), BF16),
                   jax.ShapeDtypeStruct((n, HG_WIDTH), BF16)],
        scratch_shapes=[pltpu.VMEM((1, RW_U_COLS), F32),
                        pltpu.VMEM((RW_WIDTH // RW_QUAD, RW_QUAD, RW_QUAD), F32),
                        pltpu.VMEM((HG_HEADS, HG_DIM, HG_DIM), F32),
                        pltpu.VMEM((tm, RW_U_COLS), F32),
                        pltpu.VMEM((tm, 4 * HG_WIDTH), F32)],
        compiler_params=pltpu.CompilerParams(dimension_semantics=("arbitrary",),
                                             vmem_limit_bytes=VMEM_LIMIT),
        name="proj_mix",
    )(h, *params)


def _pad_axis(w, axis, size):
    pads = [(0, 0)] * w.ndim
    pads[axis] = (0, size - w.shape[axis])
    return jnp.pad(w, pads)


def _split_rw_cols(w):
    o = 3 * RW_WIDTH
    parts = [w[..., :o]]
    for size, pad in zip((RW_DECAY_LORA, RW_AAA_LORA, RW_GATE_LORA), RW_LORA_PAD):
        parts.append(_pad_axis(w[..., o:o + size], -1, pad))
        o += size
    return jnp.concatenate(parts, axis=-1)


def kernel(x, p, ln_g, ln_b, ffn1_wg, ffn1_wu, ffn1_wd, w_in, rw_mu, rw_w0, rw_w_up, rw_a0, rw_a_up, rw_g_up, rw_k_k, rw_k_a, rw_r_k, rw_gn_g, rw_gn_b, hg_lb_raw, hg_norm_g, w_br_rw, w_br_sb, w_br_hg, w_out, ffn2_wg, ffn2_wu, ffn2_wd, ple_gate, ple_proj):
    bsz, seq, _ = x.shape
    n = bsz * seq
    tm = min(256, seq)
    sb_blk = min(256, seq)
    hg_tile = min(128, seq)
    rw_cols = 3 * RW_WIDTH + RW_DECAY_LORA + RW_AAA_LORA + RW_GATE_LORA
    sb_end = rw_cols + 3 * SB_WIDTH
    hg_end = sb_end + 4 * HG_WIDTH

    ffn1 = [w.astype(BF16) for w in (ffn1_wg, ffn1_wu, ffn1_wd)]
    proj = [_split_rw_cols(w_in[..., :rw_cols]).astype(BF16), w_in[..., rw_cols:sb_end].astype(BF16),
            w_in[..., sb_end:hg_end].astype(BF16), w_in[..., hg_end:].astype(BF16)]
    loras = [_pad_axis(w, 1, pad).astype(BF16)
             for w, pad in zip((rw_w_up, rw_a_up, rw_g_up), RW_LORA_PAD)]
    rw_mu_cols = _split_rw_cols(rw_mu)
    tail = [w.astype(BF16) for w in (w_br_rw, w_br_sb, w_br_hg, w_out, ffn2_wg, ffn2_wu, ffn2_wd,
                                     ple_gate, ple_proj)]

    h = x.reshape(n, D_MODEL)
    for i in range(DEPTH):
        def vec(a):
            return a[i].reshape(1, -1)

        h = _ffn_ln(h, *(w[i] for w in ffn1), vec(ln_g[:, 0]), vec(ln_b[:, 0]), min(512, n))
        rw_params = (vec(rw_mu_cols), vec(rw_w0), loras[0][i], vec(rw_a0), loras[1][i], loras[2][i],
                     vec(rw_k_k), vec(rw_k_a), vec(rw_r_k), vec(rw_gn_g), vec(rw_gn_b))
        q, k, v, gate, y_rw, y_hg = _proj_mix(h, *(w[i] for w in proj), rw_params, hg_lb_raw,
                                              vec(hg_norm_g), i, seq, tm, hg_tile)
        y_sb = _sb_attention(q, k, v, bsz, seq, sb_blk)
        h = _layer_tail(h, y_rw, y_sb, y_hg, gate, p[i].reshape(n, PLE_DIM), [w[i] for w in tail],
                        ln_g[i, 1:], ln_b[i, 1:], min(512, n))
    return h.reshape(bsz, seq, D_MODEL)
```

```python
import functools

import jax
import jax.numpy as jnp
from jax import lax
from jax.experimental import pallas as pl
from jax.experimental.pallas import tpu as pltpu

F32 = jnp.float32
BF16 = jnp.bfloat16

D_MODEL = 1024
DEPTH = 4
PLE_DIM = 256
D_FF = 2816
RW_HEADS = 8
RW_HEAD_DIM = 64
RW_WIDTH = RW_HEADS * RW_HEAD_DIM
RW_CHUNK = 64
RW_DECAY_LORA = 64
RW_AAA_LORA = 64
RW_GATE_LORA = 160
RW_GN_EPS = 64e-5
RW_QUAD = 256
SB_HEADS = 4
SB_HEAD_DIM = 128
SB_WIDTH = SB_HEADS * SB_HEAD_DIM
SB_DEAD_LOG = -105.0
HG_HEADS = 4
HG_DIM = 128
HG_WIDTH = HG_HEADS * HG_DIM
HG_EPS = 1e-5
LOG2_E = 1.4426950408889634
LN_EPS = 1e-5
ALPHA = (2 * DEPTH) ** 0.25

MXU_COLS = 256
FFN_COL_GROUP = 6 * MXU_COLS
RW_LORA_PAD = (128, 128, 256)
RW_U_COLS = 3 * RW_WIDTH + sum(RW_LORA_PAD)
VMEM_LIMIT = 56 * 1024 * 1024


def _resident(shape):
    return pl.BlockSpec(shape, lambda *_: (0,) * len(shape), pipeline_mode=pl.Buffered(1))


def _dot(a, b):
    return jnp.dot(a, b, preferred_element_type=F32)


def _dot_nt(a, b):
    return lax.dot_general(a, b, (((1,), (1,)), ((), ())), preferred_element_type=F32)


def _dot_tn(a, b):
    return lax.dot_general(a, b, (((0,), (0,)), ((), ())), preferred_element_type=F32)


def _split2(x):
    hi = x.astype(BF16)
    lo = (x - hi.astype(F32)).astype(BF16)
    return hi, lo


def _split3(x):
    hi = x.astype(BF16)
    r = x - hi.astype(F32)
    mid = r.astype(BF16)
    lo = (r - mid.astype(F32)).astype(BF16)
    return hi, mid, lo


def _sigmoid(x):
    return 0.5 * jnp.tanh(0.5 * x) + 0.5


def _softplus(x):
    return jnp.maximum(x, 0.0) + jnp.log(1.0 + jnp.exp(-jnp.abs(x)))


def _layer_norm(x, g, b):
    mu = jnp.mean(x, axis=-1, keepdims=True)
    xc = x - mu
    var = jnp.mean(xc * xc, axis=-1, keepdims=True)
    return xc * lax.rsqrt(var + LN_EPS) * g + b


def _swiglu(hb, wg_ref, wu_ref, wd_ref, out):
    y = None
    for c0 in range(0, D_FF, FFN_COL_GROUP):
        cols = slice(c0, min(c0 + FFN_COL_GROUP, D_FF))
        g = _dot(hb, wg_ref[:, cols])
        u = _dot(hb, wu_ref[:, cols])
        yield
        part = _dot((g * _sigmoid(g) * u).astype(BF16), wd_ref[cols, :])
        y = part if y is None else y + part
        yield
    out.append(y)


def _round_robin(generators):
    live = list(generators)
    while live:
        for gen in list(live):
            if next(gen, StopIteration) is StopIteration:
                live.remove(gen)


def _row_groups(rows):
    half = rows // 2 if rows % 16 == 0 else rows
    return [slice(r, r + half) for r in range(0, rows, half)]


def _ffn_ln_kernel(h_ref, wg_ref, wu_ref, wd_ref, lng_ref, lnb_ref, o_ref):
    def rows_stage(rows):
        h = h_ref[rows]
        y = []
        yield from _swiglu(h.astype(BF16), wg_ref, wu_ref, wd_ref, y)
        o_ref[rows] = _layer_norm(ALPHA * h + 0.5 * y[0], lng_ref[...], lnb_ref[...])

    _round_robin(rows_stage(rows) for rows in _row_groups(h_ref.shape[0]))


def _ffn_ln(h, wg, wu, wd, ln_g, ln_b, tm):
    n = h.shape[0]
    row = pl.BlockSpec((tm, D_MODEL), lambda i: (i, 0))
    return pl.pallas_call(
        _ffn_ln_kernel,
        grid=(n // tm,),
        in_specs=[row, _resident((D_MODEL, D_FF)), _resident((D_MODEL, D_FF)),
                  _resident((D_FF, D_MODEL)), _resident((1, D_MODEL)), _resident((1, D_MODEL))],
        out_specs=row,
        out_shape=jax.ShapeDtypeStruct((n, D_MODEL), F32),
        compiler_params=pltpu.CompilerParams(dimension_semantics=("arbitrary",),
                                             vmem_limit_bytes=VMEM_LIMIT),
        name="ffn_ln",
    )(h, wg, wu, wd, ln_g, ln_b)


def _layer_tail_kernel(h_ref, yrw_ref, ysb_ref, yhg_ref, gate_ref, p_ref,
                       wrw_ref, wsb_ref, whg_ref, wout_ref, wg_ref, wu_ref, wd_ref, pg_ref, pp_ref,
                       lng_ref, lnb_ref, o_ref):
    def rows_stage(rows):
        branches = [_dot(y_ref[rows], w_ref[...]) for y_ref, w_ref in
                    ((yrw_ref, wrw_ref), (ysb_ref, wsb_ref), (yhg_ref, whg_ref))]
        yield
        gate = gate_ref[rows]
        merged = sum(gate[:, j * D_MODEL:(j + 1) * D_MODEL] * b for j, b in enumerate(branches))
        mix = _dot(merged.astype(BF16), wout_ref[...])
        yield
        h = _layer_norm(ALPHA * h_ref[rows] + mix, lng_ref[0:1], lnb_ref[0:1])
        y = []
        yield from _swiglu(h.astype(BF16), wg_ref, wu_ref, wd_ref, y)
        h = _layer_norm(ALPHA * h + 0.5 * y[0], lng_ref[1:2], lnb_ref[1:2])
        ple_gate = _dot(h.astype(BF16), pg_ref[...])
        ple_in = _dot(p_ref[rows].astype(BF16), pp_ref[...])
        yield
        o_ref[rows] = _layer_norm(ALPHA * h + _sigmoid(ple_gate) * ple_in, lng_ref[2:3], lnb_ref[2:3])

    _round_robin(rows_stage(rows) for rows in _row_groups(h_ref.shape[0]))


def _layer_tail(h, y_rw, y_sb, y_hg, gate, p, weights, ln_g, ln_b, tm):
    n = h.shape[0]

    def row(c):
        return pl.BlockSpec((tm, c), lambda i: (i, 0))

    return pl.pallas_call(
        _layer_tail_kernel,
        grid=(n // tm,),
        in_specs=[row(D_MODEL), row(RW_WIDTH), row(SB_WIDTH), row(HG_WIDTH), row(3 * D_MODEL),
                  row(PLE_DIM)] + [_resident(w.shape) for w in weights]
                 + [_resident(ln_g.shape), _resident(ln_b.shape)],
        out_specs=row(D_MODEL),
        out_shape=jax.ShapeDtypeStruct((n, D_MODEL), F32),
        compiler_params=pltpu.CompilerParams(dimension_semantics=("arbitrary",),
                                             vmem_limit_bytes=VMEM_LIMIT),
        name="layer_tail",
    )(h, y_rw, y_sb, y_hg, gate, p, *weights, ln_g, ln_b)


def _sb_kernel(q_ref, k_ref, v_ref, o_ref, acc_ref, carry_ref, *, blk):
    i = pl.program_id(1)
    row = lax.broadcasted_iota(jnp.int32, (blk, blk), 0)
    col = lax.broadcasted_iota(jnp.int32, (blk, blk), 1)
    suffix = jnp.where(row >= col, 1.0, 0.0).astype(BF16)
    strict = col < row

    lanes = [slice(h * SB_HEAD_DIM, (h + 1) * SB_HEAD_DIM) for h in range(SB_HEADS)]

    def neg_softplus(z):
        return -(jnp.maximum(z, 0.0) + jnp.log(1.0 + jnp.exp(-jnp.abs(z))))

    def key_block(start):
        zs = [_dot_nt(q_ref[:, sl], k_ref[pl.ds(start, blk), sl]) for sl in lanes]
        withins = [_dot(neg_softplus(z).astype(BF16), suffix) for z in zs]
        atts = [jnp.exp(z + w + carry_ref[h]) for h, (z, w) in enumerate(zip(zs, withins))]
        for h, sl in enumerate(lanes):
            acc_ref[:, sl] += _dot(atts[h].astype(BF16), v_ref[pl.ds(start, blk), sl])
            carry_ref[h] += withins[h][:, 0:1]

    has_prev = i > 0
    d_start = pl.multiple_of(i * blk, blk)
    p_start = pl.multiple_of(jnp.maximum(i - 1, 0) * blk, blk)
    z_d = [_dot_nt(q_ref[:, sl], k_ref[pl.ds(d_start, blk), sl]) for sl in lanes]
    z_p = [_dot_nt(q_ref[:, sl], k_ref[pl.ds(p_start, blk), sl]) for sl in lanes]
    m_d = [jnp.where(strict, neg_softplus(z), 0.0) for z in z_d]
    m_p = [jnp.where(has_prev, neg_softplus(z), 0.0) for z in z_p]
    w_d = [_dot(m.astype(BF16), suffix) for m in m_d]
    w_p = [_dot(m.astype(BF16), suffix) for m in m_p]
    for h, sl in enumerate(lanes):
        att_d = jnp.where(strict, jnp.exp(z_d[h] + w_d[h]), 0.0)
        att_p = jnp.where(has_prev, jnp.exp(z_p[h] + w_p[h] + w_d[h][:, 0:1]), 0.0)
        acc_ref[:, sl] = (_dot(att_d.astype(BF16), v_ref[pl.ds(d_start, blk), sl])
                          + _dot(att_p.astype(BF16), v_ref[pl.ds(p_start, blk), sl]))
        carry_ref[h] = w_d[h][:, 0:1] + w_p[h][:, 0:1]

    def more(t):
        return jnp.logical_and(t < i, jnp.max(carry_ref[...]) > SB_DEAD_LOG)

    def body(t):
        key_block(pl.multiple_of((i - 1 - t) * blk, blk))
        return t + 1

    lax.while_loop(more, body, 1)
    o_ref[...] = acc_ref[...].astype(o_ref.dtype)


def _sb_attention(q, k, v, bsz, seq, blk):
    n = bsz * seq
    nq = seq // blk
    qspec = pl.BlockSpec((blk, SB_WIDTH), lambda b, i: (b * nq + i, 0))
    kvspec = pl.BlockSpec((seq, SB_WIDTH), lambda b, i: (b, 0))
    return pl.pallas_call(
        functools.partial(_sb_kernel, blk=blk),
        grid=(bsz, nq),
        in_specs=[qspec, kvspec, kvspec],
        out_specs=qspec,
        out_shape=jax.ShapeDtypeStruct((n, SB_WIDTH), BF16),
        scratch_shapes=[pltpu.VMEM((blk, SB_WIDTH), F32), pltpu.VMEM((SB_HEADS, blk, 1), F32)],
        compiler_params=pltpu.CompilerParams(dimension_semantics=("arbitrary", "arbitrary"),
                                             vmem_limit_bytes=VMEM_LIMIT),
        name="stick_breaking",
    )(q, k, v)


def _hgrn2_tile(u_ref, lbraw_ref, ng_ref, o_ref, rows, st_ref, layer):
    u = u_ref[rows]
    tile = u.shape[0]
    raw = lbraw_ref[...]
    e = jnp.exp(raw - jnp.max(raw, axis=0, keepdims=True))
    sm = e / jnp.sum(e, axis=0, keepdims=True)
    lb = jnp.zeros((1, HG_WIDTH), F32)
    for l in range(1, layer + 1):
        lb = lb + sm[l:l + 1]

    q = _sigmoid(u[:, :HG_WIDTH])
    f_raw = u[:, HG_WIDTH:2 * HG_WIDTH]
    val = u[:, 2 * HG_WIDTH:3 * HG_WIDTH]
    out_gate = u[:, 3 * HG_WIDTH:]

    half_tanh = 0.5 * jnp.tanh(0.5 * f_raw)
    sig_pos = 0.5 + half_tanh
    sig_neg = 0.5 - half_tanh
    log_sig = jnp.minimum(f_raw, 0.0) - jnp.log(1.0 + jnp.exp(-jnp.abs(f_raw)))
    if layer == 0:
        log_f = log_sig
    else:
        log_f = jnp.where(lb > 0.0, jnp.log(lb + (1.0 - lb) * sig_pos), log_sig)
    kg = (1.0 - lb) * sig_neg
    yield

    pos = lax.broadcasted_iota(jnp.int32, (tile, HG_WIDTH), 0)
    ti = lax.broadcasted_iota(jnp.int32, (tile, tile), 0)
    si = lax.broadcasted_iota(jnp.int32, (tile, tile), 1)
    prefix = jnp.where(si <= ti, 1.0, 0.0).astype(BF16)
    g = sum(_dot(prefix, piece) for piece in _split3(log_f))

    heads = [slice(h * HG_DIM, (h + 1) * HG_DIM) for h in range(HG_HEADS)]
    qb = q.astype(BF16)
    kgb = kg.astype(BF16)
    att = [jnp.where(ti == si, _dot_nt(qb[:, sl], kgb[:, sl]), 0.0) for sl in heads]
    last = g
    w = 1
    while w < tile:
        upper = (pos & (2 * w - 1)) >= w
        g_ref = jnp.where(upper, pltpu.roll(last, w, 0), last)
        last = jnp.where(upper, last, pltpu.roll(last, tile - w, 0))
        decay = jnp.exp2(jnp.abs(g - g_ref) * -LOG2_E)
        z = (jnp.where(upper, q, kg) * decay).astype(BF16)
        pair = ((ti & -(2 * w)) == (si & -(2 * w))) & ((ti & w) != 0) & ((si & w) == 0)
        att = [a + jnp.where(pair, _dot_nt(z[:, sl], z[:, sl]), 0.0) for a, sl in zip(att, heads)]
        yield
        w *= 2

    valb = val.astype(BF16)
    qg = (q * jnp.exp(g)).astype(BF16)
    g_last = g[tile - 1:tile]
    kd = (kg * jnp.exp(g_last - g)).astype(BF16)
    gam = jnp.exp(g_last)
    ng = ng_ref[...]
    yield
    for h, sl in enumerate(heads):
        st = st_ref[h]
        o = _dot(att[h].astype(BF16), valb[:, sl]) + _dot_nt(qg[:, sl], st.astype(BF16))
        o = o * lax.rsqrt(jnp.mean(o * o, axis=-1, keepdims=True) + HG_EPS) * ng[:, sl]
        gate = out_gate[:, sl]
        o_ref[rows, sl] = (o * (gate * _sigmoid(gate))).astype(o_ref.dtype)
        st_ref[h] = st * gam[:, sl] + _dot_tn(valb[:, sl], kd[:, sl])


def _quad_mask(rows):
    r = lax.broadcasted_iota(jnp.int32, (rows, RW_QUAD), 0)
    c = lax.broadcasted_iota(jnp.int32, (rows, RW_QUAD), 1)
    return r, c


def _block_diag(y, head_ones):
    return jnp.concatenate([y, y, y, y], axis=0) * head_ones


def _rwkv7_tile(u, mu_ref, w0_ref, wup_ref, a0_ref, aup_ref, gup_ref, kk_ref, ka_ref,
                rk_ref, gng_ref, gnb_ref, o_ref, prev_ref, st_ref, fill):
    C = RW_CHUNK
    W = RW_WIDTH
    T = u.shape[0]
    trow = lax.broadcasted_iota(jnp.int32, u.shape, 0)
    shifted = jnp.where(trow == 0, prev_ref[...], pltpu.roll(u, 1, 0))
    prev_ref[...] = u[T - 1:T]
    u = u + (shifted - u) * mu_ref[...]
    fill(4)

    r = u[:, :W]
    k = u[:, W:2 * W]
    v = u[:, 2 * W:3 * W]
    o0 = 3 * W
    xw = u[:, o0:o0 + RW_LORA_PAD[0]]
    xa = u[:, o0 + RW_LORA_PAD[0]:o0 + RW_LORA_PAD[0] + RW_LORA_PAD[1]]
    xg = u[:, o0 + RW_LORA_PAD[0] + RW_LORA_PAD[1]:]

    w_log = -_softplus(-(w0_ref[...] + _dot(jnp.tanh(xw).astype(BF16), wup_ref[...]))) - 0.5
    log_w = -jnp.exp(w_log)
    a = _sigmoid(a0_ref[...] + _dot(xa.astype(BF16), aup_ref[...]))
    gate = _dot(_sigmoid(xg).astype(BF16), gup_ref[...])

    br, bc = _quad_mask(RW_QUAD)
    same_head = (br // RW_HEAD_DIM) == (bc // RW_HEAD_DIM)
    head_ones = jnp.where(same_head, 1.0, 0.0).astype(BF16)

    def head_sum(x):
        pieces = [p[:, sl] for p in _split2(x) for sl in (slice(0, RW_QUAD), slice(RW_QUAD, W))]
        s = _dot(jnp.concatenate(pieces, axis=0), head_ones)
        halves = [s[j * T:(j + 1) * T] + s[(j + 2) * T:(j + 3) * T] for j in range(2)]
        return jnp.concatenate(halves, axis=1)

    kk = k * kk_ref[...]
    kk = kk * jnp.minimum(lax.rsqrt(head_sum(kk * kk)), 1e12)
    k = k * (1.0 + (a - 1.0) * ka_ref[...])
    bonus = head_sum(r * k * rk_ref[...]) * v
    a_vec = -kk
    b_vec = kk * a
    fill(4)

    g = log_w
    wrow = lax.broadcasted_iota(jnp.int32, (T, W), 0) % C
    d = 1
    while d < C:
        g = g + jnp.where(wrow >= d, pltpu.roll(g, d, 0), 0.0)
        d *= 2
    fill(4)
    g_last = jnp.concatenate(
        [jnp.broadcast_to(g[c * C + C - 1:c * C + C], (C, W)) for c in range(T // C)], axis=0)
    inv = jnp.exp(-g)
    a_dec = a_vec * jnp.exp(g - log_w)
    r_dec = r * jnp.exp(g)
    b_inv = b_vec * inv
    k_inv = k * inv
    dec = jnp.exp(g_last - g)
    b_dec = b_vec * dec
    k_dec = k * dec
    e_last = jnp.exp(g_last)
    fill(4)

    eye_bd = br == bc
    tr, tc = _quad_mask(C)
    s_idx = tc % RW_HEAD_DIM
    strict = s_idx < tr
    incl = s_idx <= tr
    eye_q = jnp.where(s_idx == tr, 1.0, 0.0)

    def bd(y):
        return _block_diag(y.astype(BF16), head_ones)


    items = [(slice(c * C, (c + 1) * C), slice(qd * RW_QUAD, (qd + 1) * RW_QUAD))
             for c in range(T // C) for qd in range(W // RW_QUAD)]
    cut = [dict(a_dec=a_dec[rs, ls], r_dec=r_dec[rs, ls], v=v[rs, ls]) for rs, ls in items]

    for it, (rs, ls) in zip(cut, items):
        ar = jnp.concatenate([it["a_dec"], it["r_dec"]], axis=0).astype(BF16)
        lm_b = _dot_nt(ar, bd(b_inv[rs, ls]))
        lm_k = _dot_nt(ar, bd(k_inv[rs, ls]))
        it["l_ab"] = jnp.where(strict, lm_b[:C], 0.0)
        it["m_rb"] = jnp.where(incl, lm_b[C:], 0.0).astype(BF16)
        it["m_rk"] = jnp.where(incl, lm_k[C:], 0.0).astype(BF16)
        it["v_bd"] = bd(it["v"])
        it["lv"] = _dot(jnp.where(strict, lm_k[:C], 0.0).astype(BF16), it["v_bd"])
        it["inv_l"] = eye_q + it["l_ab"]
        it["lp"] = it["l_ab"]
        fill()

    n_levels = 5
    for level in range(n_levels):
        for it in cut:
            lp_bd = bd(it["lp"])
            if level == 0:
                it["lp"] = _dot(it["lp"].astype(BF16), lp_bd)
            else:
                both = jnp.concatenate([it["inv_l"], it["lp"]], axis=0).astype(BF16)
                both = _dot(both, lp_bd)
                it["inv_l"] = it["inv_l"] + both[:C]
                it["lp"] = both[C:]
        fill(3)
    for it in cut:
        it["inv_l"] = it["inv_l"] + _dot(it["inv_l"].astype(BF16), bd(it["lp"]))
    fill(3)

    for it, (rs, ls) in zip(cut, items):
        inv_b = it["inv_l"].astype(BF16)
        w1 = _dot(inv_b, bd(it["a_dec"]))
        u0 = _dot(inv_b, bd(it["lv"]))
        it["q_eff"] = (it["r_dec"] + _dot(it["m_rb"], bd(w1))).astype(BF16)
        it["y_intra"] = _dot(it["m_rb"], bd(u0)) + _dot(it["m_rk"], it["v_bd"])
        b_dec_b = b_dec[rs, ls].astype(BF16)
        it["pt_bd"] = (jnp.where(same_head, _dot_tn(b_dec_b, w1.astype(BF16)), 0.0)
                       + jnp.where(eye_bd, e_last[rs, ls][:1], 0.0)).astype(BF16)
        keys = jnp.concatenate([b_dec_b, k_dec[rs, ls].astype(BF16)], axis=0)
        vals = jnp.concatenate([u0, it["v"]], axis=0).astype(BF16)
        it["dt_bd"] = jnp.where(same_head, _dot_tn(keys, vals), 0.0)
        fill()

    n_quads = W // RW_QUAD
    rows_out = []
    for c in range(T // C):
        ys = []
        for qd in range(n_quads):
            it = cut[c * n_quads + qd]
            st = st_ref[qd].astype(BF16)
            ys.append(it["y_intra"] + _dot(it["q_eff"], st))
            st_ref[qd] = _dot(it["pt_bd"], st) + it["dt_bd"]
        rows_out.append(jnp.concatenate(ys, axis=1))
        fill()
    y = jnp.concatenate(rows_out, axis=0)
    mean = head_sum(y) * (1.0 / RW_HEAD_DIM)
    yc = y - mean
    var = head_sum(yc * yc) * (1.0 / RW_HEAD_DIM)
    y = yc * lax.rsqrt(var + RW_GN_EPS) * gng_ref[...] + gnb_ref[...]
    o_ref[...] = ((y + bonus) * gate).astype(o_ref.dtype)


def _proj_mix_kernel(h_ref, wrw_ref, wsb_ref, whg_ref, wgt_ref,
                     mu_ref, w0_ref, wup_ref, a0_ref, aup_ref, gup_ref, kk_ref, ka_ref, rk_ref,
                     gng_ref, gnb_ref, lbraw_ref, ng_ref,
                     q_ref, k_ref, v_ref, gate_ref, yrw_ref, yhg_ref,
                     prev_ref, rwst_ref, hgst_ref, urw_ref, uhg_ref,
                     *, layer, tiles_per_seq, hg_tile):
    i = pl.program_id(0)

    @pl.when(i == 0)
    def _():
        urw_ref[...] = jnp.zeros_like(urw_ref)
        uhg_ref[...] = jnp.zeros_like(uhg_ref)

    @pl.when(jnp.maximum(i - 1, 0) % tiles_per_seq == 0)
    def _():
        prev_ref[...] = jnp.zeros_like(prev_ref)
        rwst_ref[...] = jnp.zeros_like(rwst_ref)
        hgst_ref[...] = jnp.zeros_like(hgst_ref)

    hb = h_ref[...].astype(BF16)

    def piece(w_ref, c, store):
        cols = slice(c * MXU_COLS, (c + 1) * MXU_COLS)
        store(cols, _dot(hb, w_ref[:, cols]))

    def store_rw(cols, y):
        urw_ref[:, cols] = y

    def store_hg(cols, y):
        uhg_ref[:, cols] = y

    def store_gate(cols, y):
        gate_ref[:, cols] = _sigmoid(y).astype(BF16)

    def store_sb(cols, y):
        o_ref = (q_ref, k_ref, v_ref)[cols.start // SB_WIDTH]
        local = slice(cols.start % SB_WIDTH, cols.start % SB_WIDTH + MXU_COLS)
        o_ref[:, local] = (y * (SB_HEAD_DIM ** -0.5) if o_ref is q_ref else y).astype(BF16)

    def pieces_of(w_ref, store):
        return [functools.partial(piece, w_ref, c, store) for c in range(w_ref.shape[1] // MXU_COLS)]

    def hgrn2_stages():
        tiles = []
        for j in range(hb.shape[0] // hg_tile):
            rows = slice(j * hg_tile, (j + 1) * hg_tile)
            tiles.append(_hgrn2_tile(uhg_ref, lbraw_ref, ng_ref, yhg_ref, rows, hgst_ref, layer))
            next(tiles[-1])
            yield
        for t in tiles:
            yield from t
            yield

    hgrn2 = hgrn2_stages()
    hgrn2_step = functools.partial(next, hgrn2, None)
    for _ in range(hb.shape[0] // hg_tile):
        hgrn2_step()
    queue = pieces_of(whg_ref, store_hg) + pieces_of(wrw_ref, store_rw)
    for p in pieces_of(wsb_ref, store_sb) + pieces_of(wgt_ref, store_gate):
        queue += [hgrn2_step, p]

    def fill(n=1):
        for _ in range(n):
            if queue:
                queue.pop(0)()

    _rwkv7_tile(urw_ref[...], mu_ref, w0_ref, wup_ref, a0_ref, aup_ref, gup_ref, kk_ref,
                ka_ref, rk_ref, gng_ref, gnb_ref, yrw_ref, prev_ref, rwst_ref, fill)
    fill(len(queue))
    for _ in hgrn2:
        pass


def _proj_mix(h, w_rw, w_sb, w_hg, w_gate, rw_params, lb_raw, norm_g, layer, seq, tm, hg_tile):
    n = h.shape[0]
    nt = n // tm

    def projected(c):
        return pl.BlockSpec((tm, c), lambda i: (jnp.minimum(i, nt - 1), 0))

    def mixed(c):
        return pl.BlockSpec((tm, c), lambda i: (jnp.maximum(i - 1, 0), 0))

    params = (w_rw, w_sb, w_hg, w_gate) + tuple(rw_params) + (lb_raw, norm_g)
    return pl.pallas_call(
        functools.partial(_proj_mix_kernel, layer=layer, tiles_per_seq=seq // tm, hg_tile=hg_tile),
        grid=(nt + 1,),
        in_specs=[projected(D_MODEL)] + [_resident(a.shape) for a in params],
        out_specs=[projected(SB_WIDTH), projected(SB_WIDTH), projected(SB_WIDTH),
                   projected(3 * D_MODEL), mixed(RW_WIDTH), mixed(HG_WIDTH)],
        out_shape=[jax.ShapeDtypeStruct((n, SB_WIDTH), BF16),
                   jax.ShapeDtypeStruct((n, SB_WIDTH), BF16),
                   jax.ShapeDtypeStruct((n, SB_WIDTH), BF16),
                   jax.ShapeDtypeStruct((n, 3 * D_MODEL), BF16),
                   jax.ShapeDtypeStruct((n, RW_WIDTH), BF16),
                   jax.ShapeDtypeStruct((n, HG_WIDTH), BF16)],
        scratch_shapes=[pltpu.VMEM((1, RW_U_COLS), F32),
                        pltpu.VMEM((RW_WIDTH // RW_QUAD, RW_QUAD, RW_QUAD), F32),
                        pltpu.VMEM((HG_HEADS, HG_DIM, HG_DIM), F32),
                        pltpu.VMEM((tm, RW_U_COLS), F32),
                        pltpu.VMEM((tm, 4 * HG_WIDTH), F32)],
        compiler_params=pltpu.CompilerParams(dimension_semantics=("arbitrary",),
                                             vmem_limit_bytes=VMEM_LIMIT),
        name="proj_mix",
    )(h, *params)


def _pad_axis(w, axis, size):
    pads = [(0, 0)] * w.ndim
    pads[axis] = (0, size - w.shape[axis])
    return jnp.pad(w, pads)


def _split_rw_cols(w):
    o = 3 * RW_WIDTH
    parts = [w[..., :o]]
    for size, pad in zip((RW_DECAY_LORA, RW_AAA_LORA, RW_GATE_LORA), RW_LORA_PAD):
        parts.append(_pad_axis(w[..., o:o + size], -1, pad))
        o += size
    return jnp.concatenate(parts, axis=-1)


def kernel(x, p, ln_g, ln_b, ffn1_wg, ffn1_wu, ffn1_wd, w_in, rw_mu, rw_w0, rw_w_up, rw_a0, rw_a_up, rw_g_up, rw_k_k, rw_k_a, rw_r_k, rw_gn_g, rw_gn_b, hg_lb_raw, hg_norm_g, w_br_rw, w_br_sb, w_br_hg, w_out, ffn2_wg, ffn2_wu, ffn2_wd, ple_gate, ple_proj):
    bsz, seq, _ = x.shape
    n = bsz * seq
    tm = min(256, seq)
    sb_blk = min(256, seq)
    hg_tile = min(128, seq)
    rw_cols = 3 * RW_WIDTH + RW_DECAY_LORA + RW_AAA_LORA + RW_GATE_LORA
    sb_end = rw_cols + 3 * SB_WIDTH
    hg_end = sb_end + 4 * HG_WIDTH

    ffn1 = [w.astype(BF16) for w in (ffn1_wg, ffn1_wu, ffn1_wd)]
    proj = [_split_rw_cols(w_in[..., :rw_cols]).astype(BF16), w_in[..., rw_cols:sb_end].astype(BF16),
            w_in[..., sb_end:hg_end].astype(BF16), w_in[..., hg_end:].astype(BF16)]
    loras = [_pad_axis(w, 1, pad).astype(BF16)
             for w, pad in zip((rw_w_up, rw_a_up, rw_g_up), RW_LORA_PAD)]
    rw_mu_cols = _split_rw_cols(rw_mu)
    tail = [w.astype(BF16) for w in (w_br_rw, w_br_sb, w_br_hg, w_out, ffn2_wg, ffn2_wu, ffn2_wd,
                                     ple_gate, ple_proj)]

    h = x.reshape(n, D_MODEL)
    for i in range(DEPTH):
        def vec(a):
            return a[i].reshape(1, -1)

        h = _ffn_ln(h, *(w[i] for w in ffn1), vec(ln_g[:, 0]), vec(ln_b[:, 0]), min(512, n))
        rw_params = (vec(rw_mu_cols), vec(rw_w0), loras[0][i], vec(rw_a0), loras[1][i], loras[2][i],
                     vec(rw_k_k), vec(rw_k_a), vec(rw_r_k), vec(rw_gn_g), vec(rw_gn_b))
        q, k, v, gate, y_rw, y_hg = _proj_mix(h, *(w[i] for w in proj), rw_params, hg_lb_raw,
                                              vec(hg_norm_g), i, seq, tm, hg_tile)
        y_sb = _sb_attention(q, k, v, bsz, seq, sb_blk)
        h = _layer_tail(h, y_rw, y_sb, y_hg, gate, p[i].reshape(n, PLE_DIM), [w[i] for w in tail],
                        ln_g[i, 1:], ln_b[i, 1:], min(512, n))
    return h.reshape(bsz, seq, D_MODEL)
```

```python
import functools

import jax
import jax.numpy as jnp
from jax import lax
from jax.experimental import pallas as pl
from jax.experimental.pallas import tpu as pltpu

F32 = jnp.float32
BF16 = jnp.bfloat16

D_MODEL = 1024
DEPTH = 4
PLE_DIM = 256
D_FF = 2816
RW_HEADS = 8
RW_HEAD_DIM = 64
RW_WIDTH = RW_HEADS * RW_HEAD_DIM
RW_CHUNK = 64
RW_DECAY_LORA = 64
RW_AAA_LORA = 64
RW_GATE_LORA = 160
RW_GN_EPS = 64e-5
SB_HEADS = 4
SB_HEAD_DIM = 128
SB_WIDTH = SB_HEADS * SB_HEAD_DIM
SB_DEAD_LOG = -105.0
HG_HEADS = 4
HG_DIM = 128
HG_WIDTH = HG_HEADS * HG_DIM
HG_EPS = 1e-5
LOG2_E = 1.4426950408889634
LN_EPS = 1e-5
ALPHA = (2 * DEPTH) ** 0.25

MXU_COLS = 256
RW_QUAD = MXU_COLS
FFN_COL_GROUP = 6 * MXU_COLS
RW_LORA_PAD = (128, 128, 256)
RW_U_COLS = 3 * RW_WIDTH + sum(RW_LORA_PAD)
VMEM_LIMIT = 56 * 1024 * 1024


def _resident(shape):
    return pl.BlockSpec(shape, lambda *_: (0,) * len(shape), pipeline_mode=pl.Buffered(1))


def _dot(a, b):
    return jnp.dot(a, b, preferred_element_type=F32)


def _dot_nt(a, b):
    return lax.dot_general(a, b, (((1,), (1,)), ((), ())), preferred_element_type=F32)


def _dot_tn(a, b):
    return lax.dot_general(a, b, (((0,), (0,)), ((), ())), preferred_element_type=F32)


def _split2(x):
    hi = x.astype(BF16)
    lo = (x - hi.astype(F32)).astype(BF16)
    return hi, lo


def _split3(x):
    hi = x.astype(BF16)
    r = x - hi.astype(F32)
    mid = r.astype(BF16)
    lo = (r - mid.astype(F32)).astype(BF16)
    return hi, mid, lo


def _sigmoid(x):
    return 0.5 * jnp.tanh(0.5 * x) + 0.5


def _softplus(x):
    return jnp.maximum(x, 0.0) + jnp.log(1.0 + jnp.exp(-jnp.abs(x)))


def _layer_norm(x, g, b):
    mu = jnp.mean(x, axis=-1, keepdims=True)
    xc = x - mu
    var = jnp.mean(xc * xc, axis=-1, keepdims=True)
    return xc * lax.rsqrt(var + LN_EPS) * g + b


def _swiglu(hb, wg_ref, wu_ref, wd_ref, out):
    y = None
    for c0 in range(0, D_FF, FFN_COL_GROUP):
        cols = slice(c0, min(c0 + FFN_COL_GROUP, D_FF))
        g = _dot(hb, wg_ref[:, cols])
        u = _dot(hb, wu_ref[:, cols])
        yield
        part = _dot((g * _sigmoid(g) * u).astype(BF16), wd_ref[cols, :])
        y = part if y is None else y + part
        yield
    out.append(y)


def _round_robin(generators):
    live = list(generators)
    while live:
        for gen in list(live):
            if next(gen, StopIteration) is StopIteration:
                live.remove(gen)


def _row_groups(rows):
    half = rows // 2 if rows % 16 == 0 else rows
    return [slice(r, r + half) for r in range(0, rows, half)]


def _ffn_ln_kernel(h_ref, wg_ref, wu_ref, wd_ref, lng_ref, lnb_ref, o_ref):
    def rows_stage(rows):
        h = h_ref[rows]
        y = []
        yield from _swiglu(h.astype(BF16), wg_ref, wu_ref, wd_ref, y)
        o_ref[rows] = _layer_norm(ALPHA * h + 0.5 * y[0], lng_ref[...], lnb_ref[...])

    _round_robin(rows_stage(rows) for rows in _row_groups(h_ref.shape[0]))


def _ffn_ln(h, wg, wu, wd, ln_g, ln_b, tm):
    n = h.shape[0]
    row = pl.BlockSpec((tm, D_MODEL), lambda i: (i, 0))
    return pl.pallas_call(
        _ffn_ln_kernel,
        grid=(n // tm,),
        in_specs=[row, _resident((D_MODEL, D_FF)), _resident((D_MODEL, D_FF)),
                  _resident((D_FF, D_MODEL)), _resident((1, D_MODEL)), _resident((1, D_MODEL))],
        out_specs=row,
        out_shape=jax.ShapeDtypeStruct((n, D_MODEL), F32),
        compiler_params=pltpu.CompilerParams(dimension_semantics=("arbitrary",),
                                             vmem_limit_bytes=VMEM_LIMIT),
        name="ffn_ln",
    )(h, wg, wu, wd, ln_g, ln_b)


def _layer_tail_kernel(h_ref, yrw_ref, ysb_ref, yhg_ref, gate_ref, p_ref,
                       wrw_ref, wsb_ref, whg_ref, wout_ref, wg_ref, wu_ref, wd_ref, pg_ref, pp_ref,
                       lng_ref, lnb_ref, o_ref):
    def rows_stage(rows):
        branches = [_dot(y_ref[rows], w_ref[...]) for y_ref, w_ref in
                    ((yrw_ref, wrw_ref), (ysb_ref, wsb_ref), (yhg_ref, whg_ref))]
        yield
        gate = gate_ref[rows]
        merged = sum(gate[:, j * D_MODEL:(j + 1) * D_MODEL] * b for j, b in enumerate(branches))
        mix = _dot(merged.astype(BF16), wout_ref[...])
        yield
        h = _layer_norm(ALPHA * h_ref[rows] + mix, lng_ref[0:1], lnb_ref[0:1])
        y = []
        yield from _swiglu(h.astype(BF16), wg_ref, wu_ref, wd_ref, y)
        h = _layer_norm(ALPHA * h + 0.5 * y[0], lng_ref[1:2], lnb_ref[1:2])
        ple_gate = _dot(h.astype(BF16), pg_ref[...])
        ple_in = _dot(p_ref[rows].astype(BF16), pp_ref[...])
        yield
        o_ref[rows] = _layer_norm(ALPHA * h + _sigmoid(ple_gate) * ple_in, lng_ref[2:3], lnb_ref[2:3])

    _round_robin(rows_stage(rows) for rows in _row_groups(h_ref.shape[0]))


def _layer_tail(h, y_rw, y_sb, y_hg, gate, p, weights, ln_g, ln_b, tm):
    n = h.shape[0]

    def row(c):
        return pl.BlockSpec((tm, c), lambda i: (i, 0))

    return pl.pallas_call(
        _layer_tail_kernel,
        grid=(n // tm,),
        in_specs=[row(D_MODEL), row(RW_WIDTH), row(SB_WIDTH), row(HG_WIDTH), row(3 * D_MODEL),
                  row(PLE_DIM)] + [_resident(w.shape) for w in weights]
                 + [_resident(ln_g.shape), _resident(ln_b.shape)],
        out_specs=row(D_MODEL),
        out_shape=jax.ShapeDtypeStruct((n, D_MODEL), F32),
        compiler_params=pltpu.CompilerParams(dimension_semantics=("arbitrary",),
                                             vmem_limit_bytes=VMEM_LIMIT),
        name="layer_tail",
    )(h, y_rw, y_sb, y_hg, gate, p, *weights, ln_g, ln_b)


def _sb_kernel(q_ref, k_ref, v_ref, o_ref, acc_ref, carry_ref, *, blk):
    i = pl.program_id(1)
    row = lax.broadcasted_iota(jnp.int32, (blk, blk), 0)
    col = lax.broadcasted_iota(jnp.int32, (blk, blk), 1)
    suffix = jnp.where(row >= col, 1.0, 0.0).astype(BF16)
    strict = col < row

    lanes = [slice(h * SB_HEAD_DIM, (h + 1) * SB_HEAD_DIM) for h in range(SB_HEADS)]

    def neg_softplus(z):
        return -(jnp.maximum(z, 0.0) + jnp.log(1.0 + jnp.exp(-jnp.abs(z))))

    def key_block(start):
        zs = [_dot_nt(q_ref[:, sl], k_ref[pl.ds(start, blk), sl]) for sl in lanes]
        withins = [_dot(neg_softplus(z).astype(BF16), suffix) for z in zs]
        atts = [jnp.exp(z + w + carry_ref[h]) for h, (z, w) in enumerate(zip(zs, withins))]
        for h, sl in enumerate(lanes):
            acc_ref[:, sl] += _dot(atts[h].astype(BF16), v_ref[pl.ds(start, blk), sl])
            carry_ref[h] += withins[h][:, 0:1]

    has_prev = i > 0
    d_start = pl.multiple_of(i * blk, blk)
    p_start = pl.multiple_of(jnp.maximum(i - 1, 0) * blk, blk)
    z_d = [_dot_nt(q_ref[:, sl], k_ref[pl.ds(d_start, blk), sl]) for sl in lanes]
    z_p = [_dot_nt(q_ref[:, sl], k_ref[pl.ds(p_start, blk), sl]) for sl in lanes]
    m_d = [jnp.where(strict, neg_softplus(z), 0.0) for z in z_d]
    m_p = [jnp.where(has_prev, neg_softplus(z), 0.0) for z in z_p]
    w_d = [_dot(m.astype(BF16), suffix) for m in m_d]
    w_p = [_dot(m.astype(BF16), suffix) for m in m_p]
    for h, sl in enumerate(lanes):
        att_d = jnp.where(strict, jnp.exp(z_d[h] + w_d[h]), 0.0)
        att_p = jnp.where(has_prev, jnp.exp(z_p[h] + w_p[h] + w_d[h][:, 0:1]), 0.0)
        acc_ref[:, sl] = (_dot(att_d.astype(BF16), v_ref[pl.ds(d_start, blk), sl])
                          + _dot(att_p.astype(BF16), v_ref[pl.ds(p_start, blk), sl]))
        carry_ref[h] = w_d[h][:, 0:1] + w_p[h][:, 0:1]

    def more(t):
        return jnp.logical_and(t < i, jnp.max(carry_ref[...]) > SB_DEAD_LOG)

    def body(t):
        key_block(pl.multiple_of((i - 1 - t) * blk, blk))
        return t + 1

    lax.while_loop(more, body, 1)
    o_ref[...] = acc_ref[...].astype(o_ref.dtype)


def _sb_attention(q, k, v, bsz, seq, blk):
    n = bsz * seq
    nq = seq // blk
    qspec = pl.BlockSpec((blk, SB_WIDTH), lambda b, i: (b * nq + i, 0))
    kvspec = pl.BlockSpec((seq, SB_WIDTH), lambda b, i: (b, 0))
    return pl.pallas_call(
        functools.partial(_sb_kernel, blk=blk),
        grid=(bsz, nq),
        in_specs=[qspec, kvspec, kvspec],
        out_specs=qspec,
        out_shape=jax.ShapeDtypeStruct((n, SB_WIDTH), BF16),
        scratch_shapes=[pltpu.VMEM((blk, SB_WIDTH), F32), pltpu.VMEM((SB_HEADS, blk, 1), F32)],
        compiler_params=pltpu.CompilerParams(dimension_semantics=("arbitrary", "arbitrary"),
                                             vmem_limit_bytes=VMEM_LIMIT),
        name="stick_breaking",
    )(q, k, v)


def _hgrn2_tile(u_ref, lbraw_ref, ng_ref, o_ref, rows, st_ref, layer):
    u = u_ref[rows]
    tile = u.shape[0]
    raw = lbraw_ref[...]
    e = jnp.exp(raw - jnp.max(raw, axis=0, keepdims=True))
    sm = e / jnp.sum(e, axis=0, keepdims=True)
    lb = jnp.zeros((1, HG_WIDTH), F32)
    for l in range(1, layer + 1):
        lb = lb + sm[l:l + 1]

    q = _sigmoid(u[:, :HG_WIDTH])
    f_raw = u[:, HG_WIDTH:2 * HG_WIDTH]
    val = u[:, 2 * HG_WIDTH:3 * HG_WIDTH]
    out_gate = u[:, 3 * HG_WIDTH:]

    half_tanh = 0.5 * jnp.tanh(0.5 * f_raw)
    sig_pos = 0.5 + half_tanh
    sig_neg = 0.5 - half_tanh
    log_sig = jnp.minimum(f_raw, 0.0) - jnp.log(1.0 + jnp.exp(-jnp.abs(f_raw)))
    if layer == 0:
        log_f = log_sig
    else:
        log_f = jnp.where(lb > 0.0, jnp.log(lb + (1.0 - lb) * sig_pos), log_sig)
    kg = (1.0 - lb) * sig_neg
    yield

    pos = lax.broadcasted_iota(jnp.int32, (tile, HG_WIDTH), 0)
    ti = lax.broadcasted_iota(jnp.int32, (tile, tile), 0)
    si = lax.broadcasted_iota(jnp.int32, (tile, tile), 1)
    prefix = jnp.where(si <= ti, 1.0, 0.0).astype(BF16)
    g = sum(_dot(prefix, piece) for piece in _split3(log_f))

    heads = [slice(h * HG_DIM, (h + 1) * HG_DIM) for h in range(HG_HEADS)]
    qb = q.astype(BF16)
    kgb = kg.astype(BF16)
    att = [jnp.where(ti == si, _dot_nt(qb[:, sl], kgb[:, sl]), 0.0) for sl in heads]
    last = g
    w = 1
    while w < tile:
        upper = (pos & (2 * w - 1)) >= w
        g_ref = jnp.where(upper, pltpu.roll(last, w, 0), last)
        last = jnp.where(upper, last, pltpu.roll(last, tile - w, 0))
        decay = jnp.exp2(jnp.abs(g - g_ref) * -LOG2_E)
        z = (jnp.where(upper, q, kg) * decay).astype(BF16)
        pair = ((ti & -(2 * w)) == (si & -(2 * w))) & ((ti & w) != 0) & ((si & w) == 0)
        att = [a + jnp.where(pair, _dot_nt(z[:, sl], z[:, sl]), 0.0) for a, sl in zip(att, heads)]
        yield
        w *= 2

    valb = val.astype(BF16)
    qg = (q * jnp.exp(g)).astype(BF16)
    g_last = g[tile - 1:tile]
    kd = (kg * jnp.exp(g_last - g)).astype(BF16)
    gam = jnp.exp(g_last)
    ng = ng_ref[...]
    yield
    for h, sl in enumerate(heads):
        st = st_ref[h]
        o = _dot(att[h].astype(BF16), valb[:, sl]) + _dot_nt(qg[:, sl], st.astype(BF16))
        o = o * lax.rsqrt(jnp.mean(o * o, axis=-1, keepdims=True) + HG_EPS) * ng[:, sl]
        gate = out_gate[:, sl]
        o_ref[rows, sl] = (o * (gate * _sigmoid(gate))).astype(o_ref.dtype)
        st_ref[h] = st * gam[:, sl] + _dot_tn(valb[:, sl], kd[:, sl])


def _quad_mask(rows):
    r = lax.broadcasted_iota(jnp.int32, (rows, RW_QUAD), 0)
    c = lax.broadcasted_iota(jnp.int32, (rows, RW_QUAD), 1)
    return r, c


def _block_diag(y, head_ones):
    return jnp.concatenate([y, y, y, y], axis=0) * head_ones


def _rwkv7_tile(u, mu_ref, w0_ref, wup_ref, a0_ref, aup_ref, gup_ref, kk_ref, ka_ref,
                rk_ref, gng_ref, gnb_ref, o_ref, prev_ref, st_ref, fill):
    C = RW_CHUNK
    W = RW_WIDTH
    T = u.shape[0]
    trow = lax.broadcasted_iota(jnp.int32, u.shape, 0)
    shifted = jnp.where(trow == 0, prev_ref[...], pltpu.roll(u, 1, 0))
    prev_ref[...] = u[T - 1:T]
    u = u + (shifted - u) * mu_ref[...]
    fill(4)

    br, bc = _quad_mask(RW_QUAD)
    same_head = (br // RW_HEAD_DIM) == (bc // RW_HEAD_DIM)
    head_ones = jnp.where(same_head, 1.0, 0.0).astype(BF16)
    eye_bd = br == bc
    tr, tc = _quad_mask(C)
    s_idx = tc % RW_HEAD_DIM
    strict = s_idx < tr
    incl = s_idx <= tr
    eye_q = jnp.where(s_idx == tr, 1.0, 0.0)
    n_quads = W // RW_QUAD

    def bd(y):
        return _block_diag(y.astype(BF16), head_ones)

    def head_sum(x):
        n = x.shape[0]
        pieces = [p[:, sl] for p in _split2(x) for sl in (slice(0, RW_QUAD), slice(RW_QUAD, W))]
        s = _dot(jnp.concatenate(pieces, axis=0), head_ones)
        halves = [s[j * n:(j + 1) * n] + s[(j + 2) * n:(j + 3) * n] for j in range(2)]
        return jnp.concatenate(halves, axis=1)

    def prepare(rows, out):
        ur = u[rows]
        n = ur.shape[0]
        r = ur[:, :W]
        k = ur[:, W:2 * W]
        v = ur[:, 2 * W:3 * W]
        o0 = 3 * W
        xw = ur[:, o0:o0 + RW_LORA_PAD[0]]
        xa = ur[:, o0 + RW_LORA_PAD[0]:o0 + RW_LORA_PAD[0] + RW_LORA_PAD[1]]
        xg = ur[:, o0 + RW_LORA_PAD[0] + RW_LORA_PAD[1]:]
        w_log = -_softplus(-(w0_ref[...] + _dot(jnp.tanh(xw).astype(BF16), wup_ref[...]))) - 0.5
        log_w = -jnp.exp(w_log)
        a = _sigmoid(a0_ref[...] + _dot(xa.astype(BF16), aup_ref[...]))
        gate = _dot(_sigmoid(xg).astype(BF16), gup_ref[...])
        yield
        kk = k * kk_ref[...]
        kk = kk * jnp.minimum(lax.rsqrt(head_sum(kk * kk)), 1e12)
        k = k * (1.0 + (a - 1.0) * ka_ref[...])
        bonus = head_sum(r * k * rk_ref[...]) * v
        a_vec = -kk
        b_vec = kk * a
        yield
        tt = lax.broadcasted_iota(jnp.int32, (n, n), 0)
        ts = lax.broadcasted_iota(jnp.int32, (n, n), 1)
        prefix = jnp.where((tt // C == ts // C) & (ts <= tt), 1.0, 0.0).astype(BF16)
        g = sum(_dot(prefix, piece) for piece in _split3(log_w))
        yield
        g_last = jnp.concatenate(
            [jnp.broadcast_to(g[c * C + C - 1:c * C + C], (C, W)) for c in range(n // C)], axis=0)
        inv = jnp.exp(-g)
        dec = jnp.exp(g_last - g)
        out.update(a_dec=(a_vec * jnp.exp(g - log_w)).astype(BF16), r_dec=r * jnp.exp(g),
                   b_inv=(b_vec * inv).astype(BF16), k_inv=(k * inv).astype(BF16),
                   b_dec=(b_vec * dec).astype(BF16), k_dec=(k * dec).astype(BF16),
                   v=v.astype(BF16), e_last=jnp.exp(g_last), bonus=bonus, gate=gate)

    def process(rows, p, fill):
        n = p["gate"].shape[0]
        items = [(slice(c * C, (c + 1) * C), slice(qd * RW_QUAD, (qd + 1) * RW_QUAD))
                 for c in range(n // C) for qd in range(n_quads)]
        cut = [{key: p[key][rs, ls] for key in ("a_dec", "r_dec", "v", "b_dec")} for rs, ls in items]

        for it, (rs, ls) in zip(cut, items):
            ar = jnp.concatenate([it["a_dec"], it["r_dec"].astype(BF16)], axis=0)
            lm_b = _dot_nt(ar, bd(p["b_inv"][rs, ls]))
            lm_k = _dot_nt(ar, bd(p["k_inv"][rs, ls]))
            it["l_ab"] = jnp.where(strict, lm_b[:C], 0.0)
            it["m_rb"] = jnp.where(incl, lm_b[C:], 0.0).astype(BF16)
            it["m_rk"] = jnp.where(incl, lm_k[C:], 0.0).astype(BF16)
            it["v_bd"] = bd(it["v"])
            it["lv"] = _dot(jnp.where(strict, lm_k[:C], 0.0).astype(BF16), it["v_bd"])
            it["inv_l"] = eye_q + it["l_ab"]
            it["lp"] = it["l_ab"]
            fill()

        for level in range(5):
            for it in cut:
                lp_bd = bd(it["lp"])
                if level == 0:
                    it["lp"] = _dot(it["lp"].astype(BF16), lp_bd)
                else:
                    both = jnp.concatenate([it["inv_l"], it["lp"]], axis=0).astype(BF16)
                    both = _dot(both, lp_bd)
                    it["inv_l"] = it["inv_l"] + both[:C]
                    it["lp"] = both[C:]
            fill(3)
        for it in cut:
            it["inv_l"] = it["inv_l"] + _dot(it["inv_l"].astype(BF16), bd(it["lp"]))
        fill(3)

        for it, (rs, ls) in zip(cut, items):
            inv_b = it["inv_l"].astype(BF16)
            w1 = _dot(inv_b, bd(it["a_dec"]))
            u0 = _dot(inv_b, bd(it["lv"]))
            it["q_eff"] = (it["r_dec"] + _dot(it["m_rb"], bd(w1))).astype(BF16)
            it["y_intra"] = _dot(it["m_rb"], bd(u0)) + _dot(it["m_rk"], it["v_bd"])
            it["pt_bd"] = (jnp.where(same_head, _dot_tn(it["b_dec"], w1.astype(BF16)), 0.0)
                           + jnp.where(eye_bd, p["e_last"][rs, ls][:1], 0.0)).astype(BF16)
            keys = jnp.concatenate([it["b_dec"], p["k_dec"][rs, ls]], axis=0)
            vals = jnp.concatenate([u0.astype(BF16), it["v"]], axis=0)
            it["dt_bd"] = jnp.where(same_head, _dot_tn(keys, vals), 0.0)
            fill()

        rows_out = []
        for c in range(n // C):
            ys = []
            for qd in range(n_quads):
                it = cut[c * n_quads + qd]
                st = st_ref[qd].astype(BF16)
                ys.append(it["y_intra"] + _dot(it["q_eff"], st))
                st_ref[qd] = _dot(it["pt_bd"], st) + it["dt_bd"]
            rows_out.append(jnp.concatenate(ys, axis=1))
            fill()
        y = jnp.concatenate(rows_out, axis=0)
        mean = head_sum(y) * (1.0 / RW_HEAD_DIM)
        yc = y - mean
        var = head_sum(yc * yc) * (1.0 / RW_HEAD_DIM)
        y = yc * lax.rsqrt(var + RW_GN_EPS) * gng_ref[...] + gnb_ref[...]
        o_ref[rows] = ((y + p["bonus"]) * p["gate"]).astype(o_ref.dtype)

    whole = slice(0, T)
    prepared = {}
    for _ in prepare(whole, prepared):
        fill(4)
    fill(4)
    process(whole, prepared, fill)


def _proj_mix_kernel(h_ref, wrw_ref, wsb_ref, whg_ref, wgt_ref,
                     mu_ref, w0_ref, wup_ref, a0_ref, aup_ref, gup_ref, kk_ref, ka_ref, rk_ref,
                     gng_ref, gnb_ref, lbraw_ref, ng_ref,
                     q_ref, k_ref, v_ref, gate_ref, yrw_ref, yhg_ref,
                     prev_ref, rwst_ref, hgst_ref, urw_ref, uhg_ref,
                     *, layer, tiles_per_seq, hg_tile):
    i = pl.program_id(0)

    @pl.when(i == 0)
    def _():
        urw_ref[...] = jnp.zeros_like(urw_ref)
        uhg_ref[...] = jnp.zeros_like(uhg_ref)

    @pl.when(jnp.maximum(i - 1, 0) % tiles_per_seq == 0)
    def _():
        prev_ref[...] = jnp.zeros_like(prev_ref)
        rwst_ref[...] = jnp.zeros_like(rwst_ref)
        hgst_ref[...] = jnp.zeros_like(hgst_ref)

    hb = h_ref[...].astype(BF16)

    def piece(w_ref, c, store):
        cols = slice(c * MXU_COLS, (c + 1) * MXU_COLS)
        store(cols, _dot(hb, w_ref[:, cols]))

    def store_rw(cols, y):
        urw_ref[:, cols] = y

    def store_hg(cols, y):
        uhg_ref[:, cols] = y

    def store_gate(cols, y):
        gate_ref[:, cols] = _sigmoid(y).astype(BF16)

    def store_sb(cols, y):
        o_ref = (q_ref, k_ref, v_ref)[cols.start // SB_WIDTH]
        local = slice(cols.start % SB_WIDTH, cols.start % SB_WIDTH + MXU_COLS)
        o_ref[:, local] = (y * (SB_HEAD_DIM ** -0.5) if o_ref is q_ref else y).astype(BF16)

    def pieces_of(w_ref, store):
        return [functools.partial(piece, w_ref, c, store) for c in range(w_ref.shape[1] // MXU_COLS)]

    def hgrn2_stages():
        tiles = []
        for j in range(hb.shape[0] // hg_tile):
            rows = slice(j * hg_tile, (j + 1) * hg_tile)
            tiles.append(_hgrn2_tile(uhg_ref, lbraw_ref, ng_ref, yhg_ref, rows, hgst_ref, layer))
            next(tiles[-1])
            yield
        for t in tiles:
            yield from t
            yield

    hgrn2 = hgrn2_stages()
    hgrn2_step = functools.partial(next, hgrn2, None)
    for _ in range(hb.shape[0] // hg_tile):
        hgrn2_step()
    queue = pieces_of(whg_ref, store_hg) + pieces_of(wrw_ref, store_rw)
    for p in pieces_of(wsb_ref, store_sb) + pieces_of(wgt_ref, store_gate):
        queue += [hgrn2_step, p]

    def fill(n=1):
        for _ in range(n):
            if queue:
                queue.pop(0)()

    _rwkv7_tile(urw_ref[...], mu_ref, w0_ref, wup_ref, a0_ref, aup_ref, gup_ref, kk_ref,
                ka_ref, rk_ref, gng_ref, gnb_ref, yrw_ref, prev_ref, rwst_ref, fill)
    fill(len(queue))
    for _ in hgrn2:
        pass


def _proj_mix(h, w_rw, w_sb, w_hg, w_gate, rw_params, lb_raw, norm_g, layer, seq, tm, hg_tile):
    n = h.shape[0]
    nt = n // tm

    def projected(c):
        return pl.BlockSpec((tm, c), lambda i: (jnp.minimum(i, nt - 1), 0))

    def mixed(c):
        return pl.BlockSpec((tm, c), lambda i: (jnp.maximum(i - 1, 0), 0))

    params = (w_rw, w_sb, w_hg, w_gate) + tuple(rw_params) + (lb_raw, norm_g)
    return pl.pallas_call(
        functools.partial(_proj_mix_kernel, layer=layer, tiles_per_seq=seq // tm, hg_tile=hg_tile),
        grid=(nt + 1,),
        in_specs=[projected(D_MODEL)] + [_resident(a.shape) for a in params],
        out_specs=[projected(SB_WIDTH), projected(SB_WIDTH), projected(SB_WIDTH),
                   projected(3 * D_MODEL), mixed(RW_WIDTH), mixed(HG_WIDTH)],
        out_shape=[jax.ShapeDtypeStruct((n, SB_WIDTH), BF16),
                   jax.ShapeDtypeStruct((n, SB_WIDTH), BF16),
                   jax.ShapeDtypeStruct((n, SB_WIDTH), BF16),
                   jax.ShapeDtypeStruct((n, 3 * D_MODEL), BF16),
                   jax.ShapeDtypeStruct((n, RW_WIDTH), BF16),
                   jax.ShapeDtypeStruct((n, HG_WIDTH), BF16)],
        scratch_shapes=[pltpu.VMEM((1, RW_U_COLS), F32),
                        pltpu.VMEM((RW_WIDTH // RW_QUAD, RW_QUAD, RW_QUAD), F32),
                        pltpu.VMEM((HG_HEADS, HG_DIM, HG_DIM), F32),
                        pltpu.VMEM((tm, RW_U_COLS), F32),
                        pltpu.VMEM((tm, 4 * HG_WIDTH), F32)],
        compiler_params=pltpu.CompilerParams(dimension_semantics=("arbitrary",),
                                             vmem_limit_bytes=VMEM_LIMIT),
        name="proj_mix",
    )(h, *params)


def _pad_axis(w, axis, size):
    pads = [(0, 0)] * w.ndim
    pads[axis] = (0, size - w.shape[axis])
    return jnp.pad(w, pads)


def _split_rw_cols(w):
    o = 3 * RW_WIDTH
    parts = [w[..., :o]]
    for size, pad in zip((RW_DECAY_LORA, RW_AAA_LORA, RW_GATE_LORA), RW_LORA_PAD):
        parts.append(_pad_axis(w[..., o:o + size], -1, pad))
        o += size
    return jnp.concatenate(parts, axis=-1)


def kernel(x, p, ln_g, ln_b, ffn1_wg, ffn1_wu, ffn1_wd, w_in, rw_mu, rw_w0, rw_w_up, rw_a0, rw_a_up, rw_g_up, rw_k_k, rw_k_a, rw_r_k, rw_gn_g, rw_gn_b, hg_lb_raw, hg_norm_g, w_br_rw, w_br_sb, w_br_hg, w_out, ffn2_wg, ffn2_wu, ffn2_wd, ple_gate, ple_proj):
    bsz, seq, _ = x.shape
    n = bsz * seq
    tm = min(256, seq)
    sb_blk = min(256, seq)
    hg_tile = min(128, seq)
    rw_cols = 3 * RW_WIDTH + RW_DECAY_LORA + RW_AAA_LORA + RW_GATE_LORA
    sb_end = rw_cols + 3 * SB_WIDTH
    hg_end = sb_end + 4 * HG_WIDTH

    ffn1 = [w.astype(BF16) for w in (ffn1_wg, ffn1_wu, ffn1_wd)]
    proj = [_split_rw_cols(w_in[..., :rw_cols]).astype(BF16), w_in[..., rw_cols:sb_end].astype(BF16),
            w_in[..., sb_end:hg_end].astype(BF16), w_in[..., hg_end:].astype(BF16)]
    loras = [_pad_axis(w, 1, pad).astype(BF16)
             for w, pad in zip((rw_w_up, rw_a_up, rw_g_up), RW_LORA_PAD)]
    rw_mu_cols = _split_rw_cols(rw_mu)
    tail = [w.astype(BF16) for w in (w_br_rw, w_br_sb, w_br_hg, w_out, ffn2_wg, ffn2_wu, ffn2_wd,
                                     ple_gate, ple_proj)]

    h = x.reshape(n, D_MODEL)
    for i in range(DEPTH):
        def vec(a):
            return a[i].reshape(1, -1)

        h = _ffn_ln(h, *(w[i] for w in ffn1), vec(ln_g[:, 0]), vec(ln_b[:, 0]), min(512, n))
        rw_params = (vec(rw_mu_cols), vec(rw_w0), loras[0][i], vec(rw_a0), loras[1][i], loras[2][i],
                     vec(rw_k_k), vec(rw_k_a), vec(rw_r_k), vec(rw_gn_g), vec(rw_gn_b))
        q, k, v, gate, y_rw, y_hg = _proj_mix(h, *(w[i] for w in proj), rw_params, hg_lb_raw,
                                              vec(hg_norm_g), i, seq, tm, hg_tile)
        y_sb = _sb_attention(q, k, v, bsz, seq, sb_blk)
        h = _layer_tail(h, y_rw, y_sb, y_hg, gate, p[i].reshape(n, PLE_DIM), [w[i] for w in tail],
                        ln_g[i, 1:], ln_b[i, 1:], min(512, n))
    return h.reshape(bsz, seq, D_MODEL)
```

```python
import functools

import jax
import jax.numpy as jnp
from jax import lax
from jax.experimental import pallas as pl
from jax.experimental.pallas import tpu as pltpu

F32 = jnp.float32
BF16 = jnp.bfloat16

D_MODEL = 1024
DEPTH = 4
PLE_DIM = 256
D_FF = 2816
RW_HEADS = 8
RW_HEAD_DIM = 64
RW_WIDTH = RW_HEADS * RW_HEAD_DIM
RW_CHUNK = 64
RW_DECAY_LORA = 64
RW_AAA_LORA = 64
RW_GATE_LORA = 160
RW_GN_EPS = 64e-5
SB_HEADS = 4
SB_HEAD_DIM = 128
SB_WIDTH = SB_HEADS * SB_HEAD_DIM
SB_DEAD_LOG = -105.0
HG_HEADS = 4
HG_DIM = 128
HG_WIDTH = HG_HEADS * HG_DIM
HG_EPS = 1e-5
LOG2_E = 1.4426950408889634
LN_EPS = 1e-5
ALPHA = (2 * DEPTH) ** 0.25

MXU_COLS = 256
RW_QUAD = MXU_COLS
FFN_COL_GROUP = 6 * MXU_COLS
RW_LORA_PAD = (128, 128, 256)
RW_U_COLS = 3 * RW_WIDTH + sum(RW_LORA_PAD)
PROJ_WIDTHS = {"rw": RW_U_COLS, "sb": 3 * SB_WIDTH, "hg": 4 * HG_WIDTH, "gate": 3 * D_MODEL}
PROJ_BASES = {"rw": 0, "sb": RW_U_COLS, "hg": RW_U_COLS + 3 * SB_WIDTH,
              "gate": RW_U_COLS + 3 * SB_WIDTH + 4 * HG_WIDTH}
PROJ_COLS = sum(PROJ_WIDTHS.values())
VMEM_LIMIT = 56 * 1024 * 1024


def _resident(shape):
    return pl.BlockSpec(shape, lambda *_: (0,) * len(shape), pipeline_mode=pl.Buffered(1))


def _dot(a, b):
    return jnp.dot(a, b, preferred_element_type=F32)


def _dot_nt(a, b):
    return lax.dot_general(a, b, (((1,), (1,)), ((), ())), preferred_element_type=F32)


def _dot_tn(a, b):
    return lax.dot_general(a, b, (((0,), (0,)), ((), ())), preferred_element_type=F32)


def _split2(x):
    hi = x.astype(BF16)
    lo = (x - hi.astype(F32)).astype(BF16)
    return hi, lo


def _split3(x):
    hi = x.astype(BF16)
    r = x - hi.astype(F32)
    mid = r.astype(BF16)
    lo = (r - mid.astype(F32)).astype(BF16)
    return hi, mid, lo


def _sigmoid(x):
    return 0.5 * jnp.tanh(0.5 * x) + 0.5


def _softplus(x):
    return jnp.maximum(x, 0.0) + jnp.log(1.0 + jnp.exp(-jnp.abs(x)))


def _layer_norm(x, g, b):
    mu = jnp.mean(x, axis=-1, keepdims=True)
    xc = x - mu
    var = jnp.mean(xc * xc, axis=-1, keepdims=True)
    return xc * lax.rsqrt(var + LN_EPS) * g + b


def _swiglu(hb, wg_ref, wu_ref, wd_ref, out):
    y = None
    for c0 in range(0, D_FF, FFN_COL_GROUP):
        cols = slice(c0, min(c0 + FFN_COL_GROUP, D_FF))
        g = _dot(hb, wg_ref[:, cols])
        u = _dot(hb, wu_ref[:, cols])
        yield
        part = _dot((g * _sigmoid(g) * u).astype(BF16), wd_ref[cols, :])
        y = part if y is None else y + part
        yield
    out.append(y)


def _round_robin(generators):
    live = list(generators)
    while live:
        for gen in list(live):
            if next(gen, StopIteration) is StopIteration:
                live.remove(gen)


def _row_groups(rows):
    half = rows // 2 if rows % 16 == 0 else rows
    return [slice(r, r + half) for r in range(0, rows, half)]


def _ffn_ln_kernel(h_ref, wg_ref, wu_ref, wd_ref, lng_ref, lnb_ref, o_ref):
    def rows_stage(rows):
        h = h_ref[rows]
        y = []
        yield from _swiglu(h.astype(BF16), wg_ref, wu_ref, wd_ref, y)
        o_ref[rows] = _layer_norm(ALPHA * h + 0.5 * y[0], lng_ref[...], lnb_ref[...])

    _round_robin(rows_stage(rows) for rows in _row_groups(h_ref.shape[0]))


def _ffn_ln(h, wg, wu, wd, ln_g, ln_b, tm):
    n = h.shape[0]
    row = pl.BlockSpec((tm, D_MODEL), lambda i: (i, 0))
    return pl.pallas_call(
        _ffn_ln_kernel,
        grid=(n // tm,),
        in_specs=[row, _resident((D_MODEL, D_FF)), _resident((D_MODEL, D_FF)),
                  _resident((D_FF, D_MODEL)), _resident((1, D_MODEL)), _resident((1, D_MODEL))],
        out_specs=row,
        out_shape=jax.ShapeDtypeStruct((n, D_MODEL), F32),
        compiler_params=pltpu.CompilerParams(dimension_semantics=("arbitrary",),
                                             vmem_limit_bytes=VMEM_LIMIT),
        name="ffn_ln",
    )(h, wg, wu, wd, ln_g, ln_b)


def _layer_tail_kernel(h_ref, yrw_ref, ysb_ref, yhg_ref, gate_ref, p_ref,
                       wrw_ref, wsb_ref, whg_ref, wout_ref, wg_ref, wu_ref, wd_ref, pg_ref, pp_ref,
                       lng_ref, lnb_ref, o_ref):
    def rows_stage(rows):
        branches = [_dot(y_ref[rows], w_ref[...]) for y_ref, w_ref in
                    ((yrw_ref, wrw_ref), (ysb_ref, wsb_ref), (yhg_ref, whg_ref))]
        yield
        gate = gate_ref[rows]
        merged = sum(gate[:, j * D_MODEL:(j + 1) * D_MODEL] * b for j, b in enumerate(branches))
        mix = _dot(merged.astype(BF16), wout_ref[...])
        yield
        h = _layer_norm(ALPHA * h_ref[rows] + mix, lng_ref[0:1], lnb_ref[0:1])
        y = []
        yield from _swiglu(h.astype(BF16), wg_ref, wu_ref, wd_ref, y)
        h = _layer_norm(ALPHA * h + 0.5 * y[0], lng_ref[1:2], lnb_ref[1:2])
        ple_gate = _dot(h.astype(BF16), pg_ref[...])
        ple_in = _dot(p_ref[rows].astype(BF16), pp_ref[...])
        yield
        o_ref[rows] = _layer_norm(ALPHA * h + _sigmoid(ple_gate) * ple_in, lng_ref[2:3], lnb_ref[2:3])

    _round_robin(rows_stage(rows) for rows in _row_groups(h_ref.shape[0]))


def _layer_tail(h, y_rw, y_sb, y_hg, gate, p, weights, ln_g, ln_b, tm):
    n = h.shape[0]

    def row(c):
        return pl.BlockSpec((tm, c), lambda i: (i, 0))

    return pl.pallas_call(
        _layer_tail_kernel,
        grid=(n // tm,),
        in_specs=[row(D_MODEL), row(RW_WIDTH), row(SB_WIDTH), row(HG_WIDTH), row(3 * D_MODEL),
                  row(PLE_DIM)] + [_resident(w.shape) for w in weights]
                 + [_resident(ln_g.shape), _resident(ln_b.shape)],
        out_specs=row(D_MODEL),
        out_shape=jax.ShapeDtypeStruct((n, D_MODEL), F32),
        compiler_params=pltpu.CompilerParams(dimension_semantics=("arbitrary",),
                                             vmem_limit_bytes=VMEM_LIMIT),
        name="layer_tail",
    )(h, y_rw, y_sb, y_hg, gate, p, *weights, ln_g, ln_b)


def _sb_kernel(q_ref, k_ref, v_ref, o_ref, acc_ref, carry_ref, *, blk):
    i = pl.program_id(1)
    row = lax.broadcasted_iota(jnp.int32, (blk, blk), 0)
    col = lax.broadcasted_iota(jnp.int32, (blk, blk), 1)
    suffix = jnp.where(row >= col, 1.0, 0.0).astype(BF16)
    strict = col < row

    lanes = [slice(h * SB_HEAD_DIM, (h + 1) * SB_HEAD_DIM) for h in range(SB_HEADS)]

    def neg_softplus(z):
        return -(jnp.maximum(z, 0.0) + jnp.log(1.0 + jnp.exp(-jnp.abs(z))))

    def key_block(start):
        zs = [_dot_nt(q_ref[:, sl], k_ref[pl.ds(start, blk), sl]) for sl in lanes]
        withins = [_dot(neg_softplus(z).astype(BF16), suffix) for z in zs]
        atts = [jnp.exp(z + w + carry_ref[h]) for h, (z, w) in enumerate(zip(zs, withins))]
        for h, sl in enumerate(lanes):
            acc_ref[:, sl] += _dot(atts[h].astype(BF16), v_ref[pl.ds(start, blk), sl])
            carry_ref[h] += withins[h][:, 0:1]

    has_prev = i > 0
    d_start = pl.multiple_of(i * blk, blk)
    p_start = pl.multiple_of(jnp.maximum(i - 1, 0) * blk, blk)
    z_d = [_dot_nt(q_ref[:, sl], k_ref[pl.ds(d_start, blk), sl]) for sl in lanes]
    z_p = [_dot_nt(q_ref[:, sl], k_ref[pl.ds(p_start, blk), sl]) for sl in lanes]
    m_d = [jnp.where(strict, neg_softplus(z), 0.0) for z in z_d]
    m_p = [jnp.where(has_prev, neg_softplus(z), 0.0) for z in z_p]
    w_d = [_dot(m.astype(BF16), suffix) for m in m_d]
    w_p = [_dot(m.astype(BF16), suffix) for m in m_p]
    for h, sl in enumerate(lanes):
        att_d = jnp.where(strict, jnp.exp(z_d[h] + w_d[h]), 0.0)
        att_p = jnp.where(has_prev, jnp.exp(z_p[h] + w_p[h] + w_d[h][:, 0:1]), 0.0)
        acc_ref[:, sl] = (_dot(att_d.astype(BF16), v_ref[pl.ds(d_start, blk), sl])
                          + _dot(att_p.astype(BF16), v_ref[pl.ds(p_start, blk), sl]))
        carry_ref[h] = w_d[h][:, 0:1] + w_p[h][:, 0:1]

    def more(t):
        return jnp.logical_and(t < i, jnp.max(carry_ref[...]) > SB_DEAD_LOG)

    def body(t):
        key_block(pl.multiple_of((i - 1 - t) * blk, blk))
        return t + 1

    lax.while_loop(more, body, 1)
    o_ref[...] = acc_ref[...].astype(o_ref.dtype)


def _sb_attention(q, k, v, bsz, seq, blk):
    n = bsz * seq
    nq = seq // blk
    qspec = pl.BlockSpec((blk, SB_WIDTH), lambda b, i: (b * nq + i, 0))
    kvspec = pl.BlockSpec((seq, SB_WIDTH), lambda b, i: (b, 0))
    return pl.pallas_call(
        functools.partial(_sb_kernel, blk=blk),
        grid=(bsz, nq),
        in_specs=[qspec, kvspec, kvspec],
        out_specs=qspec,
        out_shape=jax.ShapeDtypeStruct((n, SB_WIDTH), BF16),
        scratch_shapes=[pltpu.VMEM((blk, SB_WIDTH), F32), pltpu.VMEM((SB_HEADS, blk, 1), F32)],
        compiler_params=pltpu.CompilerParams(dimension_semantics=("arbitrary", "arbitrary"),
                                             vmem_limit_bytes=VMEM_LIMIT),
        name="stick_breaking",
    )(q, k, v)


def _hgrn2_tile(u_ref, lbraw_ref, ng_ref, o_ref, rows, st_ref, layer):
    u = u_ref[rows]
    tile = u.shape[0]
    raw = lbraw_ref[...]
    e = jnp.exp(raw - jnp.max(raw, axis=0, keepdims=True))
    sm = e / jnp.sum(e, axis=0, keepdims=True)
    lb = jnp.zeros((1, HG_WIDTH), F32)
    for l in range(1, layer + 1):
        lb = lb + sm[l:l + 1]

    q = _sigmoid(u[:, :HG_WIDTH])
    f_raw = u[:, HG_WIDTH:2 * HG_WIDTH]
    val = u[:, 2 * HG_WIDTH:3 * HG_WIDTH]
    out_gate = u[:, 3 * HG_WIDTH:]

    half_tanh = 0.5 * jnp.tanh(0.5 * f_raw)
    sig_pos = 0.5 + half_tanh
    sig_neg = 0.5 - half_tanh
    log_sig = jnp.minimum(f_raw, 0.0) - jnp.log(1.0 + jnp.exp(-jnp.abs(f_raw)))
    if layer == 0:
        log_f = log_sig
    else:
        log_f = jnp.where(lb > 0.0, jnp.log(lb + (1.0 - lb) * sig_pos), log_sig)
    kg = (1.0 - lb) * sig_neg
    yield

    pos = lax.broadcasted_iota(jnp.int32, (tile, HG_WIDTH), 0)
    ti = lax.broadcasted_iota(jnp.int32, (tile, tile), 0)
    si = lax.broadcasted_iota(jnp.int32, (tile, tile), 1)
    prefix = jnp.where(si <= ti, 1.0, 0.0).astype(BF16)
    g = sum(_dot(prefix, piece) for piece in _split3(log_f))

    heads = [slice(h * HG_DIM, (h + 1) * HG_DIM) for h in range(HG_HEADS)]
    qb = q.astype(BF16)
    kgb = kg.astype(BF16)
    att = [jnp.where(ti == si, _dot_nt(qb[:, sl], kgb[:, sl]), 0.0) for sl in heads]
    last = g
    w = 1
    while w < tile:
        upper = (pos & (2 * w - 1)) >= w
        g_ref = jnp.where(upper, pltpu.roll(last, w, 0), last)
        last = jnp.where(upper, last, pltpu.roll(last, tile - w, 0))
        decay = jnp.exp2(jnp.abs(g - g_ref) * -LOG2_E)
        z = (jnp.where(upper, q, kg) * decay).astype(BF16)
        pair = ((ti & -(2 * w)) == (si & -(2 * w))) & ((ti & w) != 0) & ((si & w) == 0)
        att = [a + jnp.where(pair, _dot_nt(z[:, sl], z[:, sl]), 0.0) for a, sl in zip(att, heads)]
        yield
        w *= 2

    valb = val.astype(BF16)
    qg = (q * jnp.exp(g)).astype(BF16)
    g_last = g[tile - 1:tile]
    kd = (kg * jnp.exp(g_last - g)).astype(BF16)
    gam = jnp.exp(g_last)
    ng = ng_ref[...]
    yield
    for h, sl in enumerate(heads):
        st = st_ref[h]
        o = _dot(att[h].astype(BF16), valb[:, sl]) + _dot_nt(qg[:, sl], st.astype(BF16))
        o = o * lax.rsqrt(jnp.mean(o * o, axis=-1, keepdims=True) + HG_EPS) * ng[:, sl]
        gate = out_gate[:, sl]
        o_ref[rows, sl] = (o * (gate * _sigmoid(gate))).astype(o_ref.dtype)
        st_ref[h] = st * gam[:, sl] + _dot_tn(valb[:, sl], kd[:, sl])


def _quad_mask(rows):
    r = lax.broadcasted_iota(jnp.int32, (rows, RW_QUAD), 0)
    c = lax.broadcasted_iota(jnp.int32, (rows, RW_QUAD), 1)
    return r, c


def _block_diag(y, head_ones):
    return jnp.concatenate([y, y, y, y], axis=0) * head_ones


def _rwkv7_tile(u, mu_ref, w0_ref, wup_ref, a0_ref, aup_ref, gup_ref, kk_ref, ka_ref,
                rk_ref, gng_ref, gnb_ref, o_ref, prev_ref, st_ref, fill):
    C = RW_CHUNK
    W = RW_WIDTH
    T = u.shape[0]
    trow = lax.broadcasted_iota(jnp.int32, u.shape, 0)
    shifted = jnp.where(trow == 0, prev_ref[...], pltpu.roll(u, 1, 0))
    prev_ref[...] = u[T - 1:T]
    u = u + (shifted - u) * mu_ref[...]
    fill(4)

    br, bc = _quad_mask(RW_QUAD)
    same_head = (br // RW_HEAD_DIM) == (bc // RW_HEAD_DIM)
    head_ones = jnp.where(same_head, 1.0, 0.0).astype(BF16)
    eye_bd = br == bc
    tr, tc = _quad_mask(C)
    s_idx = tc % RW_HEAD_DIM
    strict = s_idx < tr
    incl = s_idx <= tr
    eye_q = jnp.where(s_idx == tr, 1.0, 0.0)
    n_quads = W // RW_QUAD

    def bd(y):
        return _block_diag(y.astype(BF16), head_ones)

    def head_sum(x):
        n = x.shape[0]
        pieces = [p[:, sl] for p in _split2(x) for sl in (slice(0, RW_QUAD), slice(RW_QUAD, W))]
        s = _dot(jnp.concatenate(pieces, axis=0), head_ones)
        halves = [s[j * n:(j + 1) * n] + s[(j + 2) * n:(j + 3) * n] for j in range(2)]
        return jnp.concatenate(halves, axis=1)

    def prepare(rows, out):
        ur = u[rows]
        n = ur.shape[0]
        r = ur[:, :W]
        k = ur[:, W:2 * W]
        v = ur[:, 2 * W:3 * W]
        o0 = 3 * W
        xw = ur[:, o0:o0 + RW_LORA_PAD[0]]
        xa = ur[:, o0 + RW_LORA_PAD[0]:o0 + RW_LORA_PAD[0] + RW_LORA_PAD[1]]
        xg = ur[:, o0 + RW_LORA_PAD[0] + RW_LORA_PAD[1]:]
        w_log = -_softplus(-(w0_ref[...] + _dot(jnp.tanh(xw).astype(BF16), wup_ref[...]))) - 0.5
        log_w = -jnp.exp(w_log)
        a = _sigmoid(a0_ref[...] + _dot(xa.astype(BF16), aup_ref[...]))
        gate = _dot(_sigmoid(xg).astype(BF16), gup_ref[...])
        yield
        kk = k * kk_ref[...]
        kk = kk * jnp.minimum(lax.rsqrt(head_sum(kk * kk)), 1e12)
        k = k * (1.0 + (a - 1.0) * ka_ref[...])
        bonus = head_sum(r * k * rk_ref[...]) * v
        a_vec = -kk
        b_vec = kk * a
        yield
        tt = lax.broadcasted_iota(jnp.int32, (n, n), 0)
        ts = lax.broadcasted_iota(jnp.int32, (n, n), 1)
        prefix = jnp.where((tt // C == ts // C) & (ts <= tt), 1.0, 0.0).astype(BF16)
        g = sum(_dot(prefix, piece) for piece in _split3(log_w))
        yield
        g_last = jnp.concatenate(
            [jnp.broadcast_to(g[c * C + C - 1:c * C + C], (C, W)) for c in range(n // C)], axis=0)
        inv = jnp.exp(-g)
        dec = jnp.exp(g_last - g)
        out.update(a_dec=(a_vec * jnp.exp(g - log_w)).astype(BF16), r_dec=r * jnp.exp(g),
                   b_inv=(b_vec * inv).astype(BF16), k_inv=(k * inv).astype(BF16),
                   b_dec=(b_vec * dec).astype(BF16), k_dec=(k * dec).astype(BF16),
                   v=v.astype(BF16), e_last=jnp.exp(g_last), bonus=bonus, gate=gate)

    def process(rows, p, fill):
        n = p["gate"].shape[0]
        items = [(slice(c * C, (c + 1) * C), slice(qd * RW_QUAD, (qd + 1) * RW_QUAD))
                 for c in range(n // C) for qd in range(n_quads)]
        cut = [{key: p[key][rs, ls] for key in ("a_dec", "r_dec", "v", "b_dec")} for rs, ls in items]

        for it, (rs, ls) in zip(cut, items):
            ar = jnp.concatenate([it["a_dec"], it["r_dec"].astype(BF16)], axis=0)
            lm_b = _dot_nt(ar, bd(p["b_inv"][rs, ls]))
            lm_k = _dot_nt(ar, bd(p["k_inv"][rs, ls]))
            it["l_ab"] = jnp.where(strict, lm_b[:C], 0.0)
            it["m_rb"] = jnp.where(incl, lm_b[C:], 0.0).astype(BF16)
            it["m_rk"] = jnp.where(incl, lm_k[C:], 0.0).astype(BF16)
            it["v_bd"] = bd(it["v"])
            it["lv"] = _dot(jnp.where(strict, lm_k[:C], 0.0).astype(BF16), it["v_bd"])
            it["inv_l"] = eye_q + it["l_ab"]
            it["lp"] = it["l_ab"]
            fill()

        for level in range(5):
            for it in cut:
                lp_bd = bd(it["lp"])
                if level == 0:
                    it["lp"] = _dot(it["lp"].astype(BF16), lp_bd)
                else:
                    both = jnp.concatenate([it["inv_l"], it["lp"]], axis=0).astype(BF16)
                    both = _dot(both, lp_bd)
                    it["inv_l"] = it["inv_l"] + both[:C]
                    it["lp"] = both[C:]
            fill(3)
        for it in cut:
            it["inv_l"] = it["inv_l"] + _dot(it["inv_l"].astype(BF16), bd(it["lp"]))
        fill(3)

        for it, (rs, ls) in zip(cut, items):
            inv_b = it["inv_l"].astype(BF16)
            w1 = _dot(inv_b, bd(it["a_dec"]))
            u0 = _dot(inv_b, bd(it["lv"]))
            it["q_eff"] = (it["r_dec"] + _dot(it["m_rb"], bd(w1))).astype(BF16)
            it["y_intra"] = _dot(it["m_rb"], bd(u0)) + _dot(it["m_rk"], it["v_bd"])
            it["pt_bd"] = (jnp.where(same_head, _dot_tn(it["b_dec"], w1.astype(BF16)), 0.0)
                           + jnp.where(eye_bd, p["e_last"][rs, ls][:1], 0.0)).astype(BF16)
            keys = jnp.concatenate([it["b_dec"], p["k_dec"][rs, ls]], axis=0)
            vals = jnp.concatenate([u0.astype(BF16), it["v"]], axis=0)
            it["dt_bd"] = jnp.where(same_head, _dot_tn(keys, vals), 0.0)
            fill()

        rows_out = []
        for c in range(n // C):
            ys = []
            for qd in range(n_quads):
                it = cut[c * n_quads + qd]
                st = st_ref[qd].astype(BF16)
                ys.append(it["y_intra"] + _dot(it["q_eff"], st))
                st_ref[qd] = _dot(it["pt_bd"], st) + it["dt_bd"]
            rows_out.append(jnp.concatenate(ys, axis=1))
            fill()
        y = jnp.concatenate(rows_out, axis=0)
        mean = head_sum(y) * (1.0 / RW_HEAD_DIM)
        yc = y - mean
        var = head_sum(yc * yc) * (1.0 / RW_HEAD_DIM)
        y = yc * lax.rsqrt(var + RW_GN_EPS) * gng_ref[...] + gnb_ref[...]
        o_ref[rows] = ((y + p["bonus"]) * p["gate"]).astype(o_ref.dtype)

    whole = slice(0, T)
    prepared = {}
    for _ in prepare(whole, prepared):
        fill(4)
    fill(4)
    process(whole, prepared, fill)


def _proj_mix_kernel(h_ref, w_ref,
                     mu_ref, w0_ref, wup_ref, a0_ref, aup_ref, gup_ref, kk_ref, ka_ref, rk_ref,
                     gng_ref, gnb_ref, lbraw_ref, ng_ref,
                     q_ref, k_ref, v_ref, gate_ref, yrw_ref, yhg_ref,
                     prev_ref, rwst_ref, hgst_ref, urw_ref, uhg_ref,
                     *, layer, tiles_per_seq, hg_tile):
    i = pl.program_id(0)

    @pl.when(i == 0)
    def _():
        urw_ref[...] = jnp.zeros_like(urw_ref)
        uhg_ref[...] = jnp.zeros_like(uhg_ref)

    @pl.when(jnp.maximum(i - 1, 0) % tiles_per_seq == 0)
    def _():
        prev_ref[...] = jnp.zeros_like(prev_ref)
        rwst_ref[...] = jnp.zeros_like(rwst_ref)
        hgst_ref[...] = jnp.zeros_like(hgst_ref)

    hb = h_ref[...].astype(BF16)

    def piece(base, c, store):
        cols = slice(c * MXU_COLS, (c + 1) * MXU_COLS)
        store(cols, _dot(hb, w_ref[:, base + cols.start:base + cols.stop]))

    def store_rw(cols, y):
        urw_ref[:, cols] = y

    def store_hg(cols, y):
        uhg_ref[:, cols] = y

    def store_gate(cols, y):
        gate_ref[:, cols] = _sigmoid(y).astype(BF16)

    def store_sb(cols, y):
        o_ref = (q_ref, k_ref, v_ref)[cols.start // SB_WIDTH]
        local = slice(cols.start % SB_WIDTH, cols.start % SB_WIDTH + MXU_COLS)
        o_ref[:, local] = (y * (SB_HEAD_DIM ** -0.5) if o_ref is q_ref else y).astype(BF16)

    def pieces_of(group, store):
        base, width = PROJ_BASES[group], PROJ_WIDTHS[group]
        return [functools.partial(piece, base, c, store) for c in range(width // MXU_COLS)]

    def hgrn2_stages():
        tiles = []
        for j in range(hb.shape[0] // hg_tile):
            rows = slice(j * hg_tile, (j + 1) * hg_tile)
            tiles.append(_hgrn2_tile(uhg_ref, lbraw_ref, ng_ref, yhg_ref, rows, hgst_ref, layer))
            next(tiles[-1])
            yield
        for t in tiles:
            yield from t
            yield

    hgrn2 = hgrn2_stages()
    hgrn2_step = functools.partial(next, hgrn2, None)
    for _ in range(hb.shape[0] // hg_tile):
        hgrn2_step()
    queue = pieces_of("hg", store_hg) + pieces_of("rw", store_rw)
    for p in pieces_of("sb", store_sb) + pieces_of("gate", store_gate):
        queue += [hgrn2_step, p]

    def fill(n=1):
        for _ in range(n):
            if queue:
                queue.pop(0)()

    _rwkv7_tile(urw_ref[...], mu_ref, w0_ref, wup_ref, a0_ref, aup_ref, gup_ref, kk_ref,
                ka_ref, rk_ref, gng_ref, gnb_ref, yrw_ref, prev_ref, rwst_ref, fill)
    fill(len(queue))
    for _ in hgrn2:
        pass


def _proj_mix(h, w_proj, rw_params, lb_raw, norm_g, layer, seq, tm, hg_tile):
    n = h.shape[0]
    nt = n // tm

    def projected(c):
        return pl.BlockSpec((tm, c), lambda i: (jnp.minimum(i, nt - 1), 0))

    def mixed(c):
        return pl.BlockSpec((tm, c), lambda i: (jnp.maximum(i - 1, 0), 0))

    assert w_proj.shape == (D_MODEL, PROJ_COLS)
    params = (w_proj,) + tuple(rw_params) + (lb_raw, norm_g)
    return pl.pallas_call(
        functools.partial(_proj_mix_kernel, layer=layer, tiles_per_seq=seq // tm, hg_tile=hg_tile),
        grid=(nt + 1,),
        in_specs=[projected(D_MODEL)] + [_resident(a.shape) for a in params],
        out_specs=[projected(SB_WIDTH), projected(SB_WIDTH), projected(SB_WIDTH),
                   projected(3 * D_MODEL), mixed(RW_WIDTH), mixed(HG_WIDTH)],
        out_shape=[jax.ShapeDtypeStruct((n, SB_WIDTH), BF16),
                   jax.ShapeDtypeStruct((n, SB_WIDTH), BF16),
                   jax.ShapeDtypeStruct((n, SB_WIDTH), BF16),
                   jax.ShapeDtypeStruct((n, 3 * D_MODEL), BF16),
                   jax.ShapeDtypeStruct((n, RW_WIDTH), BF16),
                   jax.ShapeDtypeStruct((n, HG_WIDTH), BF16)],
        scratch_shapes=[pltpu.VMEM((1, RW_U_COLS), F32),
                        pltpu.VMEM((RW_WIDTH // RW_QUAD, RW_QUAD, RW_QUAD), F32),
                        pltpu.VMEM((HG_HEADS, HG_DIM, HG_DIM), F32),
                        pltpu.VMEM((tm, RW_U_COLS), F32),
                        pltpu.VMEM((tm, 4 * HG_WIDTH), F32)],
        compiler_params=pltpu.CompilerParams(dimension_semantics=("arbitrary",),
                                             vmem_limit_bytes=VMEM_LIMIT),
        name="proj_mix",
    )(h, *params)


def _pad_axis(w, axis, size):
    pads = [(0, 0)] * w.ndim
    pads[axis] = (0, size - w.shape[axis])
    return jnp.pad(w, pads)


def _split_rw_cols(w):
    o = 3 * RW_WIDTH
    parts = [w[..., :o]]
    for size, pad in zip((RW_DECAY_LORA, RW_AAA_LORA, RW_GATE_LORA), RW_LORA_PAD):
        parts.append(_pad_axis(w[..., o:o + size], -1, pad))
        o += size
    return jnp.concatenate(parts, axis=-1)


def kernel(x, p, ln_g, ln_b, ffn1_wg, ffn1_wu, ffn1_wd, w_in, rw_mu, rw_w0, rw_w_up, rw_a0, rw_a_up, rw_g_up, rw_k_k, rw_k_a, rw_r_k, rw_gn_g, rw_gn_b, hg_lb_raw, hg_norm_g, w_br_rw, w_br_sb, w_br_hg, w_out, ffn2_wg, ffn2_wu, ffn2_wd, ple_gate, ple_proj):
    bsz, seq, _ = x.shape
    n = bsz * seq
    tm = min(256, seq)
    sb_blk = min(256, seq)
    hg_tile = min(128, seq)
    rw_cols = 3 * RW_WIDTH + RW_DECAY_LORA + RW_AAA_LORA + RW_GATE_LORA

    ffn1 = [w.astype(BF16) for w in (ffn1_wg, ffn1_wu, ffn1_wd)]
    w_proj = jnp.concatenate([_split_rw_cols(w_in[..., :rw_cols]), w_in[..., rw_cols:]], axis=-1).astype(BF16)
    loras = [_pad_axis(w, 1, pad).astype(BF16)
             for w, pad in zip((rw_w_up, rw_a_up, rw_g_up), RW_LORA_PAD)]
    rw_mu_cols = _split_rw_cols(rw_mu)
    tail = [w.astype(BF16) for w in (w_br_rw, w_br_sb, w_br_hg, w_out, ffn2_wg, ffn2_wu, ffn2_wd,
                                     ple_gate, ple_proj)]

    h = x.reshape(n, D_MODEL)
    for i in range(DEPTH):
        def vec(a):
            return a[i].reshape(1, -1)

        h = _ffn_ln(h, *(w[i] for w in ffn1), vec(ln_g[:, 0]), vec(ln_b[:, 0]), min(512, n))
        rw_params = (vec(rw_mu_cols), vec(rw_w0), loras[0][i], vec(rw_a0), loras[1][i], loras[2][i],
                     vec(rw_k_k), vec(rw_k_a), vec(rw_r_k), vec(rw_gn_g), vec(rw_gn_b))
        q, k, v, gate, y_rw, y_hg = _proj_mix(h, w_proj[i], rw_params, hg_lb_raw, vec(hg_norm_g),
                                              i, seq, tm, hg_tile)
        y_sb = _sb_attention(q, k, v, bsz, seq, sb_blk)
        h = _layer_tail(h, y_rw, y_sb, y_hg, gate, p[i].reshape(n, PLE_DIM), [w[i] for w in tail],
                        ln_g[i, 1:], ln_b[i, 1:], min(512, n))
    return h.reshape(bsz, seq, D_MODEL)
```

```python
import functools

import jax
import jax.numpy as jnp
from jax import lax
from jax.experimental import pallas as pl
from jax.experimental.pallas import tpu as pltpu

F32 = jnp.float32
BF16 = jnp.bfloat16

D_MODEL = 1024
DEPTH = 4
PLE_DIM = 256
D_FF = 2816
RW_HEADS = 8
RW_HEAD_DIM = 64
RW_WIDTH = RW_HEADS * RW_HEAD_DIM
RW_CHUNK = 64
RW_DECAY_LORA = 64
RW_AAA_LORA = 64
RW_GATE_LORA = 160
RW_GN_EPS = 64e-5
SB_HEADS = 4
SB_HEAD_DIM = 128
SB_WIDTH = SB_HEADS * SB_HEAD_DIM
SB_DEAD_LOG = -105.0
HG_HEADS = 4
HG_DIM = 128
HG_WIDTH = HG_HEADS * HG_DIM
HG_EPS = 1e-5
LOG2_E = 1.4426950408889634
LN_EPS = 1e-5
ALPHA = (2 * DEPTH) ** 0.25

MXU_COLS = 256
RW_QUAD = MXU_COLS
FFN_COL_GROUP = 6 * MXU_COLS
RW_LORA_PAD = (128, 128, 256)
RW_U_COLS = 3 * RW_WIDTH + sum(RW_LORA_PAD)
PROJ_WIDTHS = {"rw": RW_U_COLS, "sb": 3 * SB_WIDTH, "hg": 4 * HG_WIDTH, "gate": 3 * D_MODEL}
PROJ_BASES = {"rw": 0, "sb": RW_U_COLS, "hg": RW_U_COLS + 3 * SB_WIDTH,
              "gate": RW_U_COLS + 3 * SB_WIDTH + 4 * HG_WIDTH}
PROJ_COLS = sum(PROJ_WIDTHS.values())
VMEM_LIMIT = 56 * 1024 * 1024


def _resident(shape):
    return pl.BlockSpec(shape, lambda *_: (0,) * len(shape), pipeline_mode=pl.Buffered(1))


def _resident_layer(stacked, layer):
    rest = stacked.shape[1:]
    return pl.BlockSpec((None,) + rest, lambda *_: (layer,) + (0,) * len(rest),
                        pipeline_mode=pl.Buffered(1))


def _dot(a, b):
    return jnp.dot(a, b, preferred_element_type=F32)


def _dot_nt(a, b):
    return lax.dot_general(a, b, (((1,), (1,)), ((), ())), preferred_element_type=F32)


def _dot_tn(a, b):
    return lax.dot_general(a, b, (((0,), (0,)), ((), ())), preferred_element_type=F32)


def _split2(x):
    hi = x.astype(BF16)
    lo = (x - hi.astype(F32)).astype(BF16)
    return hi, lo


def _split3(x):
    hi = x.astype(BF16)
    r = x - hi.astype(F32)
    mid = r.astype(BF16)
    lo = (r - mid.astype(F32)).astype(BF16)
    return hi, mid, lo


def _sigmoid(x):
    return 0.5 * jnp.tanh(0.5 * x) + 0.5


def _softplus(x):
    return jnp.maximum(x, 0.0) + jnp.log(1.0 + jnp.exp(-jnp.abs(x)))


def _layer_norm(x, g, b):
    mu = jnp.mean(x, axis=-1, keepdims=True)
    xc = x - mu
    var = jnp.mean(xc * xc, axis=-1, keepdims=True)
    return xc * lax.rsqrt(var + LN_EPS) * g + b


def _swiglu(hb, wg_ref, wu_ref, wd_ref, out):
    y = None
    for c0 in range(0, D_FF, FFN_COL_GROUP):
        cols = slice(c0, min(c0 + FFN_COL_GROUP, D_FF))
        g = _dot(hb, wg_ref[:, cols])
        u = _dot(hb, wu_ref[:, cols])
        yield
        part = _dot((g * _sigmoid(g) * u).astype(BF16), wd_ref[cols, :])
        y = part if y is None else y + part
        yield
    out.append(y)


def _round_robin(generators):
    live = list(generators)
    while live:
        for gen in list(live):
            if next(gen, StopIteration) is StopIteration:
                live.remove(gen)


def _row_groups(rows):
    half = rows // 2 if rows % 16 == 0 else rows
    return [slice(r, r + half) for r in range(0, rows, half)]


def _ffn_ln_kernel(h_ref, wg_ref, wu_ref, wd_ref, lng_ref, lnb_ref, o_ref):
    def rows_stage(rows):
        h = h_ref[rows]
        y = []
        yield from _swiglu(h.astype(BF16), wg_ref, wu_ref, wd_ref, y)
        o_ref[rows] = _layer_norm(ALPHA * h + 0.5 * y[0], lng_ref[...], lnb_ref[...])

    _round_robin(rows_stage(rows) for rows in _row_groups(h_ref.shape[0]))


def _ffn_ln(h, wg, wu, wd, layer, ln_g, ln_b, tm):
    n = h.shape[0]
    row = pl.BlockSpec((tm, D_MODEL), lambda i: (i, 0))
    return pl.pallas_call(
        _ffn_ln_kernel,
        grid=(n // tm,),
        in_specs=[row] + [_resident_layer(w, layer) for w in (wg, wu, wd)]
                 + [_resident((1, D_MODEL)), _resident((1, D_MODEL))],
        out_specs=row,
        out_shape=jax.ShapeDtypeStruct((n, D_MODEL), F32),
        compiler_params=pltpu.CompilerParams(dimension_semantics=("arbitrary",),
                                             vmem_limit_bytes=VMEM_LIMIT),
        name="ffn_ln",
    )(h, wg, wu, wd, ln_g, ln_b)


def _layer_tail_kernel(h_ref, yrw_ref, ysb_ref, yhg_ref, gate_ref, p_ref,
                       wrw_ref, wsb_ref, whg_ref, wout_ref, wg_ref, wu_ref, wd_ref, pg_ref, pp_ref,
                       lng_ref, lnb_ref, o_ref):
    def rows_stage(rows):
        branches = [_dot(y_ref[rows], w_ref[...]) for y_ref, w_ref in
                    ((yrw_ref, wrw_ref), (ysb_ref, wsb_ref), (yhg_ref, whg_ref))]
        yield
        gate = gate_ref[rows]
        merged = sum(gate[:, j * D_MODEL:(j + 1) * D_MODEL] * b for j, b in enumerate(branches))
        mix = _dot(merged.astype(BF16), wout_ref[...])
        yield
        h = _layer_norm(ALPHA * h_ref[rows] + mix, lng_ref[0:1], lnb_ref[0:1])
        y = []
        yield from _swiglu(h.astype(BF16), wg_ref, wu_ref, wd_ref, y)
        h = _layer_norm(ALPHA * h + 0.5 * y[0], lng_ref[1:2], lnb_ref[1:2])
        ple_gate = _dot(h.astype(BF16), pg_ref[...])
        ple_in = _dot(p_ref[rows].astype(BF16), pp_ref[...])
        yield
        o_ref[rows] = _layer_norm(ALPHA * h + _sigmoid(ple_gate) * ple_in, lng_ref[2:3], lnb_ref[2:3])

    _round_robin(rows_stage(rows) for rows in _row_groups(h_ref.shape[0]))


def _layer_tail(h, y_rw, y_sb, y_hg, gate, p, weights, layer, ln_g, ln_b, tm):
    n = h.shape[0]

    def row(c):
        return pl.BlockSpec((tm, c), lambda i: (i, 0))

    return pl.pallas_call(
        _layer_tail_kernel,
        grid=(n // tm,),
        in_specs=[row(D_MODEL), row(RW_WIDTH), row(SB_WIDTH), row(HG_WIDTH), row(3 * D_MODEL),
                  row(PLE_DIM)] + [_resident_layer(w, layer) for w in weights]
                 + [_resident(ln_g.shape), _resident(ln_b.shape)],
        out_specs=row(D_MODEL),
        out_shape=jax.ShapeDtypeStruct((n, D_MODEL), F32),
        compiler_params=pltpu.CompilerParams(dimension_semantics=("arbitrary",),
                                             vmem_limit_bytes=VMEM_LIMIT),
        name="layer_tail",
    )(h, y_rw, y_sb, y_hg, gate, p, *weights, ln_g, ln_b)


def _sb_kernel(q_ref, k_ref, v_ref, o_ref, acc_ref, carry_ref, *, blk):
    i = pl.program_id(1)
    row = lax.broadcasted_iota(jnp.int32, (blk, blk), 0)
    col = lax.broadcasted_iota(jnp.int32, (blk, blk), 1)
    suffix = jnp.where(row >= col, 1.0, 0.0).astype(BF16)
    strict = col < row

    lanes = [slice(h * SB_HEAD_DIM, (h + 1) * SB_HEAD_DIM) for h in range(SB_HEADS)]

    def neg_softplus(z):
        return -(jnp.maximum(z, 0.0) + jnp.log(1.0 + jnp.exp(-jnp.abs(z))))

    def key_block(start):
        zs = [_dot_nt(q_ref[:, sl], k_ref[pl.ds(start, blk), sl]) for sl in lanes]
        withins = [_dot(neg_softplus(z).astype(BF16), suffix) for z in zs]
        atts = [jnp.exp(z + w + carry_ref[h]) for h, (z, w) in enumerate(zip(zs, withins))]
        for h, sl in enumerate(lanes):
            acc_ref[:, sl] += _dot(atts[h].astype(BF16), v_ref[pl.ds(start, blk), sl])
            carry_ref[h] += withins[h][:, 0:1]

    has_prev = i > 0
    d_start = pl.multiple_of(i * blk, blk)
    p_start = pl.multiple_of(jnp.maximum(i - 1, 0) * blk, blk)
    z_d = [_dot_nt(q_ref[:, sl], k_ref[pl.ds(d_start, blk), sl]) for sl in lanes]
    z_p = [_dot_nt(q_ref[:, sl], k_ref[pl.ds(p_start, blk), sl]) for sl in lanes]
    m_d = [jnp.where(strict, neg_softplus(z), 0.0) for z in z_d]
    m_p = [jnp.where(has_prev, neg_softplus(z), 0.0) for z in z_p]
    w_d = [_dot(m.astype(BF16), suffix) for m in m_d]
    w_p = [_dot(m.astype(BF16), suffix) for m in m_p]
    for h, sl in enumerate(lanes):
        att_d = jnp.where(strict, jnp.exp(z_d[h] + w_d[h]), 0.0)
        att_p = jnp.where(has_prev, jnp.exp(z_p[h] + w_p[h] + w_d[h][:, 0:1]), 0.0)
        acc_ref[:, sl] = (_dot(att_d.astype(BF16), v_ref[pl.ds(d_start, blk), sl])
                          + _dot(att_p.astype(BF16), v_ref[pl.ds(p_start, blk), sl]))
        carry_ref[h] = w_d[h][:, 0:1] + w_p[h][:, 0:1]

    def more(t):
        return jnp.logical_and(t < i, jnp.max(carry_ref[...]) > SB_DEAD_LOG)

    def body(t):
        key_block(pl.multiple_of((i - 1 - t) * blk, blk))
        return t + 1

    lax.while_loop(more, body, 1)
    o_ref[...] = acc_ref[...].astype(o_ref.dtype)


def _sb_attention(q, k, v, bsz, seq, blk):
    n = bsz * seq
    nq = seq // blk
    qspec = pl.BlockSpec((blk, SB_WIDTH), lambda b, i: (b * nq + i, 0))
    kvspec = pl.BlockSpec((seq, SB_WIDTH), lambda b, i: (b, 0))
    return pl.pallas_call(
        functools.partial(_sb_kernel, blk=blk),
        grid=(bsz, nq),
        in_specs=[qspec, kvspec, kvspec],
        out_specs=qspec,
        out_shape=jax.ShapeDtypeStruct((n, SB_WIDTH), BF16),
        scratch_shapes=[pltpu.VMEM((blk, SB_WIDTH), F32), pltpu.VMEM((SB_HEADS, blk, 1), F32)],
        compiler_params=pltpu.CompilerParams(dimension_semantics=("arbitrary", "arbitrary"),
                                             vmem_limit_bytes=VMEM_LIMIT),
        name="stick_breaking",
    )(q, k, v)


def _hgrn2_tile(u_ref, lbraw_ref, ng_ref, o_ref, rows, st_ref, layer):
    u = u_ref[rows]
    tile = u.shape[0]
    raw = lbraw_ref[...]
    e = jnp.exp(raw - jnp.max(raw, axis=0, keepdims=True))
    sm = e / jnp.sum(e, axis=0, keepdims=True)
    lb = jnp.zeros((1, HG_WIDTH), F32)
    for l in range(1, layer + 1):
        lb = lb + sm[l:l + 1]

    q = _sigmoid(u[:, :HG_WIDTH])
    f_raw = u[:, HG_WIDTH:2 * HG_WIDTH]
    val = u[:, 2 * HG_WIDTH:3 * HG_WIDTH]
    out_gate = u[:, 3 * HG_WIDTH:]

    half_tanh = 0.5 * jnp.tanh(0.5 * f_raw)
    sig_pos = 0.5 + half_tanh
    sig_neg = 0.5 - half_tanh
    log_sig = jnp.minimum(f_raw, 0.0) - jnp.log(1.0 + jnp.exp(-jnp.abs(f_raw)))
    if layer == 0:
        log_f = log_sig
    else:
        log_f = jnp.where(lb > 0.0, jnp.log(lb + (1.0 - lb) * sig_pos), log_sig)
    kg = (1.0 - lb) * sig_neg
    yield

    pos = lax.broadcasted_iota(jnp.int32, (tile, HG_WIDTH), 0)
    ti = lax.broadcasted_iota(jnp.int32, (tile, tile), 0)
    si = lax.broadcasted_iota(jnp.int32, (tile, tile), 1)
    prefix = jnp.where(si <= ti, 1.0, 0.0).astype(BF16)
    g = sum(_dot(prefix, piece) for piece in _split3(log_f))

    heads = [slice(h * HG_DIM, (h + 1) * HG_DIM) for h in range(HG_HEADS)]
    qb = q.astype(BF16)
    kgb = kg.astype(BF16)
    att = [jnp.where(ti == si, _dot_nt(qb[:, sl], kgb[:, sl]), 0.0) for sl in heads]
    last = g
    w = 1
    while w < tile:
        upper = (pos & (2 * w - 1)) >= w
        g_ref = jnp.where(upper, pltpu.roll(last, w, 0), last)
        last = jnp.where(upper, last, pltpu.roll(last, tile - w, 0))
        decay = jnp.exp2(jnp.abs(g - g_ref) * -LOG2_E)
        z = (jnp.where(upper, q, kg) * decay).astype(BF16)
        pair = ((ti & -(2 * w)) == (si & -(2 * w))) & ((ti & w) != 0) & ((si & w) == 0)
        att = [a + jnp.where(pair, _dot_nt(z[:, sl], z[:, sl]), 0.0) for a, sl in zip(att, heads)]
        yield
        w *= 2

    valb = val.astype(BF16)
    qg = (q * jnp.exp(g)).astype(BF16)
    g_last = g[tile - 1:tile]
    kd = (kg * jnp.exp(g_last - g)).astype(BF16)
    gam = jnp.exp(g_last)
    ng = ng_ref[...]
    yield
    for h, sl in enumerate(heads):
        st = st_ref[h]
        o = _dot(att[h].astype(BF16), valb[:, sl]) + _dot_nt(qg[:, sl], st.astype(BF16))
        o = o * lax.rsqrt(jnp.mean(o * o, axis=-1, keepdims=True) + HG_EPS) * ng[:, sl]
        gate = out_gate[:, sl]
        o_ref[rows, sl] = (o * (gate * _sigmoid(gate))).astype(o_ref.dtype)
        st_ref[h] = st * gam[:, sl] + _dot_tn(valb[:, sl], kd[:, sl])


def _quad_mask(rows):
    r = lax.broadcasted_iota(jnp.int32, (rows, RW_QUAD), 0)
    c = lax.broadcasted_iota(jnp.int32, (rows, RW_QUAD), 1)
    return r, c


def _block_diag(y, head_ones):
    return jnp.concatenate([y, y, y, y], axis=0) * head_ones


def _rwkv7_tile(u, mu_ref, w0_ref, wup_ref, a0_ref, aup_ref, gup_ref, kk_ref, ka_ref,
                rk_ref, gng_ref, gnb_ref, o_ref, prev_ref, st_ref, fill):
    C = RW_CHUNK
    W = RW_WIDTH
    T = u.shape[0]
    trow = lax.broadcasted_iota(jnp.int32, u.shape, 0)
    shifted = jnp.where(trow == 0, prev_ref[...], pltpu.roll(u, 1, 0))
    prev_ref[...] = u[T - 1:T]
    u = u + (shifted - u) * mu_ref[...]
    fill(4)

    br, bc = _quad_mask(RW_QUAD)
    same_head = (br // RW_HEAD_DIM) == (bc // RW_HEAD_DIM)
    head_ones = jnp.where(same_head, 1.0, 0.0).astype(BF16)
    eye_bd = br == bc
    tr, tc = _quad_mask(C)
    s_idx = tc % RW_HEAD_DIM
    strict = s_idx < tr
    incl = s_idx <= tr
    eye_q = jnp.where(s_idx == tr, 1.0, 0.0)
    n_quads = W // RW_QUAD

    def bd(y):
        return _block_diag(y.astype(BF16), head_ones)

    def head_sum(x):
        n = x.shape[0]
        pieces = [p[:, sl] for p in _split2(x) for sl in (slice(0, RW_QUAD), slice(RW_QUAD, W))]
        s = _dot(jnp.concatenate(pieces, axis=0), head_ones)
        halves = [s[j * n:(j + 1) * n] + s[(j + 2) * n:(j + 3) * n] for j in range(2)]
        return jnp.concatenate(halves, axis=1)

    def prepare(rows, out):
        ur = u[rows]
        n = ur.shape[0]
        r = ur[:, :W]
        k = ur[:, W:2 * W]
        v = ur[:, 2 * W:3 * W]
        o0 = 3 * W
        xw = ur[:, o0:o0 + RW_LORA_PAD[0]]
        xa = ur[:, o0 + RW_LORA_PAD[0]:o0 + RW_LORA_PAD[0] + RW_LORA_PAD[1]]
        xg = ur[:, o0 + RW_LORA_PAD[0] + RW_LORA_PAD[1]:]
        w_log = -_softplus(-(w0_ref[...] + _dot(jnp.tanh(xw).astype(BF16), wup_ref[...]))) - 0.5
        log_w = -jnp.exp(w_log)
        a = _sigmoid(a0_ref[...] + _dot(xa.astype(BF16), aup_ref[...]))
        gate = _dot(_sigmoid(xg).astype(BF16), gup_ref[...])
        yield
        kk = k * kk_ref[...]
        kk = kk * jnp.minimum(lax.rsqrt(head_sum(kk * kk)), 1e12)
        k = k * (1.0 + (a - 1.0) * ka_ref[...])
        bonus = head_sum(r * k * rk_ref[...]) * v
        a_vec = -kk
        b_vec = kk * a
        yield
        tt = lax.broadcasted_iota(jnp.int32, (n, n), 0)
        ts = lax.broadcasted_iota(jnp.int32, (n, n), 1)
        prefix = jnp.where((tt // C == ts // C) & (ts <= tt), 1.0, 0.0).astype(BF16)
        g = sum(_dot(prefix, piece) for piece in _split3(log_w))
        yield
        g_last = jnp.concatenate(
            [jnp.broadcast_to(g[c * C + C - 1:c * C + C], (C, W)) for c in range(n // C)], axis=0)
        inv = jnp.exp(-g)
        dec = jnp.exp(g_last - g)
        out.update(a_dec=(a_vec * jnp.exp(g - log_w)).astype(BF16), r_dec=r * jnp.exp(g),
                   b_inv=(b_vec * inv).astype(BF16), k_inv=(k * inv).astype(BF16),
                   b_dec=(b_vec * dec).astype(BF16), k_dec=(k * dec).astype(BF16),
                   v=v.astype(BF16), e_last=jnp.exp(g_last), bonus=bonus, gate=gate)

    def process(rows, p, fill):
        n = p["gate"].shape[0]
        items = [(slice(c * C, (c + 1) * C), slice(qd * RW_QUAD, (qd + 1) * RW_QUAD))
                 for c in range(n // C) for qd in range(n_quads)]
        cut = [{key: p[key][rs, ls] for key in ("a_dec", "r_dec", "v", "b_dec")} for rs, ls in items]

        for it, (rs, ls) in zip(cut, items):
            ar = jnp.concatenate([it["a_dec"], it["r_dec"].astype(BF16)], axis=0)
            lm_b = _dot_nt(ar, bd(p["b_inv"][rs, ls]))
            lm_k = _dot_nt(ar, bd(p["k_inv"][rs, ls]))
            it["l_ab"] = jnp.where(strict, lm_b[:C], 0.0)
            it["m_rb"] = jnp.where(incl, lm_b[C:], 0.0).astype(BF16)
            it["m_rk"] = jnp.where(incl, lm_k[C:], 0.0).astype(BF16)
            it["v_bd"] = bd(it["v"])
            it["lv"] = _dot(jnp.where(strict, lm_k[:C], 0.0).astype(BF16), it["v_bd"])
            it["inv_l"] = eye_q + it["l_ab"]
            it["lp"] = it["l_ab"]
            fill()

        for level in range(5):
            for it in cut:
                lp_bd = bd(it["lp"])
                if level == 0:
                    it["lp"] = _dot(it["lp"].astype(BF16), lp_bd)
                else:
                    both = jnp.concatenate([it["inv_l"], it["lp"]], axis=0).astype(BF16)
                    both = _dot(both, lp_bd)
                    it["inv_l"] = it["inv_l"] + both[:C]
                    it["lp"] = both[C:]
            fill(3)
        for it in cut:
            it["inv_l"] = it["inv_l"] + _dot(it["inv_l"].astype(BF16), bd(it["lp"]))
        fill(3)

        for it, (rs, ls) in zip(cut, items):
            inv_b = it["inv_l"].astype(BF16)
            w1 = _dot(inv_b, bd(it["a_dec"]))
            u0 = _dot(inv_b, bd(it["lv"]))
            it["q_eff"] = (it["r_dec"] + _dot(it["m_rb"], bd(w1))).astype(BF16)
            it["y_intra"] = _dot(it["m_rb"], bd(u0)) + _dot(it["m_rk"], it["v_bd"])
            it["pt_bd"] = (jnp.where(same_head, _dot_tn(it["b_dec"], w1.astype(BF16)), 0.0)
                           + jnp.where(eye_bd, p["e_last"][rs, ls][:1], 0.0)).astype(BF16)
            keys = jnp.concatenate([it["b_dec"], p["k_dec"][rs, ls]], axis=0)
            vals = jnp.concatenate([u0.astype(BF16), it["v"]], axis=0)
            it["dt_bd"] = jnp.where(same_head, _dot_tn(keys, vals), 0.0)
            fill()

        rows_out = []
        for c in range(n // C):
            ys = []
            for qd in range(n_quads):
                it = cut[c * n_quads + qd]
                st = st_ref[qd].astype(BF16)
                ys.append(it["y_intra"] + _dot(it["q_eff"], st))
                st_ref[qd] = _dot(it["pt_bd"], st) + it["dt_bd"]
            rows_out.append(jnp.concatenate(ys, axis=1))
            fill()
        y = jnp.concatenate(rows_out, axis=0)
        mean = head_sum(y) * (1.0 / RW_HEAD_DIM)
        yc = y - mean
        var = head_sum(yc * yc) * (1.0 / RW_HEAD_DIM)
        y = yc * lax.rsqrt(var + RW_GN_EPS) * gng_ref[...] + gnb_ref[...]
        o_ref[rows] = ((y + p["bonus"]) * p["gate"]).astype(o_ref.dtype)

    whole = slice(0, T)
    prepared = {}
    for _ in prepare(whole, prepared):
        fill(4)
    fill(4)
    process(whole, prepared, fill)


def _proj_mix_kernel(h_ref, w_ref,
                     mu_ref, w0_ref, wup_ref, a0_ref, aup_ref, gup_ref, kk_ref, ka_ref, rk_ref,
                     gng_ref, gnb_ref, lbraw_ref, ng_ref,
                     q_ref, k_ref, v_ref, gate_ref, yrw_ref, yhg_ref,
                     prev_ref, rwst_ref, hgst_ref, urw_ref, uhg_ref,
                     *, layer, tiles_per_seq, hg_tile):
    i = pl.program_id(0)

    @pl.when(i == 0)
    def _():
        urw_ref[...] = jnp.zeros_like(urw_ref)
        uhg_ref[...] = jnp.zeros_like(uhg_ref)

    @pl.when(jnp.maximum(i - 1, 0) % tiles_per_seq == 0)
    def _():
        prev_ref[...] = jnp.zeros_like(prev_ref)
        rwst_ref[...] = jnp.zeros_like(rwst_ref)
        hgst_ref[...] = jnp.zeros_like(hgst_ref)

    hb = h_ref[...].astype(BF16)

    def piece(base, c, store):
        cols = slice(c * MXU_COLS, (c + 1) * MXU_COLS)
        store(cols, _dot(hb, w_ref[:, base + cols.start:base + cols.stop]))

    def store_rw(cols, y):
        urw_ref[:, cols] = y

    def store_hg(cols, y):
        uhg_ref[:, cols] = y

    def store_gate(cols, y):
        gate_ref[:, cols] = _sigmoid(y).astype(BF16)

    def store_sb(cols, y):
        o_ref = (q_ref, k_ref, v_ref)[cols.start // SB_WIDTH]
        local = slice(cols.start % SB_WIDTH, cols.start % SB_WIDTH + MXU_COLS)
        o_ref[:, local] = (y * (SB_HEAD_DIM ** -0.5) if o_ref is q_ref else y).astype(BF16)

    def pieces_of(group, store):
        base, width = PROJ_BASES[group], PROJ_WIDTHS[group]
        return [functools.partial(piece, base, c, store) for c in range(width // MXU_COLS)]

    def hgrn2_stages():
        tiles = []
        for j in range(hb.shape[0] // hg_tile):
            rows = slice(j * hg_tile, (j + 1) * hg_tile)
            tiles.append(_hgrn2_tile(uhg_ref, lbraw_ref, ng_ref, yhg_ref, rows, hgst_ref, layer))
            next(tiles[-1])
            yield
        for t in tiles:
            yield from t
            yield

    hgrn2 = hgrn2_stages()
    hgrn2_step = functools.partial(next, hgrn2, None)
    for _ in range(hb.shape[0] // hg_tile):
        hgrn2_step()
    queue = pieces_of("hg", store_hg) + pieces_of("rw", store_rw)
    for p in pieces_of("sb", store_sb) + pieces_of("gate", store_gate):
        queue += [hgrn2_step, p]

    def fill(n=1):
        for _ in range(n):
            if queue:
                queue.pop(0)()

    _rwkv7_tile(urw_ref[...], mu_ref, w0_ref, wup_ref, a0_ref, aup_ref, gup_ref, kk_ref,
                ka_ref, rk_ref, gng_ref, gnb_ref, yrw_ref, prev_ref, rwst_ref, fill)
    fill(len(queue))
    for _ in hgrn2:
        pass


def _proj_mix(h, w_proj, rw_params, lb_raw, norm_g, layer, seq, tm, hg_tile):
    n = h.shape[0]
    nt = n // tm

    def projected(c):
        return pl.BlockSpec((tm, c), lambda i: (jnp.minimum(i, nt - 1), 0))

    def mixed(c):
        return pl.BlockSpec((tm, c), lambda i: (jnp.maximum(i - 1, 0), 0))

    assert w_proj.shape[1:] == (D_MODEL, PROJ_COLS)
    params = tuple(rw_params) + (lb_raw, norm_g)
    return pl.pallas_call(
        functools.partial(_proj_mix_kernel, layer=layer, tiles_per_seq=seq // tm, hg_tile=hg_tile),
        grid=(nt + 1,),
        in_specs=[projected(D_MODEL), _resident_layer(w_proj, layer)]
                 + [_resident(a.shape) for a in params],
        out_specs=[projected(SB_WIDTH), projected(SB_WIDTH), projected(SB_WIDTH),
                   projected(3 * D_MODEL), mixed(RW_WIDTH), mixed(HG_WIDTH)],
        out_shape=[jax.ShapeDtypeStruct((n, SB_WIDTH), BF16),
                   jax.ShapeDtypeStruct((n, SB_WIDTH), BF16),
                   jax.ShapeDtypeStruct((n, SB_WIDTH), BF16),
                   jax.ShapeDtypeStruct((n, 3 * D_MODEL), BF16),
                   jax.ShapeDtypeStruct((n, RW_WIDTH), BF16),
                   jax.ShapeDtypeStruct((n, HG_WIDTH), BF16)],
        scratch_shapes=[pltpu.VMEM((1, RW_U_COLS), F32),
                        pltpu.VMEM((RW_WIDTH // RW_QUAD, RW_QUAD, RW_QUAD), F32),
                        pltpu.VMEM((HG_HEADS, HG_DIM, HG_DIM), F32),
                        pltpu.VMEM((tm, RW_U_COLS), F32),
                        pltpu.VMEM((tm, 4 * HG_WIDTH), F32)],
        compiler_params=pltpu.CompilerParams(dimension_semantics=("arbitrary",),
                                             vmem_limit_bytes=VMEM_LIMIT),
        name="proj_mix",
    )(h, w_proj, *params)


def _pad_axis(w, axis, size):
    pads = [(0, 0)] * w.ndim
    pads[axis] = (0, size - w.shape[axis])
    return jnp.pad(w, pads)


def _split_rw_cols(w):
    o = 3 * RW_WIDTH
    parts = [w[..., :o]]
    for size, pad in zip((RW_DECAY_LORA, RW_AAA_LORA, RW_GATE_LORA), RW_LORA_PAD):
        parts.append(_pad_axis(w[..., o:o + size], -1, pad))
        o += size
    return jnp.concatenate(parts, axis=-1)


def kernel(x, p, ln_g, ln_b, ffn1_wg, ffn1_wu, ffn1_wd, w_in, rw_mu, rw_w0, rw_w_up, rw_a0, rw_a_up, rw_g_up, rw_k_k, rw_k_a, rw_r_k, rw_gn_g, rw_gn_b, hg_lb_raw, hg_norm_g, w_br_rw, w_br_sb, w_br_hg, w_out, ffn2_wg, ffn2_wu, ffn2_wd, ple_gate, ple_proj):
    bsz, seq, _ = x.shape
    n = bsz * seq
    tm = min(256, seq)
    sb_blk = min(256, seq)
    hg_tile = min(128, seq)
    rw_cols = 3 * RW_WIDTH + RW_DECAY_LORA + RW_AAA_LORA + RW_GATE_LORA

    ffn1 = [w.astype(BF16) for w in (ffn1_wg, ffn1_wu, ffn1_wd)]
    w_proj = jnp.concatenate([_split_rw_cols(w_in[..., :rw_cols]), w_in[..., rw_cols:]], axis=-1).astype(BF16)
    loras = [_pad_axis(w, 1, pad).astype(BF16)
             for w, pad in zip((rw_w_up, rw_a_up, rw_g_up), RW_LORA_PAD)]
    rw_mu_cols = _split_rw_cols(rw_mu)
    tail = [w.astype(BF16) for w in (w_br_rw, w_br_sb, w_br_hg, w_out, ffn2_wg, ffn2_wu, ffn2_wd,
                                     ple_gate, ple_proj)]

    h = x.reshape(n, D_MODEL)
    for i in range(DEPTH):
        def vec(a):
            return a[i].reshape(1, -1)

        h = _ffn_ln(h, *ffn1, i, vec(ln_g[:, 0]), vec(ln_b[:, 0]), min(512, n))
        rw_params = (vec(rw_mu_cols), vec(rw_w0), loras[0][i], vec(rw_a0), loras[1][i], loras[2][i],
                     vec(rw_k_k), vec(rw_k_a), vec(rw_r_k), vec(rw_gn_g), vec(rw_gn_b))
        q, k, v, gate, y_rw, y_hg = _proj_mix(h, w_proj, rw_params, hg_lb_raw, vec(hg_norm_g),
                                              i, seq, tm, hg_tile)
        y_sb = _sb_attention(q, k, v, bsz, seq, sb_blk)
        h = _layer_tail(h, y_rw, y_sb, y_hg, gate, p[i].reshape(n, PLE_DIM), tail, i,
                        ln_g[i, 1:], ln_b[i, 1:], min(512, n))
    return h.reshape(bsz, seq, D_MODEL)
```

```python
import functools

import jax
import jax.numpy as jnp
from jax import lax
from jax.experimental import pallas as pl
from jax.experimental.pallas import tpu as pltpu

F32 = jnp.float32
BF16 = jnp.bfloat16

D_MODEL = 1024
DEPTH = 4
PLE_DIM = 256
D_FF = 2816
RW_HEADS = 8
RW_HEAD_DIM = 64
RW_WIDTH = RW_HEADS * RW_HEAD_DIM
RW_CHUNK = 64
RW_DECAY_LORA = 64
RW_AAA_LORA = 64
RW_GATE_LORA = 160
RW_GN_EPS = 64e-5
SB_HEADS = 4
SB_HEAD_DIM = 128
SB_WIDTH = SB_HEADS * SB_HEAD_DIM
SB_DEAD_LOG = -105.0
HG_HEADS = 4
HG_DIM = 128
HG_WIDTH = HG_HEADS * HG_DIM
HG_EPS = 1e-5
LOG2_E = 1.4426950408889634
LN_EPS = 1e-5
ALPHA = (2 * DEPTH) ** 0.25

MXU_COLS = 256
RW_QUAD = MXU_COLS
FFN_COL_GROUP = 6 * MXU_COLS
RW_LORA_PAD = (128, 128, 256)
RW_U_COLS = 3 * RW_WIDTH + sum(RW_LORA_PAD)
PROJ_WIDTHS = {"rw": RW_U_COLS, "sb": 3 * SB_WIDTH, "hg": 4 * HG_WIDTH, "gate": 3 * D_MODEL}
PROJ_BASES = {"rw": 0, "sb": RW_U_COLS, "hg": RW_U_COLS + 3 * SB_WIDTH,
              "gate": RW_U_COLS + 3 * SB_WIDTH + 4 * HG_WIDTH}
PROJ_COLS = sum(PROJ_WIDTHS.values())
VMEM_LIMIT = 56 * 1024 * 1024


def _resident(shape):
    return pl.BlockSpec(shape, lambda *_: (0,) * len(shape), pipeline_mode=pl.Buffered(1))


def _resident_layer(stacked, layer):
    rest = stacked.shape[1:]
    return pl.BlockSpec((None,) + rest, lambda *_: (layer,) + (0,) * len(rest),
                        pipeline_mode=pl.Buffered(1))


def _dot(a, b):
    return jnp.dot(a, b, preferred_element_type=F32)


def _dot_nt(a, b):
    return lax.dot_general(a, b, (((1,), (1,)), ((), ())), preferred_element_type=F32)


def _dot_tn(a, b):
    return lax.dot_general(a, b, (((0,), (0,)), ((), ())), preferred_element_type=F32)


def _split2(x):
    hi = x.astype(BF16)
    lo = (x - hi.astype(F32)).astype(BF16)
    return hi, lo


def _split3(x):
    hi = x.astype(BF16)
    r = x - hi.astype(F32)
    mid = r.astype(BF16)
    lo = (r - mid.astype(F32)).astype(BF16)
    return hi, mid, lo


def _sigmoid(x):
    return 0.5 * jnp.tanh(0.5 * x) + 0.5


def _softplus(x):
    return jnp.maximum(x, 0.0) + jnp.log(1.0 + jnp.exp(-jnp.abs(x)))


def _layer_norm(x, g, b):
    mu = jnp.mean(x, axis=-1, keepdims=True)
    xc = x - mu
    var = jnp.mean(xc * xc, axis=-1, keepdims=True)
    return xc * lax.rsqrt(var + LN_EPS) * g + b


def _swiglu(hb, wg_ref, wu_ref, wd_ref, out):
    y = None
    for c0 in range(0, D_FF, FFN_COL_GROUP):
        cols = slice(c0, min(c0 + FFN_COL_GROUP, D_FF))
        g = _dot(hb, wg_ref[:, cols])
        u = _dot(hb, wu_ref[:, cols])
        yield
        part = _dot((g * _sigmoid(g) * u).astype(BF16), wd_ref[cols, :])
        y = part if y is None else y + part
        yield
    out.append(y)


def _round_robin(generators):
    live = list(generators)
    while live:
        for gen in list(live):
            if next(gen, StopIteration) is StopIteration:
                live.remove(gen)


def _row_groups(rows):
    half = rows // 2 if rows % 16 == 0 else rows
    return [slice(r, r + half) for r in range(0, rows, half)]


def _ffn_ln_kernel(h_ref, wg_ref, wu_ref, wd_ref, lng_ref, lnb_ref, o_ref):
    def rows_stage(rows):
        h = h_ref[rows]
        y = []
        yield from _swiglu(h.astype(BF16), wg_ref, wu_ref, wd_ref, y)
        o_ref[rows] = _layer_norm(ALPHA * h + 0.5 * y[0], lng_ref[...], lnb_ref[...])

    _round_robin(rows_stage(rows) for rows in _row_groups(h_ref.shape[0]))


def _ffn_ln(h, wg, wu, wd, layer, ln_g, ln_b, tm):
    n = h.shape[0]
    row = pl.BlockSpec((tm, D_MODEL), lambda i: (i, 0))
    return pl.pallas_call(
        _ffn_ln_kernel,
        grid=(n // tm,),
        in_specs=[row] + [_resident_layer(w, layer) for w in (wg, wu, wd)]
                 + [_resident((1, D_MODEL)), _resident((1, D_MODEL))],
        out_specs=row,
        out_shape=jax.ShapeDtypeStruct((n, D_MODEL), F32),
        compiler_params=pltpu.CompilerParams(dimension_semantics=("arbitrary",),
                                             vmem_limit_bytes=VMEM_LIMIT),
        name="ffn_ln",
    )(h, wg, wu, wd, ln_g, ln_b)


def _layer_tail_kernel(h_ref, yrw_ref, ysb_ref, yhg_ref, gate_ref, p_ref,
                       wrw_ref, wsb_ref, whg_ref, wout_ref, wg_ref, wu_ref, wd_ref, pg_ref, pp_ref,
                       lng_ref, lnb_ref, o_ref):
    def rows_stage(rows):
        branches = [_dot(y_ref[rows], w_ref[...]) for y_ref, w_ref in
                    ((yrw_ref, wrw_ref), (ysb_ref, wsb_ref), (yhg_ref, whg_ref))]
        yield
        gate = gate_ref[rows]
        merged = sum(gate[:, j * D_MODEL:(j + 1) * D_MODEL] * b for j, b in enumerate(branches))
        mix = _dot(merged.astype(BF16), wout_ref[...])
        yield
        h = _layer_norm(ALPHA * h_ref[rows] + mix, lng_ref[0:1], lnb_ref[0:1])
        y = []
        yield from _swiglu(h.astype(BF16), wg_ref, wu_ref, wd_ref, y)
        h = _layer_norm(ALPHA * h + 0.5 * y[0], lng_ref[1:2], lnb_ref[1:2])
        ple_gate = _dot(h.astype(BF16), pg_ref[...])
        ple_in = _dot(p_ref[rows].astype(BF16), pp_ref[...])
        yield
        o_ref[rows] = _layer_norm(ALPHA * h + _sigmoid(ple_gate) * ple_in, lng_ref[2:3], lnb_ref[2:3])

    _round_robin(rows_stage(rows) for rows in _row_groups(h_ref.shape[0]))


def _layer_tail(h, y_rw, y_sb, y_hg, gate, p, weights, layer, ln_g, ln_b, tm):
    n = h.shape[0]

    def row(c):
        return pl.BlockSpec((tm, c), lambda i: (i, 0))

    return pl.pallas_call(
        _layer_tail_kernel,
        grid=(n // tm,),
        in_specs=[row(D_MODEL), row(RW_WIDTH), row(SB_WIDTH), row(HG_WIDTH), row(3 * D_MODEL),
                  row(PLE_DIM)] + [_resident_layer(w, layer) for w in weights]
                 + [_resident(ln_g.shape), _resident(ln_b.shape)],
        out_specs=row(D_MODEL),
        out_shape=jax.ShapeDtypeStruct((n, D_MODEL), F32),
        compiler_params=pltpu.CompilerParams(dimension_semantics=("arbitrary",),
                                             vmem_limit_bytes=VMEM_LIMIT),
        name="layer_tail",
    )(h, y_rw, y_sb, y_hg, gate, p, *weights, ln_g, ln_b)


def _sb_kernel(q_ref, k_ref, v_ref, o_ref, acc_ref, carry_ref, *, blk):
    i = pl.program_id(1)
    row = lax.broadcasted_iota(jnp.int32, (blk, blk), 0)
    col = lax.broadcasted_iota(jnp.int32, (blk, blk), 1)
    suffix = jnp.where(row >= col, 1.0, 0.0).astype(BF16)
    strict = col < row

    lanes = [slice(h * SB_HEAD_DIM, (h + 1) * SB_HEAD_DIM) for h in range(SB_HEADS)]

    def neg_softplus(z):
        return -(jnp.maximum(z, 0.0) + jnp.log(1.0 + jnp.exp(-jnp.abs(z))))

    def key_block(start):
        zs = [_dot_nt(q_ref[:, sl], k_ref[pl.ds(start, blk), sl]) for sl in lanes]
        withins = [_dot(neg_softplus(z).astype(BF16), suffix) for z in zs]
        atts = [jnp.exp(z + w + carry_ref[h]) for h, (z, w) in enumerate(zip(zs, withins))]
        for h, sl in enumerate(lanes):
            acc_ref[:, sl] += _dot(atts[h].astype(BF16), v_ref[pl.ds(start, blk), sl])
            carry_ref[h] += withins[h][:, 0:1]

    has_prev = i > 0
    d_start = pl.multiple_of(i * blk, blk)
    p_start = pl.multiple_of(jnp.maximum(i - 1, 0) * blk, blk)
    z_d = [_dot_nt(q_ref[:, sl], k_ref[pl.ds(d_start, blk), sl]) for sl in lanes]
    z_p = [_dot_nt(q_ref[:, sl], k_ref[pl.ds(p_start, blk), sl]) for sl in lanes]
    m_d = [jnp.where(strict, neg_softplus(z), 0.0) for z in z_d]
    m_p = [jnp.where(has_prev, neg_softplus(z), 0.0) for z in z_p]
    w_d = [_dot(m.astype(BF16), suffix) for m in m_d]
    w_p = [_dot(m.astype(BF16), suffix) for m in m_p]
    for h, sl in enumerate(lanes):
        att_d = jnp.where(strict, jnp.exp(z_d[h] + w_d[h]), 0.0)
        att_p = jnp.where(has_prev, jnp.exp(z_p[h] + w_p[h] + w_d[h][:, 0:1]), 0.0)
        acc_ref[:, sl] = (_dot(att_d.astype(BF16), v_ref[pl.ds(d_start, blk), sl])
                          + _dot(att_p.astype(BF16), v_ref[pl.ds(p_start, blk), sl]))
        carry_ref[h] = w_d[h][:, 0:1] + w_p[h][:, 0:1]

    def more(t):
        return jnp.logical_and(t < i, jnp.max(carry_ref[...]) > SB_DEAD_LOG)

    def body(t):
        key_block(pl.multiple_of((i - 1 - t) * blk, blk))
        return t + 1

    lax.while_loop(more, body, 1)
    o_ref[...] = acc_ref[...].astype(o_ref.dtype)


def _sb_attention(q, k, v, bsz, seq, blk):
    n = bsz * seq
    nq = seq // blk
    qspec = pl.BlockSpec((blk, SB_WIDTH), lambda b, i: (b * nq + i, 0))
    kvspec = pl.BlockSpec((seq, SB_WIDTH), lambda b, i: (b, 0))
    return pl.pallas_call(
        functools.partial(_sb_kernel, blk=blk),
        grid=(bsz, nq),
        in_specs=[qspec, kvspec, kvspec],
        out_specs=qspec,
        out_shape=jax.ShapeDtypeStruct((n, SB_WIDTH), BF16),
        scratch_shapes=[pltpu.VMEM((blk, SB_WIDTH), F32), pltpu.VMEM((SB_HEADS, blk, 1), F32)],
        compiler_params=pltpu.CompilerParams(dimension_semantics=("arbitrary", "arbitrary"),
                                             vmem_limit_bytes=VMEM_LIMIT),
        name="stick_breaking",
    )(q, k, v)


def _hgrn2_tile(u_ref, lbraw_ref, ng_ref, o_ref, rows, st_ref, layer):
    u = u_ref[rows]
    tile = u.shape[0]
    raw = lbraw_ref[...]
    e = jnp.exp(raw - jnp.max(raw, axis=0, keepdims=True))
    sm = e / jnp.sum(e, axis=0, keepdims=True)
    lb = jnp.zeros((1, HG_WIDTH), F32)
    for l in range(1, layer + 1):
        lb = lb + sm[l:l + 1]

    q = _sigmoid(u[:, :HG_WIDTH])
    f_raw = u[:, HG_WIDTH:2 * HG_WIDTH]
    val = u[:, 2 * HG_WIDTH:3 * HG_WIDTH]
    out_gate = u[:, 3 * HG_WIDTH:]

    half_tanh = 0.5 * jnp.tanh(0.5 * f_raw)
    sig_pos = 0.5 + half_tanh
    sig_neg = 0.5 - half_tanh
    log_sig = jnp.minimum(f_raw, 0.0) - jnp.log(1.0 + jnp.exp(-jnp.abs(f_raw)))
    if layer == 0:
        log_f = log_sig
    else:
        log_f = jnp.where(lb > 0.0, jnp.log(lb + (1.0 - lb) * sig_pos), log_sig)
    kg = (1.0 - lb) * sig_neg
    yield

    pos = lax.broadcasted_iota(jnp.int32, (tile, HG_WIDTH), 0)
    ti = lax.broadcasted_iota(jnp.int32, (tile, tile), 0)
    si = lax.broadcasted_iota(jnp.int32, (tile, tile), 1)
    prefix = jnp.where(si <= ti, 1.0, 0.0).astype(BF16)
    g = sum(_dot(prefix, piece) for piece in _split3(log_f))

    heads = [slice(h * HG_DIM, (h + 1) * HG_DIM) for h in range(HG_HEADS)]
    qb = q.astype(BF16)
    kgb = kg.astype(BF16)
    att = [jnp.where(ti == si, _dot_nt(qb[:, sl], kgb[:, sl]), 0.0) for sl in heads]
    last = g
    w = 1
    while w < tile:
        upper = (pos & (2 * w - 1)) >= w
        g_ref = jnp.where(upper, pltpu.roll(last, w, 0), last)
        last = jnp.where(upper, last, pltpu.roll(last, tile - w, 0))
        decay = jnp.exp2(jnp.abs(g - g_ref) * -LOG2_E)
        z = (jnp.where(upper, q, kg) * decay).astype(BF16)
        pair = ((ti & -(2 * w)) == (si & -(2 * w))) & ((ti & w) != 0) & ((si & w) == 0)
        att = [a + jnp.where(pair, _dot_nt(z[:, sl], z[:, sl]), 0.0) for a, sl in zip(att, heads)]
        yield
        w *= 2

    valb = val.astype(BF16)
    qg = (q * jnp.exp(g)).astype(BF16)
    g_last = g[tile - 1:tile]
    kd = (kg * jnp.exp(g_last - g)).astype(BF16)
    gam = jnp.exp(g_last)
    ng = ng_ref[...]
    yield
    for h, sl in enumerate(heads):
        st = st_ref[h]
        o = _dot(att[h].astype(BF16), valb[:, sl]) + _dot_nt(qg[:, sl], st.astype(BF16))
        o = o * lax.rsqrt(jnp.mean(o * o, axis=-1, keepdims=True) + HG_EPS) * ng[:, sl]
        gate = out_gate[:, sl]
        o_ref[rows, sl] = (o * (gate * _sigmoid(gate))).astype(o_ref.dtype)
        st_ref[h] = st * gam[:, sl] + _dot_tn(valb[:, sl], kd[:, sl])


def _quad_mask(rows):
    r = lax.broadcasted_iota(jnp.int32, (rows, RW_QUAD), 0)
    c = lax.broadcasted_iota(jnp.int32, (rows, RW_QUAD), 1)
    return r, c


def _block_diag(y, head_ones):
    return jnp.concatenate([y, y, y, y], axis=0) * head_ones


def _rwkv7_tile(u, mu_ref, w0_ref, wup_ref, a0_ref, aup_ref, gup_ref, kk_ref, ka_ref,
                rk_ref, gng_ref, gnb_ref, o_ref, prev_ref, st_ref, fill):
    C = RW_CHUNK
    W = RW_WIDTH
    T = u.shape[0]
    trow = lax.broadcasted_iota(jnp.int32, u.shape, 0)
    shifted = jnp.where(trow == 0, prev_ref[...], pltpu.roll(u, 1, 0))
    prev_ref[...] = u[T - 1:T]
    u = u + (shifted - u) * mu_ref[...]
    fill(4)

    br, bc = _quad_mask(RW_QUAD)
    same_head = (br // RW_HEAD_DIM) == (bc // RW_HEAD_DIM)
    head_ones = jnp.where(same_head, 1.0, 0.0).astype(BF16)
    eye_bd = br == bc
    tr, tc = _quad_mask(C)
    s_idx = tc % RW_HEAD_DIM
    strict = s_idx < tr
    incl = s_idx <= tr
    eye_q = jnp.where(s_idx == tr, 1.0, 0.0)
    n_quads = W // RW_QUAD

    def bd(y):
        return _block_diag(y.astype(BF16), head_ones)

    def head_sum(x):
        n = x.shape[0]
        pieces = [p[:, sl] for p in _split2(x) for sl in (slice(0, RW_QUAD), slice(RW_QUAD, W))]
        s = _dot(jnp.concatenate(pieces, axis=0), head_ones)
        halves = [s[j * n:(j + 1) * n] + s[(j + 2) * n:(j + 3) * n] for j in range(2)]
        return jnp.concatenate(halves, axis=1)

    def prepare(rows, out):
        ur = u[rows]
        n = ur.shape[0]
        r = ur[:, :W]
        k = ur[:, W:2 * W]
        v = ur[:, 2 * W:3 * W]
        o0 = 3 * W
        xw = ur[:, o0:o0 + RW_LORA_PAD[0]]
        xa = ur[:, o0 + RW_LORA_PAD[0]:o0 + RW_LORA_PAD[0] + RW_LORA_PAD[1]]
        xg = ur[:, o0 + RW_LORA_PAD[0] + RW_LORA_PAD[1]:]
        w_log = -_softplus(-(w0_ref[...] + _dot(jnp.tanh(xw).astype(BF16), wup_ref[...]))) - 0.5
        log_w = -jnp.exp(w_log)
        a = _sigmoid(a0_ref[...] + _dot(xa.astype(BF16), aup_ref[...]))
        gate = _dot(_sigmoid(xg).astype(BF16), gup_ref[...])
        yield
        kk = k * kk_ref[...]
        kk = kk * jnp.minimum(lax.rsqrt(head_sum(kk * kk)), 1e12)
        k = k * (1.0 + (a - 1.0) * ka_ref[...])
        bonus = head_sum(r * k * rk_ref[...]) * v
        a_vec = -kk
        b_vec = kk * a
        yield
        tt = lax.broadcasted_iota(jnp.int32, (n, n), 0)
        ts = lax.broadcasted_iota(jnp.int32, (n, n), 1)
        prefix = jnp.where((tt // C == ts // C) & (ts <= tt), 1.0, 0.0).astype(BF16)
        g = sum(_dot(prefix, piece) for piece in _split3(log_w))
        yield
        g_last = jnp.concatenate(
            [jnp.broadcast_to(g[c * C + C - 1:c * C + C], (C, W)) for c in range(n // C)], axis=0)
        inv = jnp.exp(-g)
        dec = jnp.exp(g_last - g)
        out.update(a_dec=(a_vec * jnp.exp(g - log_w)).astype(BF16), r_dec=r * jnp.exp(g),
                   b_inv=(b_vec * inv).astype(BF16), k_inv=(k * inv).astype(BF16),
                   b_dec=(b_vec * dec).astype(BF16), k_dec=(k * dec).astype(BF16),
                   v=v.astype(BF16), e_last=jnp.exp(g_last), bonus=bonus, gate=gate)

    def process(rows, p, fill):
        n = p["gate"].shape[0]
        items = [(slice(c * C, (c + 1) * C), slice(qd * RW_QUAD, (qd + 1) * RW_QUAD))
                 for c in range(n // C) for qd in range(n_quads)]
        cut = [{key: p[key][rs, ls] for key in ("a_dec", "r_dec", "v", "b_dec")} for rs, ls in items]

        for it, (rs, ls) in zip(cut, items):
            ar = jnp.concatenate([it["a_dec"], it["r_dec"].astype(BF16)], axis=0)
            lm_b = _dot_nt(ar, bd(p["b_inv"][rs, ls]))
            lm_k = _dot_nt(ar, bd(p["k_inv"][rs, ls]))
            it["l_ab"] = jnp.where(strict, lm_b[:C], 0.0)
            it["m_rb"] = jnp.where(incl, lm_b[C:], 0.0).astype(BF16)
            it["m_rk"] = jnp.where(incl, lm_k[C:], 0.0).astype(BF16)
            it["v_bd"] = bd(it["v"])
            it["lv"] = _dot(jnp.where(strict, lm_k[:C], 0.0).astype(BF16), it["v_bd"])
            it["inv_l"] = eye_q + it["l_ab"]
            it["lp"] = it["l_ab"]
            fill()

        for level in range(5):
            for it in cut:
                lp_bd = bd(it["lp"])
                if level == 0:
                    it["lp"] = _dot(it["lp"].astype(BF16), lp_bd)
                else:
                    both = jnp.concatenate([it["inv_l"], it["lp"]], axis=0).astype(BF16)
                    both = _dot(both, lp_bd)
                    it["inv_l"] = it["inv_l"] + both[:C]
                    it["lp"] = both[C:]
            fill(3)
        for it in cut:
            it["inv_l"] = it["inv_l"] + _dot(it["inv_l"].astype(BF16), bd(it["lp"]))
        fill(3)

        for it, (rs, ls) in zip(cut, items):
            inv_b = it["inv_l"].astype(BF16)
            w1 = _dot(inv_b, bd(it["a_dec"]))
            u0 = _dot(inv_b, bd(it["lv"]))
            it["q_eff"] = (it["r_dec"] + _dot(it["m_rb"], bd(w1))).astype(BF16)
            it["y_intra"] = _dot(it["m_rb"], bd(u0)) + _dot(it["m_rk"], it["v_bd"])
            it["pt_bd"] = (jnp.where(same_head, _dot_tn(it["b_dec"], w1.astype(BF16)), 0.0)
                           + jnp.where(eye_bd, p["e_last"][rs, ls][:1], 0.0)).astype(BF16)
            keys = jnp.concatenate([it["b_dec"], p["k_dec"][rs, ls]], axis=0)
            vals = jnp.concatenate([u0.astype(BF16), it["v"]], axis=0)
            it["dt_bd"] = jnp.where(same_head, _dot_tn(keys, vals), 0.0)
            fill()

        rows_out = []
        for c in range(n // C):
            ys = []
            for qd in range(n_quads):
                it = cut[c * n_quads + qd]
                st = st_ref[qd].astype(BF16)
                ys.append(it["y_intra"] + _dot(it["q_eff"], st))
                st_ref[qd] = _dot(it["pt_bd"], st) + it["dt_bd"]
            rows_out.append(jnp.concatenate(ys, axis=1))
            fill()
        y = jnp.concatenate(rows_out, axis=0)
        mean = head_sum(y) * (1.0 / RW_HEAD_DIM)
        yc = y - mean
        var = head_sum(yc * yc) * (1.0 / RW_HEAD_DIM)
        y = yc * lax.rsqrt(var + RW_GN_EPS) * gng_ref[...] + gnb_ref[...]
        o_ref[rows] = ((y + p["bonus"]) * p["gate"]).astype(o_ref.dtype)

    whole = slice(0, T)
    prepared = {}
    for _ in prepare(whole, prepared):
        fill(4)
    fill(4)
    process(whole, prepared, fill)


def _proj_mix_kernel(h_ref, w_ref,
                     mu_ref, w0_ref, wup_ref, a0_ref, aup_ref, gup_ref, kk_ref, ka_ref, rk_ref,
                     gng_ref, gnb_ref, lbraw_ref, ng_ref,
                     q_ref, k_ref, v_ref, gate_ref, yrw_ref, yhg_ref,
                     prev_ref, rwst_ref, hgst_ref, urw_ref, uhg_ref,
                     *, layer, tiles_per_seq, hg_tile):
    i = pl.program_id(0)

    @pl.when(i == 0)
    def _():
        urw_ref[...] = jnp.zeros_like(urw_ref)
        uhg_ref[...] = jnp.zeros_like(uhg_ref)

    @pl.when(jnp.maximum(i - 1, 0) % tiles_per_seq == 0)
    def _():
        prev_ref[...] = jnp.zeros_like(prev_ref)
        rwst_ref[...] = jnp.zeros_like(rwst_ref)
        hgst_ref[...] = jnp.zeros_like(hgst_ref)

    hb = h_ref[...].astype(BF16)

    def piece(base, c, store):
        cols = slice(c * MXU_COLS, (c + 1) * MXU_COLS)
        store(cols, _dot(hb, w_ref[:, base + cols.start:base + cols.stop]))

    def store_rw(cols, y):
        urw_ref[:, cols] = y

    def store_hg(cols, y):
        uhg_ref[:, cols] = y

    def store_gate(cols, y):
        gate_ref[:, cols] = _sigmoid(y).astype(BF16)

    def store_sb(cols, y):
        o_ref = (q_ref, k_ref, v_ref)[cols.start // SB_WIDTH]
        local = slice(cols.start % SB_WIDTH, cols.start % SB_WIDTH + MXU_COLS)
        o_ref[:, local] = (y * (SB_HEAD_DIM ** -0.5) if o_ref is q_ref else y).astype(BF16)

    def pieces_of(group, store):
        base, width = PROJ_BASES[group], PROJ_WIDTHS[group]
        return [functools.partial(piece, base, c, store) for c in range(width // MXU_COLS)]

    def hgrn2_stages():
        tiles = []
        for j in range(hb.shape[0] // hg_tile):
            rows = slice(j * hg_tile, (j + 1) * hg_tile)
            tiles.append(_hgrn2_tile(uhg_ref, lbraw_ref, ng_ref, yhg_ref, rows, hgst_ref, layer))
            next(tiles[-1])
            yield
        for t in tiles:
            yield from t
            yield

    hgrn2 = hgrn2_stages()
    hgrn2_step = functools.partial(next, hgrn2, None)
    for _ in range(hb.shape[0] // hg_tile):
        hgrn2_step()
    queue = pieces_of("hg", store_hg) + pieces_of("rw", store_rw)
    for p in pieces_of("sb", store_sb) + pieces_of("gate", store_gate):
        queue += [hgrn2_step, p]

    def fill(n=1):
        for _ in range(n):
            if queue:
                queue.pop(0)()

    _rwkv7_tile(urw_ref[...], mu_ref, w0_ref, wup_ref, a0_ref, aup_ref, gup_ref, kk_ref,
                ka_ref, rk_ref, gng_ref, gnb_ref, yrw_ref, prev_ref, rwst_ref, fill)
    fill(len(queue))
    for _ in hgrn2:
        pass


def _proj_mix(h, w_proj, rw_params, lb_raw, norm_g, layer, seq, tm, hg_tile):
    n = h.shape[0]
    nt = n // tm

    def projected(c):
        return pl.BlockSpec((tm, c), lambda i: (jnp.minimum(i, nt - 1), 0))

    def mixed(c):
        return pl.BlockSpec((tm, c), lambda i: (jnp.maximum(i - 1, 0), 0))

    assert w_proj.shape[1:] == (D_MODEL, PROJ_COLS)
    params = tuple(rw_params) + (lb_raw, norm_g)
    return pl.pallas_call(
        functools.partial(_proj_mix_kernel, layer=layer, tiles_per_seq=seq // tm, hg_tile=hg_tile),
        grid=(nt + 1,),
        in_specs=[projected(D_MODEL), _resident_layer(w_proj, layer)]
                 + [_resident(a.shape) for a in params],
        out_specs=[projected(SB_WIDTH), projected(SB_WIDTH), projected(SB_WIDTH),
                   projected(3 * D_MODEL), mixed(RW_WIDTH), mixed(HG_WIDTH)],
        out_shape=[jax.ShapeDtypeStruct((n, SB_WIDTH), BF16),
                   jax.ShapeDtypeStruct((n, SB_WIDTH), BF16),
                   jax.ShapeDtypeStruct((n, SB_WIDTH), BF16),
                   jax.ShapeDtypeStruct((n, 3 * D_MODEL), BF16),
                   jax.ShapeDtypeStruct((n, RW_WIDTH), BF16),
                   jax.ShapeDtypeStruct((n, HG_WIDTH), BF16)],
        scratch_shapes=[pltpu.VMEM((1, RW_U_COLS), F32),
                        pltpu.VMEM((RW_WIDTH // RW_QUAD, RW_QUAD, RW_QUAD), F32),
                        pltpu.VMEM((HG_HEADS, HG_DIM, HG_DIM), F32),
                        pltpu.VMEM((tm, RW_U_COLS), F32),
                        pltpu.VMEM((tm, 4 * HG_WIDTH), F32)],
        compiler_params=pltpu.CompilerParams(dimension_semantics=("arbitrary",),
                                             vmem_limit_bytes=VMEM_LIMIT),
        name="proj_mix",
    )(h, w_proj, *params)


def _pad_axis(w, axis, size):
    pads = [(0, 0)] * w.ndim
    pads[axis] = (0, size - w.shape[axis])
    return jnp.pad(w, pads)


def _split_rw_cols(w):
    o = 3 * RW_WIDTH
    parts = [w[..., :o]]
    for size, pad in zip((RW_DECAY_LORA, RW_AAA_LORA, RW_GATE_LORA), RW_LORA_PAD):
        parts.append(_pad_axis(w[..., o:o + size], -1, pad))
        o += size
    return jnp.concatenate(parts, axis=-1)


def kernel(x, p, ln_g, ln_b, ffn1_wg, ffn1_wu, ffn1_wd, w_in, rw_mu, rw_w0, rw_w_up, rw_a0, rw_a_up, rw_g_up, rw_k_k, rw_k_a, rw_r_k, rw_gn_g, rw_gn_b, hg_lb_raw, hg_norm_g, w_br_rw, w_br_sb, w_br_hg, w_out, ffn2_wg, ffn2_wu, ffn2_wd, ple_gate, ple_proj):
    bsz, seq, _ = x.shape
    n = bsz * seq
    tm = min(256, seq)
    sb_blk = min(256, seq)
    hg_tile = min(128, seq)
    rw_cols = 3 * RW_WIDTH + RW_DECAY_LORA + RW_AAA_LORA + RW_GATE_LORA

    ffn1 = [w.astype(BF16) for w in (ffn1_wg, ffn1_wu, ffn1_wd)]
    w_proj = jnp.concatenate([_split_rw_cols(w_in[..., :rw_cols]), w_in[..., rw_cols:]], axis=-1).astype(BF16)
    loras = [_pad_axis(w, 1, pad).astype(BF16)
             for w, pad in zip((rw_w_up, rw_a_up, rw_g_up), RW_LORA_PAD)]
    rw_mu_cols = _split_rw_cols(rw_mu)
    tail = [w.astype(BF16) for w in (w_br_rw, w_br_sb, w_br_hg, w_out, ffn2_wg, ffn2_wu, ffn2_wd,
                                     ple_gate, ple_proj)]

    h = x.reshape(n, D_MODEL)
    for i in range(DEPTH):
        def vec(a):
            return a[i].reshape(1, -1)

        h = _ffn_ln(h, *ffn1, i, vec(ln_g[:, 0]), vec(ln_b[:, 0]), min(1024, n))
        rw_params = (vec(rw_mu_cols), vec(rw_w0), loras[0][i], vec(rw_a0), loras[1][i], loras[2][i],
                     vec(rw_k_k), vec(rw_k_a), vec(rw_r_k), vec(rw_gn_g), vec(rw_gn_b))
        q, k, v, gate, y_rw, y_hg = _proj_mix(h, w_proj, rw_params, hg_lb_raw, vec(hg_norm_g),
                                              i, seq, tm, hg_tile)
        y_sb = _sb_attention(q, k, v, bsz, seq, sb_blk)
        h = _layer_tail(h, y_rw, y_sb, y_hg, gate, p[i].reshape(n, PLE_DIM), tail, i,
                        ln_g[i, 1:], ln_b[i, 1:], min(512, n))
    return h.reshape(bsz, seq, D_MODEL)
```

```python
import functools

import jax
import jax.numpy as jnp
from jax import lax
from jax.experimental import pallas as pl
from jax.experimental.pallas import tpu as pltpu

F32 = jnp.float32
BF16 = jnp.bfloat16

D_MODEL = 1024
DEPTH = 4
PLE_DIM = 256
D_FF = 2816
RW_HEADS = 8
RW_HEAD_DIM = 64
RW_WIDTH = RW_HEADS * RW_HEAD_DIM
RW_CHUNK = 64
RW_DECAY_LORA = 64
RW_AAA_LORA = 64
RW_GATE_LORA = 160
RW_GN_EPS = 64e-5
SB_HEADS = 4
SB_HEAD_DIM = 128
SB_WIDTH = SB_HEADS * SB_HEAD_DIM
SB_DEAD_LOG = -105.0
HG_HEADS = 4
HG_DIM = 128
HG_WIDTH = HG_HEADS * HG_DIM
HG_EPS = 1e-5
LOG2_E = 1.4426950408889634
LN_EPS = 1e-5
ALPHA = (2 * DEPTH) ** 0.25

MXU_COLS = 256
RW_QUAD = MXU_COLS
FFN_COL_GROUP = 6 * MXU_COLS
RW_LORA_PAD = (128, 128, 256)
RW_U_COLS = 3 * RW_WIDTH + sum(RW_LORA_PAD)
PROJ_WIDTHS = {"rw": RW_U_COLS, "sb": 3 * SB_WIDTH, "hg": 4 * HG_WIDTH, "gate": 3 * D_MODEL}
PROJ_BASES = {"rw": 0, "sb": RW_U_COLS, "hg": RW_U_COLS + 3 * SB_WIDTH,
              "gate": RW_U_COLS + 3 * SB_WIDTH + 4 * HG_WIDTH}
PROJ_COLS = sum(PROJ_WIDTHS.values())
VMEM_LIMIT = 56 * 1024 * 1024
MIX_TILE = 256
HG_TILE = 128
SB_BLOCK = 256
FFN_TILE = 1024
TAIL_TILE = 512


def _resident(shape):
    return pl.BlockSpec(shape, lambda *_: (0,) * len(shape), pipeline_mode=pl.Buffered(1))


def _resident_layer(stacked, layer):
    rest = stacked.shape[1:]
    return pl.BlockSpec((None,) + rest, lambda *_: (layer,) + (0,) * len(rest),
                        pipeline_mode=pl.Buffered(1))


def _dot(a, b):
    return jnp.dot(a, b, preferred_element_type=F32)


def _dot_nt(a, b):
    return lax.dot_general(a, b, (((1,), (1,)), ((), ())), preferred_element_type=F32)


def _dot_tn(a, b):
    return lax.dot_general(a, b, (((0,), (0,)), ((), ())), preferred_element_type=F32)


def _split2(x):
    hi = x.astype(BF16)
    lo = (x - hi.astype(F32)).astype(BF16)
    return hi, lo


def _split3(x):
    hi = x.astype(BF16)
    r = x - hi.astype(F32)
    mid = r.astype(BF16)
    lo = (r - mid.astype(F32)).astype(BF16)
    return hi, mid, lo


def _sigmoid(x):
    return 0.5 * jnp.tanh(0.5 * x) + 0.5


def _softplus(x):
    return jnp.maximum(x, 0.0) + jnp.log(1.0 + jnp.exp(-jnp.abs(x)))


def _layer_norm(x, g, b):
    mu = jnp.mean(x, axis=-1, keepdims=True)
    xc = x - mu
    var = jnp.mean(xc * xc, axis=-1, keepdims=True)
    return xc * lax.rsqrt(var + LN_EPS) * g + b


def _swiglu(hb, wg_ref, wu_ref, wd_ref, out):
    y = None
    for c0 in range(0, D_FF, FFN_COL_GROUP):
        cols = slice(c0, min(c0 + FFN_COL_GROUP, D_FF))
        g = _dot(hb, wg_ref[:, cols])
        u = _dot(hb, wu_ref[:, cols])
        yield
        part = _dot((g * _sigmoid(g) * u).astype(BF16), wd_ref[cols, :])
        y = part if y is None else y + part
        yield
    out.append(y)


def _round_robin(generators):
    live = list(generators)
    while live:
        for gen in list(live):
            if next(gen, StopIteration) is StopIteration:
                live.remove(gen)


def _row_groups(rows):
    half = rows // 2 if rows % 16 == 0 else rows
    return [slice(r, r + half) for r in range(0, rows, half)]


def _ffn_ln_kernel(h_ref, wg_ref, wu_ref, wd_ref, lng_ref, lnb_ref, o_ref):
    def rows_stage(rows):
        h = h_ref[rows]
        y = []
        yield from _swiglu(h.astype(BF16), wg_ref, wu_ref, wd_ref, y)
        o_ref[rows] = _layer_norm(ALPHA * h + 0.5 * y[0], lng_ref[...], lnb_ref[...])

    _round_robin(rows_stage(rows) for rows in _row_groups(h_ref.shape[0]))


def _ffn_ln(h, wg, wu, wd, layer, ln_g, ln_b, tm):
    n = h.shape[0]
    row = pl.BlockSpec((tm, D_MODEL), lambda i: (i, 0))
    return pl.pallas_call(
        _ffn_ln_kernel,
        grid=(n // tm,),
        in_specs=[row] + [_resident_layer(w, layer) for w in (wg, wu, wd)]
                 + [_resident((1, D_MODEL)), _resident((1, D_MODEL))],
        out_specs=row,
        out_shape=jax.ShapeDtypeStruct((n, D_MODEL), F32),
        compiler_params=pltpu.CompilerParams(dimension_semantics=("arbitrary",),
                                             vmem_limit_bytes=VMEM_LIMIT),
        name="ffn_ln",
    )(h, wg, wu, wd, ln_g, ln_b)


def _layer_tail_kernel(h_ref, yrw_ref, ysb_ref, yhg_ref, gate_ref, p_ref,
                       wrw_ref, wsb_ref, whg_ref, wout_ref, wg_ref, wu_ref, wd_ref, pg_ref, pp_ref,
                       lng_ref, lnb_ref, o_ref):
    def rows_stage(rows):
        branches = [_dot(y_ref[rows], w_ref[...]) for y_ref, w_ref in
                    ((yrw_ref, wrw_ref), (ysb_ref, wsb_ref), (yhg_ref, whg_ref))]
        yield
        gate = gate_ref[rows]
        merged = sum(gate[:, j * D_MODEL:(j + 1) * D_MODEL] * b for j, b in enumerate(branches))
        mix = _dot(merged.astype(BF16), wout_ref[...])
        yield
        h = _layer_norm(ALPHA * h_ref[rows] + mix, lng_ref[0:1], lnb_ref[0:1])
        y = []
        yield from _swiglu(h.astype(BF16), wg_ref, wu_ref, wd_ref, y)
        h = _layer_norm(ALPHA * h + 0.5 * y[0], lng_ref[1:2], lnb_ref[1:2])
        ple_gate = _dot(h.astype(BF16), pg_ref[...])
        ple_in = _dot(p_ref[rows].astype(BF16), pp_ref[...])
        yield
        o_ref[rows] = _layer_norm(ALPHA * h + _sigmoid(ple_gate) * ple_in, lng_ref[2:3], lnb_ref[2:3])

    _round_robin(rows_stage(rows) for rows in _row_groups(h_ref.shape[0]))


def _layer_tail(h, y_rw, y_sb, y_hg, gate, p, weights, layer, ln_g, ln_b, tm):
    n = h.shape[0]

    def row(c):
        return pl.BlockSpec((tm, c), lambda i: (i, 0))

    return pl.pallas_call(
        _layer_tail_kernel,
        grid=(n // tm,),
        in_specs=[row(D_MODEL), row(RW_WIDTH), row(SB_WIDTH), row(HG_WIDTH), row(3 * D_MODEL),
                  row(PLE_DIM)] + [_resident_layer(w, layer) for w in weights]
                 + [_resident(ln_g.shape), _resident(ln_b.shape)],
        out_specs=row(D_MODEL),
        out_shape=jax.ShapeDtypeStruct((n, D_MODEL), F32),
        compiler_params=pltpu.CompilerParams(dimension_semantics=("arbitrary",),
                                             vmem_limit_bytes=VMEM_LIMIT),
        name="layer_tail",
    )(h, y_rw, y_sb, y_hg, gate, p, *weights, ln_g, ln_b)


def _sb_kernel(q_ref, k_ref, v_ref, o_ref, acc_ref, carry_ref, *, blk):
    i = pl.program_id(1)
    row = lax.broadcasted_iota(jnp.int32, (blk, blk), 0)
    col = lax.broadcasted_iota(jnp.int32, (blk, blk), 1)
    suffix = jnp.where(row >= col, 1.0, 0.0).astype(BF16)
    strict = col < row

    lanes = [slice(h * SB_HEAD_DIM, (h + 1) * SB_HEAD_DIM) for h in range(SB_HEADS)]

    def neg_softplus(z):
        return -(jnp.maximum(z, 0.0) + jnp.log(1.0 + jnp.exp(-jnp.abs(z))))

    def key_block(start):
        zs = [_dot_nt(q_ref[:, sl], k_ref[pl.ds(start, blk), sl]) for sl in lanes]
        withins = [_dot(neg_softplus(z).astype(BF16), suffix) for z in zs]
        atts = [jnp.exp(z + w + carry_ref[h]) for h, (z, w) in enumerate(zip(zs, withins))]
        for h, sl in enumerate(lanes):
            acc_ref[:, sl] += _dot(atts[h].astype(BF16), v_ref[pl.ds(start, blk), sl])
            carry_ref[h] += withins[h][:, 0:1]

    has_prev = i > 0
    d_start = pl.multiple_of(i * blk, blk)
    p_start = pl.multiple_of(jnp.maximum(i - 1, 0) * blk, blk)
    z_d = [_dot_nt(q_ref[:, sl], k_ref[pl.ds(d_start, blk), sl]) for sl in lanes]
    z_p = [_dot_nt(q_ref[:, sl], k_ref[pl.ds(p_start, blk), sl]) for sl in lanes]
    m_d = [jnp.where(strict, neg_softplus(z), 0.0) for z in z_d]
    m_p = [jnp.where(has_prev, neg_softplus(z), 0.0) for z in z_p]
    w_d = [_dot(m.astype(BF16), suffix) for m in m_d]
    w_p = [_dot(m.astype(BF16), suffix) for m in m_p]
    for h, sl in enumerate(lanes):
        att_d = jnp.where(strict, jnp.exp(z_d[h] + w_d[h]), 0.0)
        att_p = jnp.where(has_prev, jnp.exp(z_p[h] + w_p[h] + w_d[h][:, 0:1]), 0.0)
        acc_ref[:, sl] = (_dot(att_d.astype(BF16), v_ref[pl.ds(d_start, blk), sl])
                          + _dot(att_p.astype(BF16), v_ref[pl.ds(p_start, blk), sl]))
        carry_ref[h] = w_d[h][:, 0:1] + w_p[h][:, 0:1]

    def more(t):
        return jnp.logical_and(t < i, jnp.max(carry_ref[...]) > SB_DEAD_LOG)

    def body(t):
        key_block(pl.multiple_of((i - 1 - t) * blk, blk))
        return t + 1

    lax.while_loop(more, body, 1)
    o_ref[...] = acc_ref[...].astype(o_ref.dtype)


def _sb_attention(q, k, v, bsz, seq, blk):
    n = bsz * seq
    nq = seq // blk
    qspec = pl.BlockSpec((blk, SB_WIDTH), lambda b, i: (b * nq + i, 0))
    kvspec = pl.BlockSpec((seq, SB_WIDTH), lambda b, i: (b, 0))
    return pl.pallas_call(
        functools.partial(_sb_kernel, blk=blk),
        grid=(bsz, nq),
        in_specs=[qspec, kvspec, kvspec],
        out_specs=qspec,
        out_shape=jax.ShapeDtypeStruct((n, SB_WIDTH), BF16),
        scratch_shapes=[pltpu.VMEM((blk, SB_WIDTH), F32), pltpu.VMEM((SB_HEADS, blk, 1), F32)],
        compiler_params=pltpu.CompilerParams(dimension_semantics=("arbitrary", "arbitrary"),
                                             vmem_limit_bytes=VMEM_LIMIT),
        name="stick_breaking",
    )(q, k, v)


def _hgrn2_tile(u_ref, lbraw_ref, ng_ref, o_ref, rows, st_ref, layer):
    u = u_ref[rows]
    tile = u.shape[0]
    raw = lbraw_ref[...]
    e = jnp.exp(raw - jnp.max(raw, axis=0, keepdims=True))
    sm = e / jnp.sum(e, axis=0, keepdims=True)
    lb = jnp.zeros((1, HG_WIDTH), F32)
    for l in range(1, layer + 1):
        lb = lb + sm[l:l + 1]

    q = _sigmoid(u[:, :HG_WIDTH])
    f_raw = u[:, HG_WIDTH:2 * HG_WIDTH]
    val = u[:, 2 * HG_WIDTH:3 * HG_WIDTH]
    out_gate = u[:, 3 * HG_WIDTH:]

    half_tanh = 0.5 * jnp.tanh(0.5 * f_raw)
    sig_pos = 0.5 + half_tanh
    sig_neg = 0.5 - half_tanh
    log_sig = jnp.minimum(f_raw, 0.0) - jnp.log(1.0 + jnp.exp(-jnp.abs(f_raw)))
    if layer == 0:
        log_f = log_sig
    else:
        log_f = jnp.where(lb > 0.0, jnp.log(lb + (1.0 - lb) * sig_pos), log_sig)
    kg = (1.0 - lb) * sig_neg
    yield

    pos = lax.broadcasted_iota(jnp.int32, (tile, HG_WIDTH), 0)
    ti = lax.broadcasted_iota(jnp.int32, (tile, tile), 0)
    si = lax.broadcasted_iota(jnp.int32, (tile, tile), 1)
    prefix = jnp.where(si <= ti, 1.0, 0.0).astype(BF16)
    g = sum(_dot(prefix, piece) for piece in _split3(log_f))

    heads = [slice(h * HG_DIM, (h + 1) * HG_DIM) for h in range(HG_HEADS)]
    qb = q.astype(BF16)
    kgb = kg.astype(BF16)
    att = [jnp.where(ti == si, _dot_nt(qb[:, sl], kgb[:, sl]), 0.0) for sl in heads]
    last = g
    w = 1
    while w < tile:
        upper = (pos & (2 * w - 1)) >= w
        g_ref = jnp.where(upper, pltpu.roll(last, w, 0), last)
        last = jnp.where(upper, last, pltpu.roll(last, tile - w, 0))
        decay = jnp.exp2(jnp.abs(g - g_ref) * -LOG2_E)
        z = (jnp.where(upper, q, kg) * decay).astype(BF16)
        pair = ((ti & -(2 * w)) == (si & -(2 * w))) & ((ti & w) != 0) & ((si & w) == 0)
        att = [a + jnp.where(pair, _dot_nt(z[:, sl], z[:, sl]), 0.0) for a, sl in zip(att, heads)]
        yield
        w *= 2

    valb = val.astype(BF16)
    qg = (q * jnp.exp(g)).astype(BF16)
    g_last = g[tile - 1:tile]
    kd = (kg * jnp.exp(g_last - g)).astype(BF16)
    gam = jnp.exp(g_last)
    ng = ng_ref[...]
    yield
    for h, sl in enumerate(heads):
        st = st_ref[h]
        o = _dot(att[h].astype(BF16), valb[:, sl]) + _dot_nt(qg[:, sl], st.astype(BF16))
        o = o * lax.rsqrt(jnp.mean(o * o, axis=-1, keepdims=True) + HG_EPS) * ng[:, sl]
        gate = out_gate[:, sl]
        o_ref[rows, sl] = (o * (gate * _sigmoid(gate))).astype(o_ref.dtype)
        st_ref[h] = st * gam[:, sl] + _dot_tn(valb[:, sl], kd[:, sl])


def _quad_mask(rows):
    r = lax.broadcasted_iota(jnp.int32, (rows, RW_QUAD), 0)
    c = lax.broadcasted_iota(jnp.int32, (rows, RW_QUAD), 1)
    return r, c


def _block_diag(y, head_ones):
    return jnp.concatenate([y, y, y, y], axis=0) * head_ones


def _rwkv7_tile(u, mu_ref, w0_ref, wup_ref, a0_ref, aup_ref, gup_ref, kk_ref, ka_ref,
                rk_ref, gng_ref, gnb_ref, o_ref, prev_ref, st_ref, fill):
    C = RW_CHUNK
    W = RW_WIDTH
    T = u.shape[0]
    trow = lax.broadcasted_iota(jnp.int32, u.shape, 0)
    shifted = jnp.where(trow == 0, prev_ref[...], pltpu.roll(u, 1, 0))
    prev_ref[...] = u[T - 1:T]
    u = u + (shifted - u) * mu_ref[...]
    fill(4)

    br, bc = _quad_mask(RW_QUAD)
    same_head = (br // RW_HEAD_DIM) == (bc // RW_HEAD_DIM)
    head_ones = jnp.where(same_head, 1.0, 0.0).astype(BF16)
    eye_bd = br == bc
    tr, tc = _quad_mask(C)
    s_idx = tc % RW_HEAD_DIM
    strict = s_idx < tr
    incl = s_idx <= tr
    eye_q = jnp.where(s_idx == tr, 1.0, 0.0)
    n_quads = W // RW_QUAD

    def bd(y):
        return _block_diag(y.astype(BF16), head_ones)

    def head_sum(x):
        n = x.shape[0]
        pieces = [p[:, sl] for p in _split2(x) for sl in (slice(0, RW_QUAD), slice(RW_QUAD, W))]
        s = _dot(jnp.concatenate(pieces, axis=0), head_ones)
        halves = [s[j * n:(j + 1) * n] + s[(j + 2) * n:(j + 3) * n] for j in range(2)]
        return jnp.concatenate(halves, axis=1)

    def prepare(rows, out):
        ur = u[rows]
        n = ur.shape[0]
        r = ur[:, :W]
        k = ur[:, W:2 * W]
        v = ur[:, 2 * W:3 * W]
        o0 = 3 * W
        xw = ur[:, o0:o0 + RW_LORA_PAD[0]]
        xa = ur[:, o0 + RW_LORA_PAD[0]:o0 + RW_LORA_PAD[0] + RW_LORA_PAD[1]]
        xg = ur[:, o0 + RW_LORA_PAD[0] + RW_LORA_PAD[1]:]
        w_log = -_softplus(-(w0_ref[...] + _dot(jnp.tanh(xw).astype(BF16), wup_ref[...]))) - 0.5
        log_w = -jnp.exp(w_log)
        a = _sigmoid(a0_ref[...] + _dot(xa.astype(BF16), aup_ref[...]))
        gate = _dot(_sigmoid(xg).astype(BF16), gup_ref[...])
        yield
        kk = k * kk_ref[...]
        kk = kk * jnp.minimum(lax.rsqrt(head_sum(kk * kk)), 1e12)
        k = k * (1.0 + (a - 1.0) * ka_ref[...])
        bonus = head_sum(r * k * rk_ref[...]) * v
        a_vec = -kk
        b_vec = kk * a
        yield
        tt = lax.broadcasted_iota(jnp.int32, (n, n), 0)
        ts = lax.broadcasted_iota(jnp.int32, (n, n), 1)
        prefix = jnp.where((tt // C == ts // C) & (ts <= tt), 1.0, 0.0).astype(BF16)
        g = sum(_dot(prefix, piece) for piece in _split3(log_w))
        yield
        g_last = jnp.concatenate(
            [jnp.broadcast_to(g[c * C + C - 1:c * C + C], (C, W)) for c in range(n // C)], axis=0)
        inv = jnp.exp(-g)
        dec = jnp.exp(g_last - g)
        out.update(a_dec=(a_vec * jnp.exp(g - log_w)).astype(BF16), r_dec=r * jnp.exp(g),
                   b_inv=(b_vec * inv).astype(BF16), k_inv=(k * inv).astype(BF16),
                   b_dec=(b_vec * dec).astype(BF16), k_dec=(k * dec).astype(BF16),
                   v=v.astype(BF16), e_last=jnp.exp(g_last), bonus=bonus, gate=gate)

    def process(rows, p, fill):
        n = p["gate"].shape[0]
        items = [(slice(c * C, (c + 1) * C), slice(qd * RW_QUAD, (qd + 1) * RW_QUAD))
                 for c in range(n // C) for qd in range(n_quads)]
        cut = [{key: p[key][rs, ls] for key in ("a_dec", "r_dec", "v", "b_dec")} for rs, ls in items]

        for it, (rs, ls) in zip(cut, items):
            ar = jnp.concatenate([it["a_dec"], it["r_dec"].astype(BF16)], axis=0)
            lm_b = _dot_nt(ar, bd(p["b_inv"][rs, ls]))
            lm_k = _dot_nt(ar, bd(p["k_inv"][rs, ls]))
            it["l_ab"] = jnp.where(strict, lm_b[:C], 0.0)
            it["m_rb"] = jnp.where(incl, lm_b[C:], 0.0).astype(BF16)
            it["m_rk"] = jnp.where(incl, lm_k[C:], 0.0).astype(BF16)
            it["v_bd"] = bd(it["v"])
            it["lv"] = _dot(jnp.where(strict, lm_k[:C], 0.0).astype(BF16), it["v_bd"])
            it["inv_l"] = eye_q + it["l_ab"]
            it["lp"] = it["l_ab"]
            fill()

        for level in range(5):
            for it in cut:
                lp_bd = bd(it["lp"])
                if level == 0:
                    it["lp"] = _dot(it["lp"].astype(BF16), lp_bd)
                else:
                    both = jnp.concatenate([it["inv_l"], it["lp"]], axis=0).astype(BF16)
                    both = _dot(both, lp_bd)
                    it["inv_l"] = it["inv_l"] + both[:C]
                    it["lp"] = both[C:]
            fill(3)
        for it in cut:
            it["inv_l"] = it["inv_l"] + _dot(it["inv_l"].astype(BF16), bd(it["lp"]))
        fill(3)

        for it, (rs, ls) in zip(cut, items):
            inv_b = it["inv_l"].astype(BF16)
            w1 = _dot(inv_b, bd(it["a_dec"]))
            u0 = _dot(inv_b, bd(it["lv"]))
            it["q_eff"] = (it["r_dec"] + _dot(it["m_rb"], bd(w1))).astype(BF16)
            it["y_intra"] = _dot(it["m_rb"], bd(u0)) + _dot(it["m_rk"], it["v_bd"])
            it["pt_bd"] = (jnp.where(same_head, _dot_tn(it["b_dec"], w1.astype(BF16)), 0.0)
                           + jnp.where(eye_bd, p["e_last"][rs, ls][:1], 0.0)).astype(BF16)
            keys = jnp.concatenate([it["b_dec"], p["k_dec"][rs, ls]], axis=0)
            vals = jnp.concatenate([u0.astype(BF16), it["v"]], axis=0)
            it["dt_bd"] = jnp.where(same_head, _dot_tn(keys, vals), 0.0)
            fill()

        rows_out = []
        for c in range(n // C):
            ys = []
            for qd in range(n_quads):
                it = cut[c * n_quads + qd]
                st = st_ref[qd].astype(BF16)
                ys.append(it["y_intra"] + _dot(it["q_eff"], st))
                st_ref[qd] = _dot(it["pt_bd"], st) + it["dt_bd"]
            rows_out.append(jnp.concatenate(ys, axis=1))
            fill()
        y = jnp.concatenate(rows_out, axis=0)
        mean = head_sum(y) * (1.0 / RW_HEAD_DIM)
        yc = y - mean
        var = head_sum(yc * yc) * (1.0 / RW_HEAD_DIM)
        y = yc * lax.rsqrt(var + RW_GN_EPS) * gng_ref[...] + gnb_ref[...]
        o_ref[rows] = ((y + p["bonus"]) * p["gate"]).astype(o_ref.dtype)

    whole = slice(0, T)
    prepared = {}
    for _ in prepare(whole, prepared):
        fill(4)
    fill(4)
    process(whole, prepared, fill)


def _proj_mix_kernel(h_ref, w_ref,
                     mu_ref, w0_ref, wup_ref, a0_ref, aup_ref, gup_ref, kk_ref, ka_ref, rk_ref,
                     gng_ref, gnb_ref, lbraw_ref, ng_ref,
                     q_ref, k_ref, v_ref, gate_ref, yrw_ref, yhg_ref,
                     prev_ref, rwst_ref, hgst_ref, urw_ref, uhg_ref,
                     *, layer, tiles_per_seq, hg_tile):
    i = pl.program_id(0)

    @pl.when(i == 0)
    def _():
        urw_ref[...] = jnp.zeros_like(urw_ref)
        uhg_ref[...] = jnp.zeros_like(uhg_ref)

    @pl.when(jnp.maximum(i - 1, 0) % tiles_per_seq == 0)
    def _():
        prev_ref[...] = jnp.zeros_like(prev_ref)
        rwst_ref[...] = jnp.zeros_like(rwst_ref)
        hgst_ref[...] = jnp.zeros_like(hgst_ref)

    hb = h_ref[...].astype(BF16)

    def piece(base, c, store):
        cols = slice(c * MXU_COLS, (c + 1) * MXU_COLS)
        store(cols, _dot(hb, w_ref[:, base + cols.start:base + cols.stop]))

    def store_rw(cols, y):
        urw_ref[:, cols] = y

    def store_hg(cols, y):
        uhg_ref[:, cols] = y

    def store_gate(cols, y):
        gate_ref[:, cols] = _sigmoid(y).astype(BF16)

    def store_sb(cols, y):
        o_ref = (q_ref, k_ref, v_ref)[cols.start // SB_WIDTH]
        local = slice(cols.start % SB_WIDTH, cols.start % SB_WIDTH + MXU_COLS)
        o_ref[:, local] = (y * (SB_HEAD_DIM ** -0.5) if o_ref is q_ref else y).astype(BF16)

    def pieces_of(group, store):
        base, width = PROJ_BASES[group], PROJ_WIDTHS[group]
        return [functools.partial(piece, base, c, store) for c in range(width // MXU_COLS)]

    def hgrn2_stages():
        tiles = []
        for j in range(hb.shape[0] // hg_tile):
            rows = slice(j * hg_tile, (j + 1) * hg_tile)
            tiles.append(_hgrn2_tile(uhg_ref, lbraw_ref, ng_ref, yhg_ref, rows, hgst_ref, layer))
            next(tiles[-1])
            yield
        for t in tiles:
            yield from t
            yield

    hgrn2 = hgrn2_stages()
    hgrn2_step = functools.partial(next, hgrn2, None)
    for _ in range(hb.shape[0] // hg_tile):
        hgrn2_step()
    queue = pieces_of("hg", store_hg) + pieces_of("rw", store_rw)
    for p in pieces_of("sb", store_sb) + pieces_of("gate", store_gate):
        queue += [hgrn2_step, p]

    def fill(n=1):
        for _ in range(n):
            if queue:
                queue.pop(0)()

    _rwkv7_tile(urw_ref[...], mu_ref, w0_ref, wup_ref, a0_ref, aup_ref, gup_ref, kk_ref,
                ka_ref, rk_ref, gng_ref, gnb_ref, yrw_ref, prev_ref, rwst_ref, fill)
    fill(len(queue))
    for _ in hgrn2:
        pass


def _proj_mix(h, w_proj, rw_params, lb_raw, norm_g, layer, seq, tm, hg_tile):
    n = h.shape[0]
    nt = n // tm

    def projected(c):
        return pl.BlockSpec((tm, c), lambda i: (jnp.minimum(i, nt - 1), 0))

    def mixed(c):
        return pl.BlockSpec((tm, c), lambda i: (jnp.maximum(i - 1, 0), 0))

    assert w_proj.shape[1:] == (D_MODEL, PROJ_COLS)
    params = tuple(rw_params) + (lb_raw, norm_g)
    return pl.pallas_call(
        functools.partial(_proj_mix_kernel, layer=layer, tiles_per_seq=seq // tm, hg_tile=hg_tile),
        grid=(nt + 1,),
        in_specs=[projected(D_MODEL), _resident_layer(w_proj, layer)]
                 + [_resident(a.shape) for a in params],
        out_specs=[projected(SB_WIDTH), projected(SB_WIDTH), projected(SB_WIDTH),
                   projected(3 * D_MODEL), mixed(RW_WIDTH), mixed(HG_WIDTH)],
        out_shape=[jax.ShapeDtypeStruct((n, SB_WIDTH), BF16),
                   jax.ShapeDtypeStruct((n, SB_WIDTH), BF16),
                   jax.ShapeDtypeStruct((n, SB_WIDTH), BF16),
                   jax.ShapeDtypeStruct((n, 3 * D_MODEL), BF16),
                   jax.ShapeDtypeStruct((n, RW_WIDTH), BF16),
                   jax.ShapeDtypeStruct((n, HG_WIDTH), BF16)],
        scratch_shapes=[pltpu.VMEM((1, RW_U_COLS), F32),
                        pltpu.VMEM((RW_WIDTH // RW_QUAD, RW_QUAD, RW_QUAD), F32),
                        pltpu.VMEM((HG_HEADS, HG_DIM, HG_DIM), F32),
                        pltpu.VMEM((tm, RW_U_COLS), F32),
                        pltpu.VMEM((tm, 4 * HG_WIDTH), F32)],
        compiler_params=pltpu.CompilerParams(dimension_semantics=("arbitrary",),
                                             vmem_limit_bytes=VMEM_LIMIT),
        name="proj_mix",
    )(h, w_proj, *params)


def _pad_axis(w, axis, size):
    pads = [(0, 0)] * w.ndim
    pads[axis] = (0, size - w.shape[axis])
    return jnp.pad(w, pads)


def _split_rw_cols(w):
    o = 3 * RW_WIDTH
    parts = [w[..., :o]]
    for size, pad in zip((RW_DECAY_LORA, RW_AAA_LORA, RW_GATE_LORA), RW_LORA_PAD):
        parts.append(_pad_axis(w[..., o:o + size], -1, pad))
        o += size
    return jnp.concatenate(parts, axis=-1)


def kernel(x, p, ln_g, ln_b, ffn1_wg, ffn1_wu, ffn1_wd, w_in, rw_mu, rw_w0, rw_w_up, rw_a0, rw_a_up, rw_g_up, rw_k_k, rw_k_a, rw_r_k, rw_gn_g, rw_gn_b, hg_lb_raw, hg_norm_g, w_br_rw, w_br_sb, w_br_hg, w_out, ffn2_wg, ffn2_wu, ffn2_wd, ple_gate, ple_proj):
    bsz, seq, _ = x.shape
    n = bsz * seq
    tm = min(MIX_TILE, seq)
    sb_blk = min(SB_BLOCK, seq)
    hg_tile = min(HG_TILE, seq)
    rw_cols = 3 * RW_WIDTH + RW_DECAY_LORA + RW_AAA_LORA + RW_GATE_LORA

    ffn1 = [w.astype(BF16) for w in (ffn1_wg, ffn1_wu, ffn1_wd)]
    w_proj = jnp.concatenate([_split_rw_cols(w_in[..., :rw_cols]), w_in[..., rw_cols:]], axis=-1).astype(BF16)
    loras = [_pad_axis(w, 1, pad).astype(BF16)
             for w, pad in zip((rw_w_up, rw_a_up, rw_g_up), RW_LORA_PAD)]
    rw_mu_cols = _split_rw_cols(rw_mu)
    tail = [w.astype(BF16) for w in (w_br_rw, w_br_sb, w_br_hg, w_out, ffn2_wg, ffn2_wu, ffn2_wd,
                                     ple_gate, ple_proj)]

    h = x.reshape(n, D_MODEL)
    for i in range(DEPTH):
        def vec(a):
            return a[i].reshape(1, -1)

        h = _ffn_ln(h, *ffn1, i, vec(ln_g[:, 0]), vec(ln_b[:, 0]), min(FFN_TILE, n))
        rw_params = (vec(rw_mu_cols), vec(rw_w0), loras[0][i], vec(rw_a0), loras[1][i], loras[2][i],
                     vec(rw_k_k), vec(rw_k_a), vec(rw_r_k), vec(rw_gn_g), vec(rw_gn_b))
        q, k, v, gate, y_rw, y_hg = _proj_mix(h, w_proj, rw_params, hg_lb_raw, vec(hg_norm_g),
                                              i, seq, tm, hg_tile)
        y_sb = _sb_attention(q, k, v, bsz, seq, sb_blk)
        h = _layer_tail(h, y_rw, y_sb, y_hg, gate, p[i].reshape(n, PLE_DIM), tail, i,
                        ln_g[i, 1:], ln_b[i, 1:], min(TAIL_TILE, n))
    return h.reshape(bsz, seq, D_MODEL)
```

```python
import functools

import jax
import jax.numpy as jnp
from jax import lax
from jax.experimental import pallas as pl
from jax.experimental.pallas import tpu as pltpu

F32 = jnp.float32
BF16 = jnp.bfloat16

D_MODEL = 1024
DEPTH = 4
PLE_DIM = 256
D_FF = 2816
RW_HEADS = 8
RW_HEAD_DIM = 64
RW_WIDTH = RW_HEADS * RW_HEAD_DIM
RW_CHUNK = 64
RW_DECAY_LORA = 64
RW_AAA_LORA = 64
RW_GATE_LORA = 160
RW_GN_EPS = 64e-5
SB_HEADS = 4
SB_HEAD_DIM = 128
SB_WIDTH = SB_HEADS * SB_HEAD_DIM
SB_DEAD_LOG = -105.0
HG_HEADS = 4
HG_DIM = 128
HG_WIDTH = HG_HEADS * HG_DIM
HG_EPS = 1e-5
LOG2_E = 1.4426950408889634
LN_EPS = 1e-5
ALPHA = (2 * DEPTH) ** 0.25

MXU_COLS = 256
RW_QUAD = MXU_COLS
FFN_COL_GROUP = 6 * MXU_COLS
RW_LORA_PAD = (128, 128, 256)
RW_U_COLS = 3 * RW_WIDTH + sum(RW_LORA_PAD)
PROJ_WIDTHS = {"rw": RW_U_COLS, "sb": 3 * SB_WIDTH, "hg": 4 * HG_WIDTH, "gate": 3 * D_MODEL}
PROJ_BASES = {"rw": 0, "sb": RW_U_COLS, "hg": RW_U_COLS + 3 * SB_WIDTH,
              "gate": RW_U_COLS + 3 * SB_WIDTH + 4 * HG_WIDTH}
PROJ_COLS = sum(PROJ_WIDTHS.values())
VMEM_LIMIT = 56 * 1024 * 1024
MIX_TILE = 256
HG_TILE = 128
SB_BLOCK = 256
FFN_TILE = 1024
TAIL_TILE = 512


def _resident(shape):
    return pl.BlockSpec(shape, lambda *_: (0,) * len(shape), pipeline_mode=pl.Buffered(1))


def _resident_layer(stacked, layer):
    rest = stacked.shape[1:]
    return pl.BlockSpec((None,) + rest, lambda *_: (layer,) + (0,) * len(rest),
                        pipeline_mode=pl.Buffered(1))


def _dot(a, b):
    return jnp.dot(a, b, preferred_element_type=F32)


def _dot_nt(a, b):
    return lax.dot_general(a, b, (((1,), (1,)), ((), ())), preferred_element_type=F32)


def _dot_tn(a, b):
    return lax.dot_general(a, b, (((0,), (0,)), ((), ())), preferred_element_type=F32)


def _split2(x):
    hi = x.astype(BF16)
    lo = (x - hi.astype(F32)).astype(BF16)
    return hi, lo


def _split3(x):
    hi = x.astype(BF16)
    r = x - hi.astype(F32)
    mid = r.astype(BF16)
    lo = (r - mid.astype(F32)).astype(BF16)
    return hi, mid, lo


def _sigmoid(x):
    return 0.5 * jnp.tanh(0.5 * x) + 0.5


def _softplus(x):
    return jnp.maximum(x, 0.0) + jnp.log(1.0 + jnp.exp(-jnp.abs(x)))


def _layer_norm(x, g, b):
    mu = jnp.mean(x, axis=-1, keepdims=True)
    xc = x - mu
    var = jnp.mean(xc * xc, axis=-1, keepdims=True)
    return xc * lax.rsqrt(var + LN_EPS) * g + b


def _swiglu(hb, wg_ref, wu_ref, wd_ref, out):
    y = None
    for c0 in range(0, D_FF, FFN_COL_GROUP):
        cols = slice(c0, min(c0 + FFN_COL_GROUP, D_FF))
        g = _dot(hb, wg_ref[:, cols])
        u = _dot(hb, wu_ref[:, cols])
        yield
        part = _dot((g * _sigmoid(g) * u).astype(BF16), wd_ref[cols, :])
        y = part if y is None else y + part
        yield
    out.append(y)


def _round_robin(generators):
    live = list(generators)
    while live:
        for gen in list(live):
            if next(gen, StopIteration) is StopIteration:
                live.remove(gen)


def _row_groups(rows):
    half = rows // 2 if rows % 16 == 0 else rows
    return [slice(r, r + half) for r in range(0, rows, half)]


def _ffn_ln_kernel(h_ref, wg_ref, wu_ref, wd_ref, lng_ref, lnb_ref, o_ref):
    def rows_stage(rows):
        h = h_ref[rows]
        y = []
        yield from _swiglu(h.astype(BF16), wg_ref, wu_ref, wd_ref, y)
        o_ref[rows] = _layer_norm(ALPHA * h + 0.5 * y[0], lng_ref[...], lnb_ref[...])

    _round_robin(rows_stage(rows) for rows in _row_groups(h_ref.shape[0]))


def _ffn_ln(h, wg, wu, wd, layer, ln_g, ln_b, tm):
    n = h.shape[0]
    row = pl.BlockSpec((tm, D_MODEL), lambda i: (i, 0))
    return pl.pallas_call(
        _ffn_ln_kernel,
        grid=(n // tm,),
        in_specs=[row] + [_resident_layer(w, layer) for w in (wg, wu, wd)]
                 + [_resident((1, D_MODEL)), _resident((1, D_MODEL))],
        out_specs=row,
        out_shape=jax.ShapeDtypeStruct((n, D_MODEL), F32),
        compiler_params=pltpu.CompilerParams(dimension_semantics=("arbitrary",),
                                             vmem_limit_bytes=VMEM_LIMIT),
        name="ffn_ln",
    )(h, wg, wu, wd, ln_g, ln_b)


def _layer_tail_kernel(h_ref, yrw_ref, ysb_ref, yhg_ref, gate_ref, p_ref,
                       wrw_ref, wsb_ref, whg_ref, wout_ref, wg_ref, wu_ref, wd_ref, pg_ref, pp_ref,
                       lng_ref, lnb_ref, o_ref):
    def rows_stage(rows):
        branches = [_dot(y_ref[rows], w_ref[...]) for y_ref, w_ref in
                    ((yrw_ref, wrw_ref), (ysb_ref, wsb_ref), (yhg_ref, whg_ref))]
        yield
        gate = gate_ref[rows]
        merged = sum(gate[:, j * D_MODEL:(j + 1) * D_MODEL] * b for j, b in enumerate(branches))
        mix = _dot(merged.astype(BF16), wout_ref[...])
        yield
        h = _layer_norm(ALPHA * h_ref[rows] + mix, lng_ref[0:1], lnb_ref[0:1])
        y = []
        yield from _swiglu(h.astype(BF16), wg_ref, wu_ref, wd_ref, y)
        h = _layer_norm(ALPHA * h + 0.5 * y[0], lng_ref[1:2], lnb_ref[1:2])
        ple_gate = _dot(h.astype(BF16), pg_ref[...])
        ple_in = _dot(p_ref[rows].astype(BF16), pp_ref[...])
        yield
        o_ref[rows] = _layer_norm(ALPHA * h + _sigmoid(ple_gate) * ple_in, lng_ref[2:3], lnb_ref[2:3])

    _round_robin(rows_stage(rows) for rows in _row_groups(h_ref.shape[0]))


def _layer_tail(h, y_rw, y_sb, y_hg, gate, p, weights, layer, ln_g, ln_b, tm):
    n = h.shape[0]

    def row(c):
        return pl.BlockSpec((tm, c), lambda i: (i, 0))

    return pl.pallas_call(
        _layer_tail_kernel,
        grid=(n // tm,),
        in_specs=[row(D_MODEL), row(RW_WIDTH), row(SB_WIDTH), row(HG_WIDTH), row(3 * D_MODEL),
                  row(PLE_DIM)] + [_resident_layer(w, layer) for w in weights]
                 + [_resident(ln_g.shape), _resident(ln_b.shape)],
        out_specs=row(D_MODEL),
        out_shape=jax.ShapeDtypeStruct((n, D_MODEL), F32),
        compiler_params=pltpu.CompilerParams(dimension_semantics=("arbitrary",),
                                             vmem_limit_bytes=VMEM_LIMIT),
        name="layer_tail",
    )(h, y_rw, y_sb, y_hg, gate, p, *weights, ln_g, ln_b)


def _sb_kernel(q_ref, k_ref, v_ref, o_ref, acc_ref, carry_ref, *, blk):
    i = pl.program_id(1)
    row = lax.broadcasted_iota(jnp.int32, (blk, blk), 0)
    col = lax.broadcasted_iota(jnp.int32, (blk, blk), 1)
    suffix = jnp.where(row >= col, 1.0, 0.0).astype(BF16)
    strict = col < row

    lanes = [slice(h * SB_HEAD_DIM, (h + 1) * SB_HEAD_DIM) for h in range(SB_HEADS)]

    def neg_softplus(z):
        return -(jnp.maximum(z, 0.0) + jnp.log(1.0 + jnp.exp(-jnp.abs(z))))

    def key_block(start):
        zs = [_dot_nt(q_ref[:, sl], k_ref[pl.ds(start, blk), sl]) for sl in lanes]
        withins = [_dot(neg_softplus(z).astype(BF16), suffix) for z in zs]
        atts = [jnp.exp(z + w + carry_ref[h]) for h, (z, w) in enumerate(zip(zs, withins))]
        for h, sl in enumerate(lanes):
            acc_ref[:, sl] += _dot(atts[h].astype(BF16), v_ref[pl.ds(start, blk), sl])
            carry_ref[h] += withins[h][:, 0:1]

    has_prev = i > 0
    d_start = pl.multiple_of(i * blk, blk)
    p_start = pl.multiple_of(jnp.maximum(i - 1, 0) * blk, blk)
    z_d = [_dot_nt(q_ref[:, sl], k_ref[pl.ds(d_start, blk), sl]) for sl in lanes]
    z_p = [_dot_nt(q_ref[:, sl], k_ref[pl.ds(p_start, blk), sl]) for sl in lanes]
    m_d = [jnp.where(strict, neg_softplus(z), 0.0) for z in z_d]
    m_p = [jnp.where(has_prev, neg_softplus(z), 0.0) for z in z_p]
    w_d = [_dot(m.astype(BF16), suffix) for m in m_d]
    w_p = [_dot(m.astype(BF16), suffix) for m in m_p]
    for h, sl in enumerate(lanes):
        att_d = jnp.where(strict, jnp.exp(z_d[h] + w_d[h]), 0.0)
        att_p = jnp.where(has_prev, jnp.exp(z_p[h] + w_p[h] + w_d[h][:, 0:1]), 0.0)
        acc_ref[:, sl] = (_dot(att_d.astype(BF16), v_ref[pl.ds(d_start, blk), sl])
                          + _dot(att_p.astype(BF16), v_ref[pl.ds(p_start, blk), sl]))
        carry_ref[h] = w_d[h][:, 0:1] + w_p[h][:, 0:1]

    def more(t):
        return jnp.logical_and(t < i, jnp.max(carry_ref[...]) > SB_DEAD_LOG)

    def body(t):
        key_block(pl.multiple_of((i - 1 - t) * blk, blk))
        return t + 1

    lax.while_loop(more, body, 1)
    o_ref[...] = acc_ref[...].astype(o_ref.dtype)


def _sb_attention(q, k, v, bsz, seq, blk):
    n = bsz * seq
    nq = seq // blk
    qspec = pl.BlockSpec((blk, SB_WIDTH), lambda b, i: (b * nq + i, 0))
    kvspec = pl.BlockSpec((seq, SB_WIDTH), lambda b, i: (b, 0))
    return pl.pallas_call(
        functools.partial(_sb_kernel, blk=blk),
        grid=(bsz, nq),
        in_specs=[qspec, kvspec, kvspec],
        out_specs=qspec,
        out_shape=jax.ShapeDtypeStruct((n, SB_WIDTH), BF16),
        scratch_shapes=[pltpu.VMEM((blk, SB_WIDTH), F32), pltpu.VMEM((SB_HEADS, blk, 1), F32)],
        compiler_params=pltpu.CompilerParams(dimension_semantics=("arbitrary", "arbitrary"),
                                             vmem_limit_bytes=VMEM_LIMIT),
        name="stick_breaking",
    )(q, k, v)


def _hgrn2_tile(u_ref, lbraw_ref, ng_ref, o_ref, rows, st_ref, layer):
    u = u_ref[rows]
    tile = u.shape[0]
    raw = lbraw_ref[...]
    e = jnp.exp(raw - jnp.max(raw, axis=0, keepdims=True))
    sm = e / jnp.sum(e, axis=0, keepdims=True)
    lb = jnp.zeros((1, HG_WIDTH), F32)
    for l in range(1, layer + 1):
        lb = lb + sm[l:l + 1]

    q = _sigmoid(u[:, :HG_WIDTH])
    f_raw = u[:, HG_WIDTH:2 * HG_WIDTH]
    val = u[:, 2 * HG_WIDTH:3 * HG_WIDTH]
    out_gate = u[:, 3 * HG_WIDTH:]

    half_tanh = 0.5 * jnp.tanh(0.5 * f_raw)
    sig_pos = 0.5 + half_tanh
    sig_neg = 0.5 - half_tanh
    log_sig = jnp.minimum(f_raw, 0.0) - jnp.log(1.0 + jnp.exp(-jnp.abs(f_raw)))
    if layer == 0:
        log_f = log_sig
    else:
        log_f = jnp.where(lb > 0.0, jnp.log(lb + (1.0 - lb) * sig_pos), log_sig)
    kg = (1.0 - lb) * sig_neg
    yield

    pos = lax.broadcasted_iota(jnp.int32, (tile, HG_WIDTH), 0)
    ti = lax.broadcasted_iota(jnp.int32, (tile, tile), 0)
    si = lax.broadcasted_iota(jnp.int32, (tile, tile), 1)
    prefix = jnp.where(si <= ti, 1.0, 0.0).astype(BF16)
    g = sum(_dot(prefix, piece) for piece in _split3(log_f))

    heads = [slice(h * HG_DIM, (h + 1) * HG_DIM) for h in range(HG_HEADS)]
    qb = q.astype(BF16)
    kgb = kg.astype(BF16)
    att = [jnp.where(ti == si, _dot_nt(qb[:, sl], kgb[:, sl]), 0.0) for sl in heads]
    last = g
    w = 1
    while w < tile:
        upper = (pos & (2 * w - 1)) >= w
        g_ref = jnp.where(upper, pltpu.roll(last, w, 0), last)
        last = jnp.where(upper, last, pltpu.roll(last, tile - w, 0))
        decay = jnp.exp2(jnp.abs(g - g_ref) * -LOG2_E)
        z = (jnp.where(upper, q, kg) * decay).astype(BF16)
        pair = ((ti & -(2 * w)) == (si & -(2 * w))) & ((ti & w) != 0) & ((si & w) == 0)
        att = [a + jnp.where(pair, _dot_nt(z[:, sl], z[:, sl]), 0.0) for a, sl in zip(att, heads)]
        yield
        w *= 2

    valb = val.astype(BF16)
    qg = (q * jnp.exp(g)).astype(BF16)
    g_last = g[tile - 1:tile]
    kd = (kg * jnp.exp(g_last - g)).astype(BF16)
    gam = jnp.exp(g_last)
    ng = ng_ref[...]
    yield
    for h, sl in enumerate(heads):
        st = st_ref[h]
        o = _dot(att[h].astype(BF16), valb[:, sl]) + _dot_nt(qg[:, sl], st.astype(BF16))
        o = o * lax.rsqrt(jnp.mean(o * o, axis=-1, keepdims=True) + HG_EPS) * ng[:, sl]
        gate = out_gate[:, sl]
        o_ref[rows, sl] = (o * (gate * _sigmoid(gate))).astype(o_ref.dtype)
        st_ref[h] = st * gam[:, sl] + _dot_tn(valb[:, sl], kd[:, sl])


def _quad_mask(rows):
    r = lax.broadcasted_iota(jnp.int32, (rows, RW_QUAD), 0)
    c = lax.broadcasted_iota(jnp.int32, (rows, RW_QUAD), 1)
    return r, c


def _block_diag(y, head_ones):
    return jnp.concatenate([y, y, y, y], axis=0) * head_ones


def _rwkv7_tile(u, mu_ref, w0_ref, wup_ref, a0_ref, aup_ref, gup_ref, kk_ref, ka_ref,
                rk_ref, gng_ref, gnb_ref, o_ref, prev_ref, st_ref, fill):
    C = RW_CHUNK
    W = RW_WIDTH
    T = u.shape[0]
    trow = lax.broadcasted_iota(jnp.int32, u.shape, 0)
    shifted = jnp.where(trow == 0, prev_ref[...], pltpu.roll(u, 1, 0))
    prev_ref[...] = u[T - 1:T]
    u = u + (shifted - u) * mu_ref[...]
    fill(4)

    br, bc = _quad_mask(RW_QUAD)
    same_head = (br // RW_HEAD_DIM) == (bc // RW_HEAD_DIM)
    head_ones = jnp.where(same_head, 1.0, 0.0).astype(BF16)
    eye_bd = br == bc
    tr, tc = _quad_mask(C)
    s_idx = tc % RW_HEAD_DIM
    strict = s_idx < tr
    incl = s_idx <= tr
    eye_q = jnp.where(s_idx == tr, 1.0, 0.0)
    n_quads = W // RW_QUAD

    def bd(y):
        return _block_diag(y.astype(BF16), head_ones)

    def head_sum(x):
        n = x.shape[0]
        pieces = [p[:, sl] for p in _split2(x) for sl in (slice(0, RW_QUAD), slice(RW_QUAD, W))]
        s = _dot(jnp.concatenate(pieces, axis=0), head_ones)
        halves = [s[j * n:(j + 1) * n] + s[(j + 2) * n:(j + 3) * n] for j in range(2)]
        return jnp.concatenate(halves, axis=1)

    def prepare(rows, out):
        ur = u[rows]
        n = ur.shape[0]
        r = ur[:, :W]
        k = ur[:, W:2 * W]
        v = ur[:, 2 * W:3 * W]
        o0 = 3 * W
        xw = ur[:, o0:o0 + RW_LORA_PAD[0]]
        xa = ur[:, o0 + RW_LORA_PAD[0]:o0 + RW_LORA_PAD[0] + RW_LORA_PAD[1]]
        xg = ur[:, o0 + RW_LORA_PAD[0] + RW_LORA_PAD[1]:]
        w_log = -_softplus(-(w0_ref[...] + _dot(jnp.tanh(xw).astype(BF16), wup_ref[...]))) - 0.5
        log_w = -jnp.exp(w_log)
        a = _sigmoid(a0_ref[...] + _dot(xa.astype(BF16), aup_ref[...]))
        gate = _dot(_sigmoid(xg).astype(BF16), gup_ref[...])
        yield
        kk = k * kk_ref[...]
        kk = kk * jnp.minimum(lax.rsqrt(head_sum(kk * kk)), 1e12)
        k = k * (1.0 + (a - 1.0) * ka_ref[...])
        bonus = head_sum(r * k * rk_ref[...]) * v
        a_vec = -kk
        b_vec = kk * a
        yield
        tt = lax.broadcasted_iota(jnp.int32, (n, n), 0)
        ts = lax.broadcasted_iota(jnp.int32, (n, n), 1)
        prefix = jnp.where((tt // C == ts // C) & (ts <= tt), 1.0, 0.0).astype(BF16)
        g = sum(_dot(prefix, piece) for piece in _split3(log_w))
        yield
        g_last = jnp.concatenate(
            [jnp.broadcast_to(g[c * C + C - 1:c * C + C], (C, W)) for c in range(n // C)], axis=0)
        inv = jnp.exp(-g)
        dec = jnp.exp(g_last - g)
        out.update(a_dec=(a_vec * jnp.exp(g - log_w)).astype(BF16), r_dec=r * jnp.exp(g),
                   b_inv=(b_vec * inv).astype(BF16), k_inv=(k * inv).astype(BF16),
                   b_dec=(b_vec * dec).astype(BF16), k_dec=(k * dec).astype(BF16),
                   v=v.astype(BF16), e_last=jnp.exp(g_last), bonus=bonus, gate=gate)

    def process(rows, p, fill):
        n = p["gate"].shape[0]
        items = [(slice(c * C, (c + 1) * C), slice(qd * RW_QUAD, (qd + 1) * RW_QUAD))
                 for c in range(n // C) for qd in range(n_quads)]
        cut = [{key: p[key][rs, ls] for key in ("a_dec", "r_dec", "v", "b_dec")} for rs, ls in items]

        for it, (rs, ls) in zip(cut, items):
            ar = jnp.concatenate([it["a_dec"], it["r_dec"].astype(BF16)], axis=0)
            lm_b = _dot_nt(ar, bd(p["b_inv"][rs, ls]))
            lm_k = _dot_nt(ar, bd(p["k_inv"][rs, ls]))
            it["l_ab"] = jnp.where(strict, lm_b[:C], 0.0)
            it["m_rb"] = jnp.where(incl, lm_b[C:], 0.0).astype(BF16)
            it["m_rk"] = jnp.where(incl, lm_k[C:], 0.0).astype(BF16)
            it["v_bd"] = bd(it["v"])
            it["lv"] = _dot(jnp.where(strict, lm_k[:C], 0.0).astype(BF16), it["v_bd"])
            it["inv_l"] = eye_q + it["l_ab"]
            it["lp"] = it["l_ab"]
            fill()

        for level in range(5):
            for it in cut:
                lp_bd = bd(it["lp"])
                if level == 0:
                    it["lp"] = _dot(it["lp"].astype(BF16), lp_bd)
                else:
                    both = jnp.concatenate([it["inv_l"], it["lp"]], axis=0).astype(BF16)
                    both = _dot(both, lp_bd)
                    it["inv_l"] = it["inv_l"] + both[:C]
                    it["lp"] = both[C:]
            fill(1)
        for it in cut:
            it["inv_l"] = it["inv_l"] + _dot(it["inv_l"].astype(BF16), bd(it["lp"]))
        fill(3)

        for it, (rs, ls) in zip(cut, items):
            inv_b = it["inv_l"].astype(BF16)
            w1 = _dot(inv_b, bd(it["a_dec"]))
            u0 = _dot(inv_b, bd(it["lv"]))
            it["q_eff"] = (it["r_dec"] + _dot(it["m_rb"], bd(w1))).astype(BF16)
            it["y_intra"] = _dot(it["m_rb"], bd(u0)) + _dot(it["m_rk"], it["v_bd"])
            it["pt_bd"] = (jnp.where(same_head, _dot_tn(it["b_dec"], w1.astype(BF16)), 0.0)
                           + jnp.where(eye_bd, p["e_last"][rs, ls][:1], 0.0)).astype(BF16)
            keys = jnp.concatenate([it["b_dec"], p["k_dec"][rs, ls]], axis=0)
            vals = jnp.concatenate([u0.astype(BF16), it["v"]], axis=0)
            it["dt_bd"] = jnp.where(same_head, _dot_tn(keys, vals), 0.0)
            fill()

        rows_out = []
        for c in range(n // C):
            ys = []
            for qd in range(n_quads):
                it = cut[c * n_quads + qd]
                st = st_ref[qd].astype(BF16)
                ys.append(it["y_intra"] + _dot(it["q_eff"], st))
                st_ref[qd] = _dot(it["pt_bd"], st) + it["dt_bd"]
            rows_out.append(jnp.concatenate(ys, axis=1))
            fill()
        y = jnp.concatenate(rows_out, axis=0)
        mean = head_sum(y) * (1.0 / RW_HEAD_DIM)
        yc = y - mean
        var = head_sum(yc * yc) * (1.0 / RW_HEAD_DIM)
        y = yc * lax.rsqrt(var + RW_GN_EPS) * gng_ref[...] + gnb_ref[...]
        o_ref[rows] = ((y + p["bonus"]) * p["gate"]).astype(o_ref.dtype)

    whole = slice(0, T)
    prepared = {}
    for _ in prepare(whole, prepared):
        fill(4)
    fill(4)
    process(whole, prepared, fill)


def _proj_mix_kernel(h_ref, w_ref,
                     mu_ref, w0_ref, wup_ref, a0_ref, aup_ref, gup_ref, kk_ref, ka_ref, rk_ref,
                     gng_ref, gnb_ref, lbraw_ref, ng_ref,
                     q_ref, k_ref, v_ref, gate_ref, yrw_ref, yhg_ref,
                     prev_ref, rwst_ref, hgst_ref, urw_ref, uhg_ref,
                     *, layer, tiles_per_seq, hg_tile):
    i = pl.program_id(0)

    @pl.when(i == 0)
    def _():
        urw_ref[...] = jnp.zeros_like(urw_ref)
        uhg_ref[...] = jnp.zeros_like(uhg_ref)

    @pl.when(jnp.maximum(i - 1, 0) % tiles_per_seq == 0)
    def _():
        prev_ref[...] = jnp.zeros_like(prev_ref)
        rwst_ref[...] = jnp.zeros_like(rwst_ref)
        hgst_ref[...] = jnp.zeros_like(hgst_ref)

    hb = h_ref[...].astype(BF16)

    def piece(base, c, store):
        cols = slice(c * MXU_COLS, (c + 1) * MXU_COLS)
        store(cols, _dot(hb, w_ref[:, base + cols.start:base + cols.stop]))

    def store_rw(cols, y):
        urw_ref[:, cols] = y

    def store_hg(cols, y):
        uhg_ref[:, cols] = y

    def store_gate(cols, y):
        gate_ref[:, cols] = _sigmoid(y).astype(BF16)

    def store_sb(cols, y):
        o_ref = (q_ref, k_ref, v_ref)[cols.start // SB_WIDTH]
        local = slice(cols.start % SB_WIDTH, cols.start % SB_WIDTH + MXU_COLS)
        o_ref[:, local] = (y * (SB_HEAD_DIM ** -0.5) if o_ref is q_ref else y).astype(BF16)

    def pieces_of(group, store):
        base, width = PROJ_BASES[group], PROJ_WIDTHS[group]
        return [functools.partial(piece, base, c, store) for c in range(width // MXU_COLS)]

    def hgrn2_stages():
        tiles = []
        for j in range(hb.shape[0] // hg_tile):
            rows = slice(j * hg_tile, (j + 1) * hg_tile)
            tiles.append(_hgrn2_tile(uhg_ref, lbraw_ref, ng_ref, yhg_ref, rows, hgst_ref, layer))
            next(tiles[-1])
            yield
        for t in tiles:
            yield from t
            yield

    hgrn2 = hgrn2_stages()
    hgrn2_step = functools.partial(next, hgrn2, None)
    for _ in range(hb.shape[0] // hg_tile):
        hgrn2_step()
    queue = pieces_of("hg", store_hg) + pieces_of("rw", store_rw)
    for p in pieces_of("sb", store_sb) + pieces_of("gate", store_gate):
        queue += [hgrn2_step, p]

    def fill(n=1):
        for _ in range(n):
            if queue:
                queue.pop(0)()

    _rwkv7_tile(urw_ref[...], mu_ref, w0_ref, wup_ref, a0_ref, aup_ref, gup_ref, kk_ref,
                ka_ref, rk_ref, gng_ref, gnb_ref, yrw_ref, prev_ref, rwst_ref, fill)
    fill(len(queue))
    for _ in hgrn2:
        pass


def _proj_mix(h, w_proj, rw_params, lb_raw, norm_g, layer, seq, tm, hg_tile):
    n = h.shape[0]
    nt = n // tm

    def projected(c):
        return pl.BlockSpec((tm, c), lambda i: (jnp.minimum(i, nt - 1), 0))

    def mixed(c):
        return pl.BlockSpec((tm, c), lambda i: (jnp.maximum(i - 1, 0), 0))

    assert w_proj.shape[1:] == (D_MODEL, PROJ_COLS)
    params = tuple(rw_params) + (lb_raw, norm_g)
    return pl.pallas_call(
        functools.partial(_proj_mix_kernel, layer=layer, tiles_per_seq=seq // tm, hg_tile=hg_tile),
        grid=(nt + 1,),
        in_specs=[projected(D_MODEL), _resident_layer(w_proj, layer)]
                 + [_resident(a.shape) for a in params],
        out_specs=[projected(SB_WIDTH), projected(SB_WIDTH), projected(SB_WIDTH),
                   projected(3 * D_MODEL), mixed(RW_WIDTH), mixed(HG_WIDTH)],
        out_shape=[jax.ShapeDtypeStruct((n, SB_WIDTH), BF16),
                   jax.ShapeDtypeStruct((n, SB_WIDTH), BF16),
                   jax.ShapeDtypeStruct((n, SB_WIDTH), BF16),
                   jax.ShapeDtypeStruct((n, 3 * D_MODEL), BF16),
                   jax.ShapeDtypeStruct((n, RW_WIDTH), BF16),
                   jax.ShapeDtypeStruct((n, HG_WIDTH), BF16)],
        scratch_shapes=[pltpu.VMEM((1, RW_U_COLS), F32),
                        pltpu.VMEM((RW_WIDTH // RW_QUAD, RW_QUAD, RW_QUAD), F32),
                        pltpu.VMEM((HG_HEADS, HG_DIM, HG_DIM), F32),
                        pltpu.VMEM((tm, RW_U_COLS), F32),
                        pltpu.VMEM((tm, 4 * HG_WIDTH), F32)],
        compiler_params=pltpu.CompilerParams(dimension_semantics=("arbitrary",),
                                             vmem_limit_bytes=VMEM_LIMIT),
        name="proj_mix",
    )(h, w_proj, *params)


def _pad_axis(w, axis, size):
    pads = [(0, 0)] * w.ndim
    pads[axis] = (0, size - w.shape[axis])
    return jnp.pad(w, pads)


def _split_rw_cols(w):
    o = 3 * RW_WIDTH
    parts = [w[..., :o]]
    for size, pad in zip((RW_DECAY_LORA, RW_AAA_LORA, RW_GATE_LORA), RW_LORA_PAD):
        parts.append(_pad_axis(w[..., o:o + size], -1, pad))
        o += size
    return jnp.concatenate(parts, axis=-1)


def kernel(x, p, ln_g, ln_b, ffn1_wg, ffn1_wu, ffn1_wd, w_in, rw_mu, rw_w0, rw_w_up, rw_a0, rw_a_up, rw_g_up, rw_k_k, rw_k_a, rw_r_k, rw_gn_g, rw_gn_b, hg_lb_raw, hg_norm_g, w_br_rw, w_br_sb, w_br_hg, w_out, ffn2_wg, ffn2_wu, ffn2_wd, ple_gate, ple_proj):
    bsz, seq, _ = x.shape
    n = bsz * seq
    tm = min(MIX_TILE, seq)
    sb_blk = min(SB_BLOCK, seq)
    hg_tile = min(HG_TILE, seq)
    rw_cols = 3 * RW_WIDTH + RW_DECAY_LORA + RW_AAA_LORA + RW_GATE_LORA

    ffn1 = [w.astype(BF16) for w in (ffn1_wg, ffn1_wu, ffn1_wd)]
    w_proj = jnp.concatenate([_split_rw_cols(w_in[..., :rw_cols]), w_in[..., rw_cols:]], axis=-1).astype(BF16)
    loras = [_pad_axis(w, 1, pad).astype(BF16)
             for w, pad in zip((rw_w_up, rw_a_up, rw_g_up), RW_LORA_PAD)]
    rw_mu_cols = _split_rw_cols(rw_mu)
    tail = [w.astype(BF16) for w in (w_br_rw, w_br_sb, w_br_hg, w_out, ffn2_wg, ffn2_wu, ffn2_wd,
                                     ple_gate, ple_proj)]

    h = x.reshape(n, D_MODEL)
    for i in range(DEPTH):
        def vec(a):
            return a[i].reshape(1, -1)

        h = _ffn_ln(h, *ffn1, i, vec(ln_g[:, 0]), vec(ln_b[:, 0]), min(FFN_TILE, n))
        rw_params = (vec(rw_mu_cols), vec(rw_w0), loras[0][i], vec(rw_a0), loras[1][i], loras[2][i],
                     vec(rw_k_k), vec(rw_k_a), vec(rw_r_k), vec(rw_gn_g), vec(rw_gn_b))
        q, k, v, gate, y_rw, y_hg = _proj_mix(h, w_proj, rw_params, hg_lb_raw, vec(hg_norm_g),
                                              i, seq, tm, hg_tile)
        y_sb = _sb_attention(q, k, v, bsz, seq, sb_blk)
        h = _layer_tail(h, y_rw, y_sb, y_hg, gate, p[i].reshape(n, PLE_DIM), tail, i,
                        ln_g[i, 1:], ln_b[i, 1:], min(TAIL_TILE, n))
    return h.reshape(bsz, seq, D_MODEL)
```
